```python
import math, functools
import jax, jax.numpy as jnp
from jax import lax
import numpy as np

D_MODEL = 1024
BATCH = 4
SEQ = 4096
DEPTH = 1
DEC_BATCH = 32
DEC_SEQ = 32
PAST_LEN = 4096

CHUNK = 64
WINDOW = 128
D_MIX = D_MODEL
ATTN_WIDTH = D_MIX // 2
GLA_WIDTH = D_MIX - ATTN_WIDTH
HEAD_DIM = 64
N_HEADS = ATTN_WIDTH // HEAD_DIM
N_KV_HEADS = 2
GQA_GROUP = N_HEADS // N_KV_HEADS
ROT_DIM = HEAD_DIM // 4
ROPE_THETA = 500000.0
GLA_HEADS = 4
GLA_DV = GLA_WIDTH // GLA_HEADS
GLA_DK = GLA_DV // 2
GLA_RANK = 16
GATE_TAU = 16.0
NORM_EPS = 1e-6
IN_SIZES = (N_HEADS * HEAD_DIM, N_KV_HEADS * HEAD_DIM, N_KV_HEADS * HEAD_DIM, ATTN_WIDTH,
            GLA_HEADS * GLA_DK, GLA_HEADS * GLA_DK, GLA_WIDTH, GLA_WIDTH, GLA_RANK)
D_IN = sum(IN_SIZES)

kernel_name = 'hymba_swa_sink_gla_streaming_step'


def _rms(x, w):
    x32 = x.astype(jnp.float32)
    y = x32 * lax.rsqrt(jnp.mean(x32 * x32, axis=-1, keepdims=True) + NORM_EPS)
    return (y * w.astype(jnp.float32)).astype(x.dtype)


def _rotary(x, pos):
    half = ROT_DIM // 2
    inv = ROPE_THETA ** (-jnp.arange(half, dtype=jnp.float32) * (2.0 / ROT_DIM))
    ang = pos.astype(jnp.float32)[:, None] * inv[None, :]
    cos = jnp.cos(ang)[:, None, :]
    sin = jnp.sin(ang)[:, None, :]
    xf = x.astype(jnp.float32)
    x1 = xf[..., :half]
    x2 = xf[..., half:ROT_DIM]
    out = jnp.concatenate([x1 * cos - x2 * sin, x1 * sin + x2 * cos, xf[..., ROT_DIM:]], axis=-1)
    return out.astype(x.dtype)


def _sink_attend(q, k, v, valid, sinks):
    B, N, Cq = q.shape[:3]
    s = jnp.einsum('bnqkgd,bnskd->bnkgqs', q, k).astype(jnp.float32) * (HEAD_DIM ** -0.5)
    s = jnp.where(valid[None, :, None, None, None, :], s, -1e30)
    sink = sinks.astype(jnp.float32).reshape(1, 1, N_KV_HEADS, GQA_GROUP, 1, 1)
    m = jnp.maximum(jnp.max(s, axis=-1, keepdims=True), sink)
    p = jnp.exp(s - m)
    probs = p / (jnp.sum(p, axis=-1, keepdims=True) + jnp.exp(sink - m))
    o = jnp.einsum('bnkgqs,bnskd->bnqkgd', probs.astype(v.dtype), v)
    return o.reshape(B, N * Cq, N_HEADS * HEAD_DIM)


def _attend_prompt(q, k, v, sinks):
    B, T = q.shape[:2]
    N = T // CHUNK
    nb = WINDOW // CHUNK
    qc = q.reshape(B, N, CHUNK, N_KV_HEADS, GQA_GROUP, HEAD_DIM)

    def band(t):
        tc = t.reshape(B, N, CHUNK, N_KV_HEADS, HEAD_DIM)
        tp = jnp.pad(tc, ((0, 0), (nb, 0), (0, 0), (0, 0), (0, 0)))
        return jnp.concatenate([tp[:, j:j + N] for j in range(nb + 1)], axis=2)

    key_chunk = jnp.arange(N)[:, None] - nb + jnp.arange((nb + 1) * CHUNK)[None, :] // CHUNK
    return _sink_attend(qc, band(k), band(v), key_chunk >= 0, sinks)


def _attend_sample(q, k, v, sinks, cache_k, cache_v):
    B, T = q.shape[:2]
    kk = jnp.concatenate([cache_k.astype(k.dtype), k], axis=1)[:, None]
    vv = jnp.concatenate([cache_v.astype(v.dtype), v], axis=1)[:, None]
    valid = jnp.ones((1, kk.shape[2]), dtype=bool)
    return _sink_attend(q.reshape(B, 1, T, N_KV_HEADS, GQA_GROUP, HEAD_DIM), kk, vv, valid, sinks)


def _gla_chunked(q, k, v, log_a, s0, chunk):
    B, T, H, DK = q.shape
    DV = v.shape[-1]
    N = T // chunk
    f = lambda t: t.astype(jnp.float32).reshape(B, N, chunk, H, t.shape[-1])
    q, k, v, g = f(q) * (GLA_DK ** -0.5), f(k), f(v), f(log_a)
    b = jnp.cumsum(g, axis=2)
    b_last = b[:, :, -1:]
    q_in = q * jnp.exp(b)
    A = jnp.einsum('bnihd,bnjhd->bnhij', q_in, k * jnp.exp(-b))
    causal = jnp.tril(jnp.ones((chunk, chunk), dtype=bool))
    A = jnp.where(causal, A, 0.0)
    o_intra = jnp.einsum('bnhij,bnjhe->bnihe', A, v)
    u = jnp.einsum('bnjhd,bnjhe->bnhde', k * jnp.exp(b_last - b), v)
    decay = jnp.exp(b_last[:, :, 0])

    def step(s, inp):
        dec, uu = inp
        return dec[..., None] * s + uu, s

    s_final, s_prev = lax.scan(step, s0.astype(jnp.float32), (jnp.moveaxis(decay, 1, 0), jnp.moveaxis(u, 1, 0)))
    o_inter = jnp.einsum('bnihd,bnhde->bnihe', q_in, jnp.moveaxis(s_prev, 0, 1))
    return (o_intra + o_inter).reshape(B, T, H, DV), s_final


def _layer(x, pos, attend, s0, gla_chunk, norm_pre_w, w_in, attn_sinks, w_gk_up, b_gk, gla_norm_w, w_out, norm_post_w):
    B, T, _ = x.shape
    h = _rms(x, norm_pre_w)
    z = h @ w_in
    idx = np.cumsum(IN_SIZES)[:-1].tolist()
    aq, ak, av, ag, gq, gk, gv, gg, glr = jnp.split(z, idx, axis=-1)
    aq = _rotary(aq.reshape(B, T, N_HEADS, HEAD_DIM), pos)
    ak = _rotary(ak.reshape(B, T, N_KV_HEADS, HEAD_DIM), pos)
    av = av.reshape(B, T, N_KV_HEADS, HEAD_DIM)
    attn = attend(aq, ak, av, attn_sinks) * jax.nn.silu(ag)
    log_a = jax.nn.log_sigmoid((glr @ w_gk_up + b_gk).astype(jnp.float32)) / GATE_TAU
    o, s_new = _gla_chunked(gq.reshape(B, T, GLA_HEADS, GLA_DK), gk.reshape(B, T, GLA_HEADS, GLA_DK),
                            gv.reshape(B, T, GLA_HEADS, GLA_DV), log_a.reshape(B, T, GLA_HEADS, GLA_DK), s0, gla_chunk)
    o = _rms(o.astype(x.dtype), gla_norm_w).reshape(B, T, GLA_WIDTH) * jax.nn.silu(gg)
    mix = jnp.concatenate([attn, o], axis=-1) @ w_out
    return x + _rms(mix, norm_post_w), ak, av, s_new


def setup_inputs(seed: int = 0) -> dict:
    key = jax.random.key(seed)
    ks = jax.random.split(key, 13)
    nrm = jax.random.normal
    f32 = jnp.float32
    return {
        'x_prompt': nrm(ks[0], (BATCH, SEQ, D_MODEL), f32),
        'x_sample': nrm(ks[1], (DEC_BATCH, DEC_SEQ, D_MODEL), f32),
        'cache_k': nrm(ks[2], (DEPTH, DEC_BATCH, WINDOW, N_KV_HEADS, HEAD_DIM), f32),
        'cache_v': nrm(ks[3], (DEPTH, DEC_BATCH, WINDOW, N_KV_HEADS, HEAD_DIM), f32),
        'state_gla': 0.5 * nrm(ks[4], (DEPTH, DEC_BATCH, GLA_HEADS, GLA_DK, GLA_DV), f32),
        'norm_pre_w': 1.0 + 0.01 * nrm(ks[5], (DEPTH, D_MODEL), f32),
        'w_in': nrm(ks[6], (DEPTH, D_MODEL, D_IN), f32) * (D_MODEL ** -0.5),
        'attn_sinks': 0.5 * nrm(ks[7], (DEPTH, N_HEADS), f32),
        'w_gk_up': nrm(ks[8], (DEPTH, GLA_RANK, GLA_HEADS * GLA_DK), f32) * (GLA_RANK ** -0.5),
        'b_gk': 0.1 * nrm(ks[9], (DEPTH, GLA_HEADS * GLA_DK), f32),
        'gla_norm_w': 1.0 + 0.01 * nrm(ks[10], (DEPTH, GLA_DV), f32),
        'w_out': nrm(ks[11], (DEPTH, D_MIX, D_MODEL), f32) * (D_MIX ** -0.5),
        'norm_post_w': 1.0 + 0.01 * nrm(ks[12], (DEPTH, D_MODEL), f32),
    }


def reference(x_prompt, x_sample, cache_k, cache_v, state_gla, norm_pre_w, w_in, attn_sinks, w_gk_up, b_gk, gla_norm_w, w_out, norm_post_w):
    B, T_p = x_prompt.shape[:2]
    T_s = x_sample.shape[1]
    pos_p = jnp.arange(T_p)
    pos_s = PAST_LEN + jnp.arange(T_s)
    y_p, y_s = x_prompt, x_sample
    kp_l, vp_l, sp_l, ks_l, vs_l, ss_l = [], [], [], [], [], []
    for l in range(DEPTH):
        w = (norm_pre_w[l], w_in[l], attn_sinks[l], w_gk_up[l], b_gk[l], gla_norm_w[l], w_out[l], norm_post_w[l])
        s0 = jnp.zeros((B, GLA_HEADS, GLA_DK, GLA_DV), jnp.float32)
        y_p, kp, vp, sp = _layer(y_p, pos_p, _attend_prompt, s0, CHUNK, *w)
        att_s = functools.partial(_attend_sample, cache_k=cache_k[l], cache_v=cache_v[l])
        y_s, k_s, v_s, s_s = _layer(y_s, pos_s, att_s, state_gla[l], T_s, *w)
        kp_l.append(kp[:, -WINDOW:])
        vp_l.append(vp[:, -WINDOW:])
        sp_l.append(sp.astype(x_prompt.dtype))
        ks_l.append(k_s)
        vs_l.append(v_s)
        ss_l.append(s_s.astype(state_gla.dtype))
    new_k_prompt = jnp.stack(kp_l)
    new_v_prompt = jnp.stack(vp_l)
    new_state_prompt = jnp.stack(sp_l)
    new_k_sample = jnp.stack(ks_l)
    new_v_sample = jnp.stack(vs_l)
    new_state_sample = jnp.stack(ss_l)
    return (y_p, y_s, new_k_prompt, new_v_prompt, new_state_prompt, new_k_sample, new_v_sample, new_state_sample)
```

```python
import functools

import jax
import jax.numpy as jnp
from jax import lax
from jax.experimental import pallas as pl
from jax.experimental.pallas import tpu as pltpu

D_MODEL = 1024
CHUNK = 64
WINDOW = 128
HEAD_DIM = 64
N_HEADS = 8
N_KV_HEADS = 2
ROT_DIM = 16
ROPE_THETA = 500000.0
GLA_HEADS = 4
GLA_DV = 128
GLA_DK = 64
GLA_RANK = 16
GATE_TAU = 16.0
NORM_EPS = 1e-6
PAST_LEN = 4096

LANES = 128
ATTN_W = N_HEADS * HEAD_DIM
KV_W = N_KV_HEADS * HEAD_DIM
GQK_W = GLA_HEADS * GLA_DK
GV_W = GLA_HEADS * GLA_DV
OFF_AQ = 0
OFF_AK = OFF_AQ + ATTN_W
OFF_AV = OFF_AK + KV_W
OFF_AG = OFF_AV + KV_W
OFF_GQ = OFF_AG + ATTN_W
OFF_GK = OFF_GQ + GQK_W
OFF_GV = OFF_GK + GQK_W
OFF_GG = OFF_GV + GV_W
OFF_LR = OFF_GG + GV_W
D_IN = OFF_LR + GLA_RANK
D_IN_PAD = OFF_LR + LANES

PROMPT_TILE = 256
SAMPLE_SEQS = 8
VMEM_LIMIT = 48 * 1024 * 1024

NEG = -1e30


def _dot(a, b):
    return jnp.dot(a, b, preferred_element_type=jnp.float32)


def _dot_nt(a, b):
    return lax.dot_general(a, b, (((1,), (1,)), ((), ())), preferred_element_type=jnp.float32)


def _bf(x):
    return x.astype(jnp.bfloat16)


def _lane_masks():
    lane = lax.broadcasted_iota(jnp.int32, (1, LANES), 1)
    return lane < HEAD_DIM, lane >= HEAD_DIM


def _silu(x):
    return x * (1.0 / (1.0 + jnp.exp(-x)))


def _log_sigmoid(x):
    return jnp.minimum(x, 0.0) - jnp.log1p(jnp.exp(-jnp.abs(x)))


def _rms_scale(x):
    return lax.rsqrt(jnp.mean(x * x, axis=-1, keepdims=True) + NORM_EPS)


def _chunk_cumsum(g, chunk):
    row = lax.broadcasted_iota(jnp.int32, g.shape, 0) & (chunk - 1)
    s = 1
    while s < chunk:
        g = g + jnp.where(row >= s, pltpu.roll(g, s, 0), 0.0)
        s *= 2
    return g


def _rotary(xcol, cos_t, sin_dn, sin_up):
    return (xcol * cos_t + pltpu.roll(xcol, ROT_DIM // 2, 1) * sin_dn
            + pltpu.roll(xcol, LANES - ROT_DIM // 2, 1) * sin_up)


def _project(x, cos_t, sin_dn, sin_up, wpre_ref, win_ref, wup_ref, bgk_ref, chunk):
    h = _bf(x * _rms_scale(x) * wpre_ref[...])

    def seg(off, width):
        return _dot(h, win_ref[:, off:off + width])

    q = seg(OFF_AQ, ATTN_W)
    qcols = [_rotary(q[:, j * LANES:(j + 1) * LANES], cos_t, sin_dn, sin_up) * (HEAD_DIM ** -0.5)
             for j in range(ATTN_W // LANES)]
    k = _rotary(seg(OFF_AK, KV_W), cos_t, sin_dn, sin_up)
    v = seg(OFF_AV, KV_W)
    ag = seg(OFF_AG, ATTN_W)
    gq = seg(OFF_GQ, GQK_W)
    gk = seg(OFF_GK, GQK_W)
    gv = _bf(seg(OFF_GV, GV_W))
    gg = seg(OFF_GG, GV_W)
    lr = _bf(seg(OFF_LR, LANES))
    log_a = _log_sigmoid(_dot(lr, wup_ref[...]) + bgk_ref[...]) * (1.0 / GATE_TAU)
    b = _chunk_cumsum(log_a, chunk)
    return dict(qcols=qcols, k=k, v=v, ag=ag, gq=gq, gk=gk, gv=gv, gg=gg, b=b)


def _kv_variants(k, v):
    m_lo, m_hi = _lane_masks()
    k_sw = pltpu.roll(k, HEAD_DIM, 1)
    v_sw = pltpu.roll(v, HEAD_DIM, 1)
    return (_bf(k), _bf(k_sw),
            _bf(jnp.where(m_lo, v, 0.0)), _bf(jnp.where(m_hi, v_sw, 0.0)),
            _bf(jnp.where(m_lo, v_sw, 0.0)), _bf(jnp.where(m_hi, v, 0.0)))


def _attend(q4, k_a, k_b, vs, sinks_ref, n_invalid):
    c = q4[0].shape[0]
    m_lo, m_hi = _lane_masks()
    lo = [jnp.where(m_lo, q, 0.0) for q in q4]
    hi = [jnp.where(m_hi, q, 0.0) for q in q4]
    s_a = _dot_nt(_bf(jnp.concatenate([lo[0], lo[1], hi[2], hi[3]], axis=0)), k_a)
    s_b = _dot_nt(_bf(jnp.concatenate([hi[0], hi[1], lo[2], lo[3]], axis=0)), k_b)
    if n_invalid is not None:
        key = lax.broadcasted_iota(jnp.int32, (1, s_a.shape[1]), 1)
        dead = key < n_invalid
        s_a = jnp.where(dead, NEG, s_a)
        s_b = jnp.where(dead, NEG, s_b)
    where = {0: (s_a, 0), 2: (s_a, 1), 5: (s_a, 2), 7: (s_a, 3),
             1: (s_b, 0), 3: (s_b, 1), 4: (s_b, 2), 6: (s_b, 3)}
    v0l, v0r, v1l, v1r = vs
    cols = []
    for j in range(4):
        acc = None
        for h in (2 * j, 2 * j + 1):
            src, blk = where[h]
            s = src[blk * c:(blk + 1) * c]
            sink = sinks_ref[0, h]
            m = jnp.maximum(jnp.max(s, axis=-1, keepdims=True), sink)
            p = jnp.exp(s - m)
            denom = jnp.sum(p, axis=-1, keepdims=True) + jnp.exp(sink - m)
            probs = _bf(p * (1.0 / denom))
            if h < 4:
                vv = v0l if h % 2 == 0 else v0r
            else:
                vv = v1l if h % 2 == 0 else v1r
            o = _dot(probs, vv)
            acc = o if acc is None else acc + o
        cols.append(acc)
    return jnp.concatenate(cols, axis=1)


def _gla_pair(qp, ko, kd, decay_row, vwide, s_pair):
    c = qp.shape[0]
    m_lo, m_hi = _lane_masks()
    lhs = _bf(jnp.concatenate([jnp.where(m_lo, qp, 0.0), jnp.where(m_hi, qp, 0.0)], axis=0))
    a2 = _dot_nt(lhs, _bf(ko))
    ri = lax.broadcasted_iota(jnp.int32, a2.shape, 0) & (c - 1)
    ci = lax.broadcasted_iota(jnp.int32, a2.shape, 1)
    a2 = _bf(jnp.where(ri >= ci, a2, 0.0))
    inter = _dot(lhs, _bf(s_pair))
    o0 = _dot(a2[:c], vwide[:, :GLA_DV]) + inter[:c]
    o1 = _dot(a2[c:], vwide[:, GLA_DV:]) + inter[c:]
    pad = LANES - c
    kd_t = jnp.concatenate([kd, jnp.zeros((pad, LANES), jnp.float32)], axis=0).T
    v_pad = jnp.concatenate([vwide, jnp.zeros((pad, 2 * GLA_DV), jnp.bfloat16)], axis=0)
    uw = _dot(_bf(kd_t), v_pad)
    u_pair = jnp.concatenate([uw[:GLA_DK, :GLA_DV], uw[GLA_DK:, GLA_DV:]], axis=0)
    dcol = jnp.broadcast_to(decay_row, (LANES, LANES)).T
    return o0, o1, dcol * s_pair + u_pair


def _gla_chunk(pr, r0, c, get_state, set_state, gnw_ref, mix_ref):
    rows = slice(r0, r0 + c)
    b = pr["b"][rows]
    b_last = b[c - 1:c]
    q_in = pr["gq"][rows] * (GLA_DK ** -0.5) * jnp.exp(b)
    k_out = pr["gk"][rows] * jnp.exp(-b)
    k_dec = pr["gk"][rows] * jnp.exp(b_last - b)
    decay = jnp.exp(b_last)
    gnw = gnw_ref[...]
    for p in range(GLA_HEADS // 2):
        ln = slice(p * LANES, (p + 1) * LANES)
        vwide = pr["gv"][rows, p * 2 * GLA_DV:(p + 1) * 2 * GLA_DV]
        o0, o1, s_new = _gla_pair(q_in[:, ln], k_out[:, ln], k_dec[:, ln], decay[:, ln], vwide, get_state(p))
        set_state(p, s_new)
        for i, o in enumerate((o0, o1)):
            hh = 2 * p + i
            gate = _silu(pr["gg"][rows, hh * GLA_DV:(hh + 1) * GLA_DV])
            y = o * _rms_scale(o) * gnw * gate
            mix_ref[rows, ATTN_W + hh * GLA_DV:ATTN_W + (hh + 1) * GLA_DV] = _bf(y)


def _finish(x, mix_ref, wout_ref, wpost_ref):
    mix = _dot(mix_ref[...], wout_ref[...])
    return x + mix * _rms_scale(mix) * wpost_ref[...]


def _prompt_kernel(sinks_ref, x_ref, cos_ref, sdn_ref, sup_ref, wpre_ref, win_ref, wup_ref, bgk_ref,
                   gnw_ref, wout_ref, wpost_ref,
                   y_ref, kout_ref, vout_ref, sout_ref,
                   ka_ref, kb_ref, v0l_ref, v0r_ref, v1l_ref, v1r_ref, s_ref, mix_ref):
    t = pl.program_id(1)
    tt = x_ref.shape[1]
    bufs = (ka_ref, kb_ref, v0l_ref, v0r_ref, v1l_ref, v1r_ref)

    @pl.when(t == 0)
    def _():
        for r in bufs:
            r[0:WINDOW, :] = jnp.zeros((WINDOW, LANES), jnp.bfloat16)
        s_ref[...] = jnp.zeros_like(s_ref)

    x = x_ref[0]
    pr = _project(x, cos_ref[...], sdn_ref[...], sup_ref[...], wpre_ref, win_ref, wup_ref, bgk_ref, CHUNK)
    kout_ref[0] = pr["k"][tt - WINDOW:]
    vout_ref[0] = pr["v"][tt - WINDOW:]
    for r, val in zip(bufs, _kv_variants(pr["k"], pr["v"])):
        r[WINDOW:WINDOW + tt, :] = val

    def get_state(p):
        return s_ref[p * LANES:(p + 1) * LANES, :]

    def set_state(p, val):
        s_ref[p * LANES:(p + 1) * LANES, :] = val

    win = WINDOW + CHUNK
    for c in range(tt // CHUNK):
        r0 = c * CHUNK
        rows = slice(r0, r0 + CHUNK)
        q4 = [qc[rows] for qc in pr["qcols"]]
        n_invalid = WINDOW - (t * tt + r0) if r0 < WINDOW else None
        attn = _attend(q4, ka_ref[r0:r0 + win, :], kb_ref[r0:r0 + win, :],
                       [r[r0:r0 + win, :] for r in bufs[2:]], sinks_ref, n_invalid)
        mix_ref[rows, 0:ATTN_W] = _bf(attn * _silu(pr["ag"][rows]))
        _gla_chunk(pr, r0, CHUNK, get_state, set_state, gnw_ref, mix_ref)

    y_ref[0] = _finish(x, mix_ref, wout_ref, wpost_ref)
    sout_ref[0] = s_ref[...]
    for r in bufs:
        r[0:WINDOW, :] = r[tt:tt + WINDOW, :]


def _sample_kernel(sinks_ref, x_ref, cos_ref, sdn_ref, sup_ref, ck_ref, cv_ref, s0_ref, wpre_ref, win_ref,
                   wup_ref, bgk_ref, gnw_ref, wout_ref, wpost_ref,
                   y_ref, kout_ref, vout_ref, sout_ref, mix_ref):
    nb = ck_ref.shape[0]
    c = x_ref.shape[0] // nb
    x = x_ref[...]
    pr = _project(x, cos_ref[...], sdn_ref[...], sup_ref[...], wpre_ref, win_ref, wup_ref, bgk_ref, c)
    kout_ref[...] = pr["k"]
    vout_ref[...] = pr["v"]
    for i in range(nb):
        r0 = i * c
        rows = slice(r0, r0 + c)
        kfull = jnp.concatenate([ck_ref[i], pr["k"][rows]], axis=0)
        vfull = jnp.concatenate([cv_ref[i], pr["v"][rows]], axis=0)
        kv = _kv_variants(kfull, vfull)
        q4 = [qc[rows] for qc in pr["qcols"]]
        attn = _attend(q4, kv[0], kv[1], kv[2:], sinks_ref, None)
        mix_ref[rows, 0:ATTN_W] = _bf(attn * _silu(pr["ag"][rows]))

        def get_state(p, i=i):
            return s0_ref[i, p * LANES:(p + 1) * LANES, :]

        def set_state(p, val, i=i):
            sout_ref[i, p * LANES:(p + 1) * LANES, :] = val

        _gla_chunk(pr, r0, c, get_state, set_state, gnw_ref, mix_ref)
    y_ref[...] = _finish(x, mix_ref, wout_ref, wpost_ref)


def _rotary_tables(pos):
    half = ROT_DIM // 2
    inv = ROPE_THETA ** (-jnp.arange(half, dtype=jnp.float32) * (2.0 / ROT_DIM))
    ang = pos.astype(jnp.float32)[:, None] * inv[None, :]
    cos, sin = jnp.cos(ang), jnp.sin(ang)
    t = pos.shape[0]
    one = jnp.ones((t, HEAD_DIM - ROT_DIM), jnp.float32)
    zero = jnp.zeros((t, HEAD_DIM - ROT_DIM), jnp.float32)
    z8 = jnp.zeros((t, half), jnp.float32)
    cos_t = jnp.concatenate([cos, cos, one], axis=1)
    sin_dn = jnp.concatenate([z8, sin, zero], axis=1)
    sin_up = jnp.concatenate([-sin, z8, zero], axis=1)
    rep = LANES // HEAD_DIM
    return tuple(jnp.tile(a, (1, rep)) for a in (cos_t, sin_dn, sin_up))


def _const_spec(shape, n_grid):
    zeros = (0,) * len(shape)
    if n_grid == 1:
        return pl.BlockSpec(shape, lambda i: zeros)
    return pl.BlockSpec(shape, lambda i, j: zeros)


def kernel(x_prompt, x_sample, cache_k, cache_v, state_gla, norm_pre_w, w_in, attn_sinks, w_gk_up, b_gk,
           gla_norm_w, w_out, norm_post_w):
    bsz, t_p, _ = x_prompt.shape
    dec_b, t_s, _ = x_sample.shape
    assert w_in.shape[0] == 1, "single layer"
    assert t_p % PROMPT_TILE == 0 and PROMPT_TILE % CHUNK == 0 and PROMPT_TILE >= WINDOW
    assert dec_b % SAMPLE_SEQS == 0 and t_s & (t_s - 1) == 0

    win = _bf(jnp.pad(w_in[0], ((0, 0), (0, D_IN_PAD - D_IN))))
    wup = _bf(jnp.pad(w_gk_up[0], ((0, LANES - GLA_RANK), (0, 0))))
    wout = _bf(w_out[0])
    wpre = norm_pre_w[0][None, :]
    wpost = norm_post_w[0][None, :]
    bgk = b_gk[0][None, :]
    gnw = gla_norm_w[0][None, :]
    sinks = attn_sinks[0][None, :]
    smem = pl.BlockSpec(memory_space=pltpu.SMEM)

    weights_specs = lambda n: [
        _const_spec((1, D_MODEL), n), _const_spec((D_MODEL, D_IN_PAD), n), _const_spec((LANES, GQK_W), n),
        _const_spec((1, GQK_W), n), _const_spec((1, GLA_DV), n), _const_spec((D_MODEL, D_MODEL), n),
        _const_spec((1, D_MODEL), n)]

    tt = PROMPT_TILE
    cos_p, sdn_p, sup_p = _rotary_tables(jnp.arange(t_p))
    tab_spec = pl.BlockSpec((tt, LANES), lambda b, t: (t, 0))
    y_p, k_p, v_p, s_p = pl.pallas_call(
        _prompt_kernel,
        grid=(bsz, t_p // tt),
        in_specs=[smem, pl.BlockSpec((1, tt, D_MODEL), lambda b, t: (b, t, 0)), tab_spec, tab_spec, tab_spec]
        + weights_specs(2),
        out_specs=[pl.BlockSpec((1, tt, D_MODEL), lambda b, t: (b, t, 0)),
                   pl.BlockSpec((1, WINDOW, KV_W), lambda b, t: (b, 0, 0)),
                   pl.BlockSpec((1, WINDOW, KV_W), lambda b, t: (b, 0, 0)),
                   pl.BlockSpec((1, GLA_HEADS * GLA_DK, GLA_DV), lambda b, t: (b, 0, 0))],
        out_shape=[jax.ShapeDtypeStruct((bsz, t_p, D_MODEL), jnp.float32),
                   jax.ShapeDtypeStruct((bsz, WINDOW, KV_W), jnp.float32),
                   jax.ShapeDtypeStruct((bsz, WINDOW, KV_W), jnp.float32),
                   jax.ShapeDtypeStruct((bsz, GLA_HEADS * GLA_DK, GLA_DV), jnp.float32)],
        scratch_shapes=[pltpu.VMEM((WINDOW + tt, LANES), jnp.bfloat16) for _ in range(6)]
        + [pltpu.VMEM((GLA_HEADS * GLA_DK, GLA_DV), jnp.float32), pltpu.VMEM((tt, D_MODEL), jnp.bfloat16)],
        compiler_params=pltpu.CompilerParams(dimension_semantics=("arbitrary", "arbitrary"),
                                             vmem_limit_bytes=VMEM_LIMIT),
        name="prompt_layer",
    )(sinks, x_prompt, cos_p, sdn_p, sup_p, wpre, win, wup, bgk, gnw, wout, wpost)

    nb = SAMPLE_SEQS
    rows = nb * t_s
    tabs = [jnp.tile(a, (nb, 1)) for a in _rotary_tables(PAST_LEN + jnp.arange(t_s))]
    tab_spec = pl.BlockSpec((rows, LANES), lambda i: (0, 0))
    seq3 = lambda d1, d2: pl.BlockSpec((nb, d1, d2), lambda i: (i, 0, 0))
    tok = lambda d: pl.BlockSpec((rows, d), lambda i: (i, 0))
    sdim = GLA_HEADS * GLA_DK
    y_s, k_s, v_s, s_s = pl.pallas_call(
        _sample_kernel,
        grid=(dec_b // nb,),
        in_specs=[smem, tok(D_MODEL), tab_spec, tab_spec, tab_spec, seq3(WINDOW, KV_W), seq3(WINDOW, KV_W),
                  seq3(sdim, GLA_DV)] + weights_specs(1),
        out_specs=[tok(D_MODEL), tok(KV_W), tok(KV_W), seq3(sdim, GLA_DV)],
        out_shape=[jax.ShapeDtypeStruct((dec_b * t_s, D_MODEL), jnp.float32),
                   jax.ShapeDtypeStruct((dec_b * t_s, KV_W), jnp.float32),
                   jax.ShapeDtypeStruct((dec_b * t_s, KV_W), jnp.float32),
                   jax.ShapeDtypeStruct((dec_b, sdim, GLA_DV), jnp.float32)],
        scratch_shapes=[pltpu.VMEM((rows, D_MODEL), jnp.bfloat16)],
        compiler_params=pltpu.CompilerParams(dimension_semantics=("arbitrary",), vmem_limit_bytes=VMEM_LIMIT),
        name="sample_layer",
    )(sinks, x_sample.reshape(dec_b * t_s, D_MODEL), *tabs,
      cache_k[0].reshape(dec_b, WINDOW, KV_W), cache_v[0].reshape(dec_b, WINDOW, KV_W),
      state_gla[0].reshape(dec_b, sdim, GLA_DV), wpre, win, wup, bgk, gnw, wout, wpost)

    kv5 = lambda a, b, t: a.reshape(1, b, t, N_KV_HEADS, HEAD_DIM)
    st5 = lambda a, b: a.reshape(1, b, GLA_HEADS, GLA_DK, GLA_DV)
    return (y_p, y_s.reshape(dec_b, t_s, D_MODEL),
            kv5(k_p, bsz, WINDOW), kv5(v_p, bsz, WINDOW), st5(s_p, bsz),
            kv5(k_s, dec_b, t_s), kv5(v_s, dec_b, t_s), st5(s_s, dec_b))
```

```python
import jax
import jax.numpy as jnp
from jax import lax
from jax.experimental import pallas as pl
from jax.experimental.pallas import tpu as pltpu

D_MODEL = 1024
CHUNK = 64
WINDOW = 128
HEAD_DIM = 64
N_HEADS = 8
N_KV_HEADS = 2
ROT_DIM = 16
ROPE_THETA = 500000.0
GLA_HEADS = 4
GLA_DV = 128
GLA_DK = 64
GLA_RANK = 16
GATE_TAU = 16.0
NORM_EPS = 1e-6
PAST_LEN = 4096

LANES = 128
ATTN_W = N_HEADS * HEAD_DIM
KV_W = N_KV_HEADS * HEAD_DIM
GQK_W = GLA_HEADS * GLA_DK
GV_W = GLA_HEADS * GLA_DV
OFF_AQ = 0
OFF_AK = OFF_AQ + ATTN_W
OFF_AV = OFF_AK + KV_W
OFF_AG = OFF_AV + KV_W
OFF_GQ = OFF_AG + ATTN_W
OFF_GK = OFF_GQ + GQK_W
OFF_GV = OFF_GK + GQK_W
OFF_GG = OFF_GV + GV_W
OFF_LR = OFF_GG + GV_W
D_IN = OFF_LR + GLA_RANK
D_IN_PAD = OFF_LR + LANES

PROMPT_TILE = 256
SAMPLE_SEQS = 8
VMEM_LIMIT = 48 * 1024 * 1024
STAGE_SKEW = 0

NEG = -1e30


def _dot(a, b):
    return jnp.dot(a, b, preferred_element_type=jnp.float32)


def _dot_nt(a, b):
    return lax.dot_general(a, b, (((1,), (1,)), ((), ())), preferred_element_type=jnp.float32)


def _bf(x):
    return x.astype(jnp.bfloat16)


def _lane_masks():
    lane = lax.broadcasted_iota(jnp.int32, (1, LANES), 1)
    return lane < HEAD_DIM, lane >= HEAD_DIM


def _silu(x):
    return x * (1.0 / (1.0 + jnp.exp(-x)))


def _log_sigmoid(x):
    return jnp.minimum(x, 0.0) - jnp.log1p(jnp.exp(-jnp.abs(x)))


def _rms_scale(x):
    return lax.rsqrt(jnp.mean(x * x, axis=-1, keepdims=True) + NORM_EPS)


def _chunk_cumsum(g, chunk):
    row = lax.broadcasted_iota(jnp.int32, g.shape, 0) & (chunk - 1)
    s = 1
    while s < chunk:
        g = g + jnp.where(row >= s, pltpu.roll(g, s, 0), 0.0)
        s *= 2
    return g


def _rotary(xcol, cos_t, sin_dn, sin_up):
    return (xcol * cos_t + pltpu.roll(xcol, ROT_DIM // 2, 1) * sin_dn
            + pltpu.roll(xcol, LANES - ROT_DIM // 2, 1) * sin_up)


def _project(x, cos_t, sin_dn, sin_up, wpre_ref, win_ref, wup_ref, bgk_ref, chunk):
    h = _bf(x * _rms_scale(x) * wpre_ref[...])

    def seg(off, width):
        return _dot(h, win_ref[:, off:off + width])

    q = seg(OFF_AQ, ATTN_W)
    qcols = [_rotary(q[:, j * LANES:(j + 1) * LANES], cos_t, sin_dn, sin_up) * (HEAD_DIM ** -0.5)
             for j in range(ATTN_W // LANES)]
    k = _rotary(seg(OFF_AK, KV_W), cos_t, sin_dn, sin_up)
    v = seg(OFF_AV, KV_W)
    ag = seg(OFF_AG, ATTN_W)
    gq = seg(OFF_GQ, GQK_W)
    gk = seg(OFF_GK, GQK_W)
    gv = _bf(seg(OFF_GV, GV_W))
    gg = seg(OFF_GG, GV_W)
    lr = _bf(seg(OFF_LR, LANES))
    log_a = _log_sigmoid(_dot(lr, wup_ref[...]) + bgk_ref[...]) * (1.0 / GATE_TAU)
    b = _chunk_cumsum(log_a, chunk)
    return dict(qcols=qcols, k=k, v=v, ag=ag, gq=gq, gk=gk, gv=gv, gg=gg, b=b)


def _kv_variants(k, v):
    m_lo, m_hi = _lane_masks()
    k_sw = pltpu.roll(k, HEAD_DIM, 1)
    v_sw = pltpu.roll(v, HEAD_DIM, 1)
    return (_bf(k), _bf(k_sw),
            _bf(jnp.where(m_lo, v, 0.0)), _bf(jnp.where(m_hi, v_sw, 0.0)),
            _bf(jnp.where(m_lo, v_sw, 0.0)), _bf(jnp.where(m_hi, v, 0.0)))


def _run_staged(gens, skew):
    done = [False] * len(gens)
    tick = 0
    while not all(done):
        for i, g in enumerate(gens):
            if done[i] or tick < i * skew:
                continue
            try:
                next(g)
            except StopIteration:
                done[i] = True
        tick += 1


def _attend_stages(q4, k_a, k_b, vs, sinks_ref, n_invalid, emit):
    c = q4[0].shape[0]
    m_lo, m_hi = _lane_masks()
    lo = [jnp.where(m_lo, q, 0.0) for q in q4]
    hi = [jnp.where(m_hi, q, 0.0) for q in q4]
    s_a = _dot_nt(_bf(jnp.concatenate([lo[0], lo[1], hi[2], hi[3]], axis=0)), k_a)
    s_b = _dot_nt(_bf(jnp.concatenate([hi[0], hi[1], lo[2], lo[3]], axis=0)), k_b)
    yield
    if n_invalid is not None:
        key = lax.broadcasted_iota(jnp.int32, (1, s_a.shape[1]), 1)
        dead = key < n_invalid
        s_a = jnp.where(dead, NEG, s_a)
        s_b = jnp.where(dead, NEG, s_b)
    where = {0: (s_a, 0), 2: (s_a, 1), 5: (s_a, 2), 7: (s_a, 3),
             1: (s_b, 0), 3: (s_b, 1), 4: (s_b, 2), 6: (s_b, 3)}
    scores, maxes = [], []
    for h in range(N_HEADS):
        src, blk = where[h]
        s = src[blk * c:(blk + 1) * c]
        scores.append(s)
        maxes.append(jnp.maximum(jnp.max(s, axis=-1, keepdims=True), sinks_ref[0, h]))
    yield
    v0l, v0r, v1l, v1r = vs
    outs = []
    for h in range(N_HEADS):
        p = jnp.exp(scores[h] - maxes[h])
        denom = jnp.sum(p, axis=-1, keepdims=True) + jnp.exp(sinks_ref[0, h] - maxes[h])
        probs = _bf(p * (1.0 / denom))
        if h < 4:
            vv = v0l if h % 2 == 0 else v0r
        else:
            vv = v1l if h % 2 == 0 else v1r
        outs.append(_dot(probs, vv))
    yield
    emit(jnp.concatenate([outs[2 * j] + outs[2 * j + 1] for j in range(4)], axis=1))


def _gla_stages(pr, r0, c, state, gnw_ref, mix_ref):
    rows = slice(r0, r0 + c)
    m_lo, m_hi = _lane_masks()
    b = pr["b"][rows]
    b_last = b[c - 1:c]
    q_in = pr["gq"][rows] * (GLA_DK ** -0.5) * jnp.exp(b)
    k_out = pr["gk"][rows] * jnp.exp(-b)
    k_dec = pr["gk"][rows] * jnp.exp(b_last - b)
    decay = jnp.exp(b_last)
    pad = LANES - c
    lhs, a2, uw, dcol, vw = [], [], [], [], []
    for p in range(GLA_HEADS // 2):
        ln = slice(p * LANES, (p + 1) * LANES)
        qp = q_in[:, ln]
        vwide = pr["gv"][rows, p * 2 * GLA_DV:(p + 1) * 2 * GLA_DV]
        lhs.append(_bf(jnp.concatenate([jnp.where(m_lo, qp, 0.0), jnp.where(m_hi, qp, 0.0)], axis=0)))
        a2.append(_dot_nt(lhs[p], _bf(k_out[:, ln])))
        kd_t = jnp.concatenate([k_dec[:, ln], jnp.zeros((pad, LANES), jnp.float32)], axis=0).T
        v_pad = jnp.concatenate([vwide, jnp.zeros((pad, 2 * GLA_DV), jnp.bfloat16)], axis=0)
        uw.append(_dot(_bf(kd_t), v_pad))
        dcol.append(jnp.broadcast_to(decay[:, ln], (LANES, LANES)).T)
        vw.append(vwide)
    yield
    outs = []
    ri = lax.broadcasted_iota(jnp.int32, a2[0].shape, 0) & (c - 1)
    ci = lax.broadcasted_iota(jnp.int32, a2[0].shape, 1)
    for p in range(GLA_HEADS // 2):
        am = _bf(jnp.where(ri >= ci, a2[p], 0.0))
        inter = _dot(lhs[p], _bf(state[p]))
        outs.append(_dot(am[:c], vw[p][:, :GLA_DV]) + inter[:c])
        outs.append(_dot(am[c:], vw[p][:, GLA_DV:]) + inter[c:])
        u_pair = jnp.concatenate([uw[p][:GLA_DK, :GLA_DV], uw[p][GLA_DK:, GLA_DV:]], axis=0)
        state[p] = dcol[p] * state[p] + u_pair
    yield
    gnw = gnw_ref[...]
    for hh, o in enumerate(outs):
        gate = _silu(pr["gg"][rows, hh * GLA_DV:(hh + 1) * GLA_DV])
        y = o * _rms_scale(o) * gnw * gate
        mix_ref[rows, ATTN_W + hh * GLA_DV:ATTN_W + (hh + 1) * GLA_DV] = _bf(y)


def _chunk_stages(attn_gen, gla_gen):
    live = [attn_gen, gla_gen]
    while live:
        for g in list(live):
            try:
                next(g)
            except StopIteration:
                live.remove(g)
        yield


def _finish(x, mix_ref, wout_ref, wpost_ref):
    mix = _dot(mix_ref[...], wout_ref[...])
    return x + mix * _rms_scale(mix) * wpost_ref[...]


def _prompt_kernel(sinks_ref, x_ref, cos_ref, sdn_ref, sup_ref, wpre_ref, win_ref, wup_ref, bgk_ref,
                   gnw_ref, wout_ref, wpost_ref,
                   y_ref, kout_ref, vout_ref, sout_ref,
                   ka_ref, kb_ref, v0l_ref, v0r_ref, v1l_ref, v1r_ref, s_ref, mix_ref):
    t = pl.program_id(1)
    tt = x_ref.shape[1]
    bufs = (ka_ref, kb_ref, v0l_ref, v0r_ref, v1l_ref, v1r_ref)

    @pl.when(t == 0)
    def _():
        for r in bufs:
            r[0:WINDOW, :] = jnp.zeros((WINDOW, LANES), jnp.bfloat16)
        s_ref[...] = jnp.zeros_like(s_ref)

    x = x_ref[0]
    pr = _project(x, cos_ref[...], sdn_ref[...], sup_ref[...], wpre_ref, win_ref, wup_ref, bgk_ref, CHUNK)
    kout_ref[0] = pr["k"][tt - WINDOW:]
    vout_ref[0] = pr["v"][tt - WINDOW:]
    for r, val in zip(bufs, _kv_variants(pr["k"], pr["v"])):
        r[WINDOW:WINDOW + tt, :] = val

    state = [s_ref[p * LANES:(p + 1) * LANES, :] for p in range(GLA_HEADS // 2)]
    win = WINDOW + CHUNK
    gens = []
    for c in range(tt // CHUNK):
        r0 = c * CHUNK
        rows = slice(r0, r0 + CHUNK)

        def emit(attn, rows=rows):
            mix_ref[rows, 0:ATTN_W] = _bf(attn * _silu(pr["ag"][rows]))

        n_invalid = WINDOW - (t * tt + r0) if r0 < WINDOW else None
        attn_gen = _attend_stages([qc[rows] for qc in pr["qcols"]], ka_ref[r0:r0 + win, :], kb_ref[r0:r0 + win, :],
                                  [r[r0:r0 + win, :] for r in bufs[2:]], sinks_ref, n_invalid, emit)
        gens.append(_chunk_stages(attn_gen, _gla_stages(pr, r0, CHUNK, state, gnw_ref, mix_ref)))
    _run_staged(gens, STAGE_SKEW)

    y_ref[0] = _finish(x, mix_ref, wout_ref, wpost_ref)
    for p in range(GLA_HEADS // 2):
        s_ref[p * LANES:(p + 1) * LANES, :] = state[p]
        sout_ref[0, p * LANES:(p + 1) * LANES, :] = state[p]
    for r in bufs:
        r[0:WINDOW, :] = r[tt:tt + WINDOW, :]


def _sample_kernel(sinks_ref, x_ref, cos_ref, sdn_ref, sup_ref, ck_ref, cv_ref, s0_ref, wpre_ref, win_ref,
                   wup_ref, bgk_ref, gnw_ref, wout_ref, wpost_ref,
                   y_ref, kout_ref, vout_ref, sout_ref, mix_ref):
    nb = ck_ref.shape[0]
    c = x_ref.shape[0] // nb
    x = x_ref[...]
    pr = _project(x, cos_ref[...], sdn_ref[...], sup_ref[...], wpre_ref, win_ref, wup_ref, bgk_ref, c)
    kout_ref[...] = pr["k"]
    vout_ref[...] = pr["v"]
    gens, states = [], []
    for i in range(nb):
        r0 = i * c
        rows = slice(r0, r0 + c)

        def emit(attn, rows=rows):
            mix_ref[rows, 0:ATTN_W] = _bf(attn * _silu(pr["ag"][rows]))

        kv = _kv_variants(jnp.concatenate([ck_ref[i], pr["k"][rows]], axis=0),
                          jnp.concatenate([cv_ref[i], pr["v"][rows]], axis=0))
        attn_gen = _attend_stages([qc[rows] for qc in pr["qcols"]], kv[0], kv[1], kv[2:], sinks_ref, None, emit)
        states.append([s0_ref[i, p * LANES:(p + 1) * LANES, :] for p in range(GLA_HEADS // 2)])
        gens.append(_chunk_stages(attn_gen, _gla_stages(pr, r0, c, states[i], gnw_ref, mix_ref)))
    _run_staged(gens, STAGE_SKEW)
    for i in range(nb):
        for p in range(GLA_HEADS // 2):
            sout_ref[i, p * LANES:(p + 1) * LANES, :] = states[i][p]
    y_ref[...] = _finish(x, mix_ref, wout_ref, wpost_ref)


def _rotary_tables(pos):
    half = ROT_DIM // 2
    inv = ROPE_THETA ** (-jnp.arange(half, dtype=jnp.float32) * (2.0 / ROT_DIM))
    ang = pos.astype(jnp.float32)[:, None] * inv[None, :]
    cos, sin = jnp.cos(ang), jnp.sin(ang)
    t = pos.shape[0]
    one = jnp.ones((t, HEAD_DIM - ROT_DIM), jnp.float32)
    zero = jnp.zeros((t, HEAD_DIM - ROT_DIM), jnp.float32)
    z8 = jnp.zeros((t, half), jnp.float32)
    cos_t = jnp.concatenate([cos, cos, one], axis=1)
    sin_dn = jnp.concatenate([z8, sin, zero], axis=1)
    sin_up = jnp.concatenate([-sin, z8, zero], axis=1)
    rep = LANES // HEAD_DIM
    return tuple(jnp.tile(a, (1, rep)) for a in (cos_t, sin_dn, sin_up))


def _const_spec(shape, n_grid):
    zeros = (0,) * len(shape)
    if n_grid == 1:
        return pl.BlockSpec(shape, lambda i: zeros)
    return pl.BlockSpec(shape, lambda i, j: zeros)


def kernel(x_prompt, x_sample, cache_k, cache_v, state_gla, norm_pre_w, w_in, attn_sinks, w_gk_up, b_gk,
           gla_norm_w, w_out, norm_post_w):
    bsz, t_p, _ = x_prompt.shape
    dec_b, t_s, _ = x_sample.shape
    assert w_in.shape[0] == 1, "single layer"
    assert t_p % PROMPT_TILE == 0 and PROMPT_TILE % CHUNK == 0 and PROMPT_TILE >= WINDOW
    assert dec_b % SAMPLE_SEQS == 0 and t_s & (t_s - 1) == 0

    win = _bf(jnp.pad(w_in[0], ((0, 0), (0, D_IN_PAD - D_IN))))
    wup = _bf(jnp.pad(w_gk_up[0], ((0, LANES - GLA_RANK), (0, 0))))
    wout = _bf(w_out[0])
    wpre = norm_pre_w[0][None, :]
    wpost = norm_post_w[0][None, :]
    bgk = b_gk[0][None, :]
    gnw = gla_norm_w[0][None, :]
    sinks = attn_sinks[0][None, :]
    smem = pl.BlockSpec(memory_space=pltpu.SMEM)

    weights_specs = lambda n: [
        _const_spec((1, D_MODEL), n), _const_spec((D_MODEL, D_IN_PAD), n), _const_spec((LANES, GQK_W), n),
        _const_spec((1, GQK_W), n), _const_spec((1, GLA_DV), n), _const_spec((D_MODEL, D_MODEL), n),
        _const_spec((1, D_MODEL), n)]

    tt = PROMPT_TILE
    cos_p, sdn_p, sup_p = _rotary_tables(jnp.arange(t_p))
    tab_spec = pl.BlockSpec((tt, LANES), lambda b, t: (t, 0))
    y_p, k_p, v_p, s_p = pl.pallas_call(
        _prompt_kernel,
        grid=(bsz, t_p // tt),
        in_specs=[smem, pl.BlockSpec((1, tt, D_MODEL), lambda b, t: (b, t, 0)), tab_spec, tab_spec, tab_spec]
        + weights_specs(2),
        out_specs=[pl.BlockSpec((1, tt, D_MODEL), lambda b, t: (b, t, 0)),
                   pl.BlockSpec((1, WINDOW, KV_W), lambda b, t: (b, 0, 0)),
                   pl.BlockSpec((1, WINDOW, KV_W), lambda b, t: (b, 0, 0)),
                   pl.BlockSpec((1, GLA_HEADS * GLA_DK, GLA_DV), lambda b, t: (b, 0, 0))],
        out_shape=[jax.ShapeDtypeStruct((bsz, t_p, D_MODEL), jnp.float32),
                   jax.ShapeDtypeStruct((bsz, WINDOW, KV_W), jnp.float32),
                   jax.ShapeDtypeStruct((bsz, WINDOW, KV_W), jnp.float32),
                   jax.ShapeDtypeStruct((bsz, GLA_HEADS * GLA_DK, GLA_DV), jnp.float32)],
        scratch_shapes=[pltpu.VMEM((WINDOW + tt, LANES), jnp.bfloat16) for _ in range(6)]
        + [pltpu.VMEM((GLA_HEADS * GLA_DK, GLA_DV), jnp.float32), pltpu.VMEM((tt, D_MODEL), jnp.bfloat16)],
        compiler_params=pltpu.CompilerParams(dimension_semantics=("arbitrary", "arbitrary"),
                                             vmem_limit_bytes=VMEM_LIMIT),
        name="prompt_layer",
    )(sinks, x_prompt, cos_p, sdn_p, sup_p, wpre, win, wup, bgk, gnw, wout, wpost)

    nb = SAMPLE_SEQS
    rows = nb * t_s
    tabs = [jnp.tile(a, (nb, 1)) for a in _rotary_tables(PAST_LEN + jnp.arange(t_s))]
    tab_spec = pl.BlockSpec((rows, LANES), lambda i: (0, 0))
    seq3 = lambda d1, d2: pl.BlockSpec((nb, d1, d2), lambda i: (i, 0, 0))
    tok = lambda d: pl.BlockSpec((rows, d), lambda i: (i, 0))
    sdim = GLA_HEADS * GLA_DK
    y_s, k_s, v_s, s_s = pl.pallas_call(
        _sample_kernel,
        grid=(dec_b // nb,),
        in_specs=[smem, tok(D_MODEL), tab_spec, tab_spec, tab_spec, seq3(WINDOW, KV_W), seq3(WINDOW, KV_W),
                  seq3(sdim, GLA_DV)] + weights_specs(1),
        out_specs=[tok(D_MODEL), tok(KV_W), tok(KV_W), seq3(sdim, GLA_DV)],
        out_shape=[jax.ShapeDtypeStruct((dec_b * t_s, D_MODEL), jnp.float32),
                   jax.ShapeDtypeStruct((dec_b * t_s, KV_W), jnp.float32),
                   jax.ShapeDtypeStruct((dec_b * t_s, KV_W), jnp.float32),
                   jax.ShapeDtypeStruct((dec_b, sdim, GLA_DV), jnp.float32)],
        scratch_shapes=[pltpu.VMEM((rows, D_MODEL), jnp.bfloat16)],
        compiler_params=pltpu.CompilerParams(dimension_semantics=("arbitrary",), vmem_limit_bytes=VMEM_LIMIT),
        name="sample_layer",
    )(sinks, x_sample.reshape(dec_b * t_s, D_MODEL), *tabs,
      cache_k[0].reshape(dec_b, WINDOW, KV_W), cache_v[0].reshape(dec_b, WINDOW, KV_W),
      state_gla[0].reshape(dec_b, sdim, GLA_DV), wpre, win, wup, bgk, gnw, wout, wpost)

    kv5 = lambda a, b, t: a.reshape(1, b, t, N_KV_HEADS, HEAD_DIM)
    st5 = lambda a, b: a.reshape(1, b, GLA_HEADS, GLA_DK, GLA_DV)
    return (y_p, y_s.reshape(dec_b, t_s, D_MODEL),
            kv5(k_p, bsz, WINDOW), kv5(v_p, bsz, WINDOW), st5(s_p, bsz),
            kv5(k_s, dec_b, t_s), kv5(v_s, dec_b, t_s), st5(s_s, dec_b))
```

```python
import collections
import functools

import jax
import jax.numpy as jnp
from jax import lax
from jax.experimental import pallas as pl
from jax.experimental.pallas import tpu as pltpu

D_MODEL = 1024
CHUNK = 64
WINDOW = 128
HEAD_DIM = 64
N_HEADS = 8
N_KV_HEADS = 2
ROT_DIM = 16
ROPE_THETA = 500000.0
GLA_HEADS = 4
GLA_DV = 128
GLA_DK = 64
GLA_RANK = 16
GATE_TAU = 16.0
NORM_EPS = 1e-6
PAST_LEN = 4096

LANES = 128
SUBLANES = 8
ATTN_W = N_HEADS * HEAD_DIM
KV_W = N_KV_HEADS * HEAD_DIM
GQK_W = GLA_HEADS * GLA_DK
GV_W = GLA_HEADS * GLA_DV
S_ROWS = GLA_HEADS * GLA_DK
OFF_AQ = 0
OFF_AK = OFF_AQ + ATTN_W
OFF_AV = OFF_AK + KV_W
OFF_AG = OFF_AV + KV_W
OFF_GQ = OFF_AG + ATTN_W
OFF_GK = OFF_GQ + GQK_W
OFF_GV = OFF_GK + GQK_W
OFF_GG = OFF_GV + GV_W
OFF_LR = OFF_GG + GV_W
D_IN = OFF_LR + GLA_RANK
D_IN_PAD = OFF_LR + LANES

PROMPT_TILE = 256
SAMPLE_SEQS = 8
VMEM_LIMIT = 48 * 1024 * 1024
PROMPT_ORDER = "CPCPCPCPCPPPP"

NEG = -1e30

Slot = collections.namedtuple("Slot", "qs kv ag qin kog kdec dec gv gg")


def _slot_shapes(rows, kv_rows, n_chunks):
    bf, f32 = jnp.bfloat16, jnp.float32
    return ([pltpu.VMEM((rows, 2 * ATTN_W), bf)] + [pltpu.VMEM((kv_rows, LANES), bf) for _ in range(6)]
            + [pltpu.VMEM((rows, ATTN_W), f32), pltpu.VMEM((rows, 2 * GQK_W), bf), pltpu.VMEM((rows, GQK_W), bf),
               pltpu.VMEM((rows, GQK_W), f32), pltpu.VMEM((n_chunks * SUBLANES, GQK_W), f32),
               pltpu.VMEM((rows, GV_W), bf), pltpu.VMEM((rows, GV_W), f32)])


N_SLOT_REFS = 14


def _make_slot(refs):
    return Slot(refs[0], tuple(refs[1:7]), *refs[7:14])


def _dot(a, b):
    return jnp.dot(a, b, preferred_element_type=jnp.float32)


def _dot_nt(a, b):
    return lax.dot_general(a, b, (((1,), (1,)), ((), ())), preferred_element_type=jnp.float32)


def _bf(x):
    return x.astype(jnp.bfloat16)


def _lane_masks():
    lane = lax.broadcasted_iota(jnp.int32, (1, LANES), 1)
    return lane < HEAD_DIM, lane >= HEAD_DIM


def _silu(x):
    return x * (1.0 / (1.0 + jnp.exp(-x)))


def _log_sigmoid(x):
    return jnp.minimum(x, 0.0) - jnp.log1p(jnp.exp(-jnp.abs(x)))


def _rms_scale(x):
    return lax.rsqrt(jnp.mean(x * x, axis=-1, keepdims=True) + NORM_EPS)


def _chunk_cumsum(g, chunk):
    row = lax.broadcasted_iota(jnp.int32, g.shape, 0) & (chunk - 1)
    s = 1
    while s < chunk:
        g = g + jnp.where(row >= s, pltpu.roll(g, s, 0), 0.0)
        s *= 2
    return g


def _rotary(xcol, cos_t, sin_dn, sin_up):
    return (xcol * cos_t + pltpu.roll(xcol, ROT_DIM // 2, 1) * sin_dn
            + pltpu.roll(xcol, LANES - ROT_DIM // 2, 1) * sin_up)


def _kv_variants(k, v):
    m_lo, m_hi = _lane_masks()
    k_sw = pltpu.roll(k, HEAD_DIM, 1)
    v_sw = pltpu.roll(v, HEAD_DIM, 1)
    return (_bf(k), _bf(k_sw),
            _bf(jnp.where(m_lo, v, 0.0)), _bf(jnp.where(m_hi, v_sw, 0.0)),
            _bf(jnp.where(m_lo, v_sw, 0.0)), _bf(jnp.where(m_hi, v, 0.0)))


def _store_halves(ref, j, width, col):
    m_lo, m_hi = _lane_masks()
    ref[:, j * LANES:(j + 1) * LANES] = _bf(jnp.where(m_lo, col, 0.0))
    ref[:, width + j * LANES:width + (j + 1) * LANES] = _bf(jnp.where(m_hi, col, 0.0))


def _project_stages(x, tabs, wpre_ref, win_ref, wup_ref, bgk_ref, chunk, slot, store_kv):
    rows = x.shape[0]
    h = _bf(x * _rms_scale(x) * wpre_ref[...])
    cos_t, sin_dn, sin_up = (t[...] for t in tabs)

    def seg(off, width):
        return _dot(h, win_ref[:, off:off + width])

    yield
    q = seg(OFF_AQ, ATTN_W)
    for j in range(ATTN_W // LANES):
        ln = slice(j * LANES, (j + 1) * LANES)
        _store_halves(slot.qs, j, ATTN_W, _rotary(q[:, ln], cos_t, sin_dn, sin_up) * (HEAD_DIM ** -0.5))
    yield
    kv = seg(OFF_AK, 2 * KV_W)
    store_kv(_rotary(kv[:, :KV_W], cos_t, sin_dn, sin_up), kv[:, KV_W:])
    yield
    slot.ag[...] = _silu(seg(OFF_AG, ATTN_W))
    yield
    lr = _bf(seg(OFF_LR, LANES))
    log_a = _log_sigmoid(_dot(lr, wup_ref[...]) + bgk_ref[...]) * (1.0 / GATE_TAU)
    b = _chunk_cumsum(log_a, chunk)
    lasts = [b[r + chunk - 1:r + chunk] for r in range(0, rows, chunk)]
    for i, bl in enumerate(lasts):
        slot.dec[i * SUBLANES:(i + 1) * SUBLANES, :] = jnp.broadcast_to(jnp.exp(bl), (SUBLANES, GQK_W))
    b_last = jnp.concatenate([jnp.broadcast_to(bl, (chunk, GQK_W)) for bl in lasts], axis=0)
    q_in = seg(OFF_GQ, GQK_W) * (GLA_DK ** -0.5) * jnp.exp(b)
    for j in range(GQK_W // LANES):
        _store_halves(slot.qin, j, GQK_W, q_in[:, j * LANES:(j + 1) * LANES])
    yield
    gk = seg(OFF_GK, GQK_W)
    slot.kog[...] = _bf(gk * jnp.exp(-b))
    slot.kdec[...] = gk * jnp.exp(b_last - b)
    yield
    slot.gv[...] = _bf(seg(OFF_GV, GV_W))
    yield
    slot.gg[...] = _silu(seg(OFF_GG, GV_W))


def _attend_stages(lo, hi, k_a, k_b, vs, sinks_ref, n_invalid, emit):
    c = lo[0].shape[0]
    s_a = _dot_nt(jnp.concatenate([lo[0], lo[1], hi[2], hi[3]], axis=0), k_a)
    s_b = _dot_nt(jnp.concatenate([hi[0], hi[1], lo[2], lo[3]], axis=0), k_b)
    yield
    if n_invalid is not None:
        key = lax.broadcasted_iota(jnp.int32, (1, s_a.shape[1]), 1)
        dead = key < n_invalid
        s_a = jnp.where(dead, NEG, s_a)
        s_b = jnp.where(dead, NEG, s_b)
    where = {0: (s_a, 0), 2: (s_a, 1), 5: (s_a, 2), 7: (s_a, 3),
             1: (s_b, 0), 3: (s_b, 1), 4: (s_b, 2), 6: (s_b, 3)}
    scores, maxes = [], []
    for h in range(N_HEADS):
        src, blk = where[h]
        s = src[blk * c:(blk + 1) * c]
        scores.append(s)
        maxes.append(jnp.maximum(jnp.max(s, axis=-1, keepdims=True), sinks_ref[0, h]))
    yield
    v0l, v0r, v1l, v1r = vs
    outs = []
    for h in range(N_HEADS):
        p = jnp.exp(scores[h] - maxes[h])
        denom = jnp.sum(p, axis=-1, keepdims=True) + jnp.exp(sinks_ref[0, h] - maxes[h])
        probs = _bf(p * (1.0 / denom))
        if h < 4:
            vv = v0l if h % 2 == 0 else v0r
        else:
            vv = v1l if h % 2 == 0 else v1r
        outs.append(_dot(probs, vv))
    yield
    emit(jnp.concatenate([outs[2 * j] + outs[2 * j + 1] for j in range(4)], axis=1))


def _gla_stages(slot, r0, c, state, gnw_ref, mix_ref):
    rows = slice(r0, r0 + c)
    ci0 = (r0 // c) * SUBLANES
    pad = LANES - c
    lhs, a2, uw, dcol, vw = [], [], [], [], []
    for p in range(GLA_HEADS // 2):
        ln = slice(p * LANES, (p + 1) * LANES)
        vwide = slot.gv[rows, p * 2 * GLA_DV:(p + 1) * 2 * GLA_DV]
        lhs.append(jnp.concatenate([slot.qin[rows, ln], slot.qin[rows, GQK_W + p * LANES:GQK_W + (p + 1) * LANES]],
                                   axis=0))
        a2.append(_dot_nt(lhs[p], slot.kog[rows, ln]))
        kd_t = jnp.concatenate([slot.kdec[rows, ln], jnp.zeros((pad, LANES), jnp.float32)], axis=0).T
        v_pad = jnp.concatenate([vwide, jnp.zeros((pad, 2 * GLA_DV), jnp.bfloat16)], axis=0)
        uw.append(_dot(_bf(kd_t), v_pad))
        dcol.append(jnp.broadcast_to(slot.dec[ci0:ci0 + 1, ln], (LANES, LANES)).T)
        vw.append(vwide)
    yield
    outs = []
    ri = lax.broadcasted_iota(jnp.int32, a2[0].shape, 0) & (c - 1)
    ci = lax.broadcasted_iota(jnp.int32, a2[0].shape, 1)
    for p in range(GLA_HEADS // 2):
        am = _bf(jnp.where(ri >= ci, a2[p], 0.0))
        inter = _dot(lhs[p], _bf(state[p]))
        outs.append(_dot(am[:c], vw[p][:, :GLA_DV]) + inter[:c])
        outs.append(_dot(am[c:], vw[p][:, GLA_DV:]) + inter[c:])
        u_pair = jnp.concatenate([uw[p][:GLA_DK, :GLA_DV], uw[p][GLA_DK:, GLA_DV:]], axis=0)
        state[p] = dcol[p] * state[p] + u_pair
    yield
    gnw = gnw_ref[...]
    for hh, o in enumerate(outs):
        y = o * _rms_scale(o) * gnw * slot.gg[rows, hh * GLA_DV:(hh + 1) * GLA_DV]
        mix_ref[rows, ATTN_W + hh * GLA_DV:ATTN_W + (hh + 1) * GLA_DV] = _bf(y)


def _lockstep(gens):
    live = list(gens)
    while live:
        for g in list(live):
            try:
                next(g)
            except StopIteration:
                live.remove(g)
        yield


def _chunk_stages(slot, chunk_args, c, state_of, sinks_ref, gnw_ref, mix_ref):
    gens = []
    for i, (r0, k_a, k_b, vs, n_invalid) in enumerate(chunk_args):
        rows = slice(r0, r0 + c)

        def emit(attn, rows=rows):
            mix_ref[rows, 0:ATTN_W] = _bf(attn * slot.ag[rows, :])

        lo = [slot.qs[rows, j * LANES:(j + 1) * LANES] for j in range(ATTN_W // LANES)]
        hi = [slot.qs[rows, ATTN_W + j * LANES:ATTN_W + (j + 1) * LANES] for j in range(ATTN_W // LANES)]
        gens.append(_attend_stages(lo, hi, k_a, k_b, vs, sinks_ref, n_invalid, emit))
        gens.append(_gla_stages(slot, r0, c, state_of(i), gnw_ref, mix_ref))
    return _lockstep(gens)


def _finish(x, mix_ref, wout_ref, wpost_ref):
    mix = _dot(mix_ref[...], wout_ref[...])
    return x + mix * _rms_scale(mix) * wpost_ref[...]


def _run_order(order, p_gen, c_gen):
    for ch in order:
        next(p_gen if ch == "P" else c_gen, None)
    for g in (p_gen, c_gen):
        for _ in g:
            pass


def _prompt_kernel(nt, sinks_ref, xp_ref, xc_ref, cos_ref, sdn_ref, sup_ref, wpre_ref, win_ref, wup_ref, bgk_ref,
                   gnw_ref, wout_ref, wpost_ref,
                   y_ref, kout_ref, vout_ref, sout_ref, *scratch):
    slots = (_make_slot(scratch[:N_SLOT_REFS]), _make_slot(scratch[N_SLOT_REFS:2 * N_SLOT_REFS]))
    s_ref, mix_ref = scratch[2 * N_SLOT_REFS:]
    s = pl.program_id(0)
    tt = xp_ref.shape[1]
    t_c = jnp.maximum(s - 1, 0) % nt

    @pl.when(s == 0)
    def _():
        for r in scratch[N_SLOT_REFS:]:
            r[...] = jnp.zeros_like(r)

    def body(slot_p, slot_c):
        def store_kv(k, v):
            kout_ref[0] = k[tt - WINDOW:]
            vout_ref[0] = v[tt - WINDOW:]
            for r, rc, val in zip(slot_p.kv, slot_c.kv, _kv_variants(k, v)):
                r[WINDOW:WINDOW + tt, :] = val
                r[0:WINDOW, :] = rc[tt:tt + WINDOW, :]

        p_gen = _project_stages(xp_ref[0], (cos_ref, sdn_ref, sup_ref), wpre_ref, win_ref, wup_ref, bgk_ref,
                                CHUNK, slot_p, store_kv)

        def c_stages():
            keep = t_c != 0
            state = [jnp.where(keep, s_ref[p * LANES:(p + 1) * LANES, :], 0.0) for p in range(GLA_HEADS // 2)]
            win = WINDOW + CHUNK
            chunk_args = []
            for r0 in range(0, tt, CHUNK):
                n_invalid = WINDOW - (t_c * tt + r0) if r0 < WINDOW else None
                chunk_args.append((r0, slot_c.kv[0][r0:r0 + win, :], slot_c.kv[1][r0:r0 + win, :],
                                   [r[r0:r0 + win, :] for r in slot_c.kv[2:]], n_invalid))
            yield from _chunk_stages(slot_c, chunk_args, CHUNK, lambda i: state, sinks_ref, gnw_ref, mix_ref)
            y_ref[0] = _finish(xc_ref[0], mix_ref, wout_ref, wpost_ref)
            for p in range(GLA_HEADS // 2):
                s_ref[p * LANES:(p + 1) * LANES, :] = state[p]
                sout_ref[0, p * LANES:(p + 1) * LANES, :] = state[p]

        _run_order(PROMPT_ORDER, p_gen, c_stages())

    @pl.when(s % 2 == 0)
    def _():
        body(slots[0], slots[1])

    @pl.when(s % 2 == 1)
    def _():
        body(slots[1], slots[0])


def _sample_kernel(sinks_ref, x_ref, cos_ref, sdn_ref, sup_ref, ck_ref, cv_ref, s0_ref, wpre_ref, win_ref,
                   wup_ref, bgk_ref, gnw_ref, wout_ref, wpost_ref,
                   y_ref, kout_ref, vout_ref, sout_ref, *scratch):
    slot = _make_slot(scratch[:N_SLOT_REFS])
    mix_ref = scratch[N_SLOT_REFS]
    nb = ck_ref.shape[0]
    c = x_ref.shape[0] // nb
    win = WINDOW + c

    def store_kv(k, v):
        kout_ref[...] = k
        vout_ref[...] = v
        for i in range(nb):
            rows = slice(i * c, (i + 1) * c)
            kv = _kv_variants(jnp.concatenate([ck_ref[i], k[rows]], axis=0),
                              jnp.concatenate([cv_ref[i], v[rows]], axis=0))
            for r, val in zip(slot.kv, kv):
                r[i * win:(i + 1) * win, :] = val

    x = x_ref[...]
    for _ in _project_stages(x, (cos_ref, sdn_ref, sup_ref), wpre_ref, win_ref, wup_ref, bgk_ref, c, slot, store_kv):
        pass
    states = [[s0_ref[i, p * LANES:(p + 1) * LANES, :] for p in range(GLA_HEADS // 2)] for i in range(nb)]
    chunk_args = [(i * c, slot.kv[0][i * win:(i + 1) * win, :], slot.kv[1][i * win:(i + 1) * win, :],
                   [r[i * win:(i + 1) * win, :] for r in slot.kv[2:]], None) for i in range(nb)]
    for _ in _chunk_stages(slot, chunk_args, c, lambda i: states[i], sinks_ref, gnw_ref, mix_ref):
        pass
    for i in range(nb):
        for p in range(GLA_HEADS // 2):
            sout_ref[i, p * LANES:(p + 1) * LANES, :] = states[i][p]
    y_ref[...] = _finish(x, mix_ref, wout_ref, wpost_ref)


def _rotary_tables(pos):
    half = ROT_DIM // 2
    inv = ROPE_THETA ** (-jnp.arange(half, dtype=jnp.float32) * (2.0 / ROT_DIM))
    ang = pos.astype(jnp.float32)[:, None] * inv[None, :]
    cos, sin = jnp.cos(ang), jnp.sin(ang)
    t = pos.shape[0]
    one = jnp.ones((t, HEAD_DIM - ROT_DIM), jnp.float32)
    zero = jnp.zeros((t, HEAD_DIM - ROT_DIM), jnp.float32)
    z8 = jnp.zeros((t, half), jnp.float32)
    cos_t = jnp.concatenate([cos, cos, one], axis=1)
    sin_dn = jnp.concatenate([z8, sin, zero], axis=1)
    sin_up = jnp.concatenate([-sin, z8, zero], axis=1)
    rep = LANES // HEAD_DIM
    return tuple(jnp.tile(a, (1, rep)) for a in (cos_t, sin_dn, sin_up))


def _const_spec(shape):
    zeros = (0,) * len(shape)
    return pl.BlockSpec(shape, lambda i: zeros)


def kernel(x_prompt, x_sample, cache_k, cache_v, state_gla, norm_pre_w, w_in, attn_sinks, w_gk_up, b_gk,
           gla_norm_w, w_out, norm_post_w):
    bsz, t_p, _ = x_prompt.shape
    dec_b, t_s, _ = x_sample.shape
    assert w_in.shape[0] == 1, "single layer"
    assert t_p % PROMPT_TILE == 0 and PROMPT_TILE % CHUNK == 0 and PROMPT_TILE >= WINDOW
    assert dec_b % SAMPLE_SEQS == 0 and t_s & (t_s - 1) == 0

    win = _bf(jnp.pad(w_in[0], ((0, 0), (0, D_IN_PAD - D_IN))))
    wup = _bf(jnp.pad(w_gk_up[0], ((0, LANES - GLA_RANK), (0, 0))))
    wout = _bf(w_out[0])
    wpre = norm_pre_w[0][None, :]
    wpost = norm_post_w[0][None, :]
    bgk = b_gk[0][None, :]
    gnw = gla_norm_w[0][None, :]
    sinks = attn_sinks[0][None, :]
    smem = pl.BlockSpec(memory_space=pltpu.SMEM)

    weights_specs = [
        _const_spec((1, D_MODEL)), _const_spec((D_MODEL, D_IN_PAD)), _const_spec((LANES, GQK_W)),
        _const_spec((1, GQK_W)), _const_spec((1, GLA_DV)), _const_spec((D_MODEL, D_MODEL)),
        _const_spec((1, D_MODEL))]

    tt = PROMPT_TILE
    nt = t_p // tt
    n_tiles = bsz * nt
    cos_p, sdn_p, sup_p = _rotary_tables(jnp.arange(t_p))

    def tile_p(s):
        return jnp.minimum(s, n_tiles - 1)

    def tile_c(s):
        return jnp.maximum(s - 1, 0)

    tab_spec = pl.BlockSpec((tt, LANES), lambda s: (tile_p(s) % nt, 0))
    x_spec = lambda tile: pl.BlockSpec((1, tt, D_MODEL), lambda s: (tile(s) // nt, tile(s) % nt, 0))
    seq_spec = lambda tile, d1, d2: pl.BlockSpec((1, d1, d2), lambda s: (tile(s) // nt, 0, 0))
    y_p, k_p, v_p, s_p = pl.pallas_call(
        functools.partial(_prompt_kernel, nt),
        grid=(n_tiles + 1,),
        in_specs=[smem, x_spec(tile_p), x_spec(tile_c), tab_spec, tab_spec, tab_spec] + weights_specs,
        out_specs=[x_spec(tile_c), seq_spec(tile_p, WINDOW, KV_W), seq_spec(tile_p, WINDOW, KV_W),
                   seq_spec(tile_c, S_ROWS, GLA_DV)],
        out_shape=[jax.ShapeDtypeStruct((bsz, t_p, D_MODEL), jnp.float32),
                   jax.ShapeDtypeStruct((bsz, WINDOW, KV_W), jnp.float32),
                   jax.ShapeDtypeStruct((bsz, WINDOW, KV_W), jnp.float32),
                   jax.ShapeDtypeStruct((bsz, S_ROWS, GLA_DV), jnp.float32)],
        scratch_shapes=_slot_shapes(tt, WINDOW + tt, tt // CHUNK) * 2
        + [pltpu.VMEM((S_ROWS, GLA_DV), jnp.float32), pltpu.VMEM((tt, D_MODEL), jnp.bfloat16)],
        compiler_params=pltpu.CompilerParams(dimension_semantics=("arbitrary",), vmem_limit_bytes=VMEM_LIMIT),
        name="prompt_layer",
    )(sinks, x_prompt, x_prompt, cos_p, sdn_p, sup_p, wpre, win, wup, bgk, gnw, wout, wpost)

    nb = SAMPLE_SEQS
    rows = nb * t_s
    tabs = [jnp.tile(a, (nb, 1)) for a in _rotary_tables(PAST_LEN + jnp.arange(t_s))]
    tab_spec = pl.BlockSpec((rows, LANES), lambda i: (0, 0))
    seq3 = lambda d1, d2: pl.BlockSpec((nb, d1, d2), lambda i: (i, 0, 0))
    tok = lambda d: pl.BlockSpec((rows, d), lambda i: (i, 0))
    y_s, k_s, v_s, s_s = pl.pallas_call(
        _sample_kernel,
        grid=(dec_b // nb,),
        in_specs=[smem, tok(D_MODEL), tab_spec, tab_spec, tab_spec, seq3(WINDOW, KV_W), seq3(WINDOW, KV_W),
                  seq3(S_ROWS, GLA_DV)] + weights_specs,
        out_specs=[tok(D_MODEL), tok(KV_W), tok(KV_W), seq3(S_ROWS, GLA_DV)],
        out_shape=[jax.ShapeDtypeStruct((dec_b * t_s, D_MODEL), jnp.float32),
                   jax.ShapeDtypeStruct((dec_b * t_s, KV_W), jnp.float32),
                   jax.ShapeDtypeStruct((dec_b * t_s, KV_W), jnp.float32),
                   jax.ShapeDtypeStruct((dec_b, S_ROWS, GLA_DV), jnp.float32)],
        scratch_shapes=_slot_shapes(rows, nb * (WINDOW + t_s), nb) + [pltpu.VMEM((rows, D_MODEL), jnp.bfloat16)],
        compiler_params=pltpu.CompilerParams(dimension_semantics=("arbitrary",), vmem_limit_bytes=VMEM_LIMIT),
        name="sample_layer",
    )(sinks, x_sample.reshape(dec_b * t_s, D_MODEL), *tabs,
      cache_k[0].reshape(dec_b, WINDOW, KV_W), cache_v[0].reshape(dec_b, WINDOW, KV_W),
      state_gla[0].reshape(dec_b, S_ROWS, GLA_DV), wpre, win, wup, bgk, gnw, wout, wpost)

    kv5 = lambda a, b, t: a.reshape(1, b, t, N_KV_HEADS, HEAD_DIM)
    st5 = lambda a, b: a.reshape(1, b, GLA_HEADS, GLA_DK, GLA_DV)
    return (y_p, y_s.reshape(dec_b, t_s, D_MODEL),
            kv5(k_p, bsz, WINDOW), kv5(v_p, bsz, WINDOW), st5(s_p, bsz),
            kv5(k_s, dec_b, t_s), kv5(v_s, dec_b, t_s), st5(s_s, dec_b))
```

```python
import collections
import functools

import jax
import jax.numpy as jnp
from jax import lax
from jax.experimental import pallas as pl
from jax.experimental.pallas import tpu as pltpu

D_MODEL = 1024
CHUNK = 64
WINDOW = 128
HEAD_DIM = 64
N_HEADS = 8
N_KV_HEADS = 2
ROT_DIM = 16
ROPE_THETA = 500000.0
GLA_HEADS = 4
GLA_DV = 128
GLA_DK = 64
GLA_RANK = 16
GATE_TAU = 16.0
NORM_EPS = 1e-6
PAST_LEN = 4096

LANES = 128
SUBLANES = 8
ATTN_W = N_HEADS * HEAD_DIM
KV_W = N_KV_HEADS * HEAD_DIM
GQK_W = GLA_HEADS * GLA_DK
GV_W = GLA_HEADS * GLA_DV
S_ROWS = GLA_HEADS * GLA_DK
OFF_AQ = 0
OFF_AK = OFF_AQ + ATTN_W
OFF_AV = OFF_AK + KV_W
OFF_AG = OFF_AV + KV_W
OFF_GQ = OFF_AG + ATTN_W
OFF_GK = OFF_GQ + GQK_W
OFF_GV = OFF_GK + GQK_W
OFF_GG = OFF_GV + GV_W
OFF_LR = OFF_GG + GV_W
D_IN = OFF_LR + GLA_RANK
D_IN_PAD = OFF_LR + LANES

PROMPT_TILE = 256
SAMPLE_SEQS = 8
VMEM_LIMIT = 48 * 1024 * 1024
PROMPT_ORDER = "CPCPCPCPCPPPP"

NEG = -1e30
LOG2E = 1.4426950408889634

Slot = collections.namedtuple("Slot", "qs kv ag qin kog kdec dec gv gg")


def _slot_shapes(rows, kv_rows, n_chunks):
    bf, f32 = jnp.bfloat16, jnp.float32
    return ([pltpu.VMEM((rows, 2 * ATTN_W), bf)] + [pltpu.VMEM((kv_rows, LANES), bf) for _ in range(6)]
            + [pltpu.VMEM((rows, ATTN_W), f32), pltpu.VMEM((rows, 2 * GQK_W), bf), pltpu.VMEM((rows, GQK_W), bf),
               pltpu.VMEM((rows, GQK_W), f32), pltpu.VMEM((n_chunks * SUBLANES, GQK_W), f32),
               pltpu.VMEM((rows, GV_W), bf), pltpu.VMEM((rows, GV_W), f32)])


N_SLOT_REFS = 14


def _make_slot(refs):
    return Slot(refs[0], tuple(refs[1:7]), *refs[7:14])


def _dot(a, b):
    return jnp.dot(a, b, preferred_element_type=jnp.float32)


def _dot_nt(a, b):
    return lax.dot_general(a, b, (((1,), (1,)), ((), ())), preferred_element_type=jnp.float32)


def _bf(x):
    return x.astype(jnp.bfloat16)


def _lane_masks():
    lane = lax.broadcasted_iota(jnp.int32, (1, LANES), 1)
    return lane < HEAD_DIM, lane >= HEAD_DIM


def _silu(x):
    return x * (1.0 / (1.0 + jnp.exp(-x)))


def _log_sigmoid(x):
    return jnp.minimum(x, 0.0) - jnp.log(1.0 + jnp.exp(-jnp.abs(x)))


def _rms_scale(x):
    return lax.rsqrt(jnp.mean(x * x, axis=-1, keepdims=True) + NORM_EPS)


def _chunk_cumsum(g, chunk):
    row = lax.broadcasted_iota(jnp.int32, g.shape, 0) & (chunk - 1)
    s = 1
    while s < chunk:
        g = g + jnp.where(row >= s, pltpu.roll(g, s, 0), 0.0)
        s *= 2
    return g


def _rotary(xcol, cos_t, sin_dn, sin_up):
    return (xcol * cos_t + pltpu.roll(xcol, ROT_DIM // 2, 1) * sin_dn
            + pltpu.roll(xcol, LANES - ROT_DIM // 2, 1) * sin_up)


def _kv_variants(k, v):
    m_lo, m_hi = _lane_masks()
    k_sw = pltpu.roll(k, HEAD_DIM, 1)
    v_sw = pltpu.roll(v, HEAD_DIM, 1)
    return (_bf(k), _bf(k_sw),
            _bf(jnp.where(m_lo, v, 0.0)), _bf(jnp.where(m_hi, v_sw, 0.0)),
            _bf(jnp.where(m_lo, v_sw, 0.0)), _bf(jnp.where(m_hi, v, 0.0)))


def _store_halves(ref, j, width, col):
    m_lo, m_hi = _lane_masks()
    ref[:, j * LANES:(j + 1) * LANES] = _bf(jnp.where(m_lo, col, 0.0))
    ref[:, width + j * LANES:width + (j + 1) * LANES] = _bf(jnp.where(m_hi, col, 0.0))


def _project_stages(x, tabs, wpre_ref, win_ref, wup_ref, bgk_ref, chunk, slot, store_kv):
    rows = x.shape[0]
    h = _bf(x * _rms_scale(x) * wpre_ref[...])
    cos_t, sin_dn, sin_up = (t[...] for t in tabs)

    def seg(off, width):
        return _dot(h, win_ref[:, off:off + width])

    yield
    q = seg(OFF_AQ, ATTN_W)
    for j in range(ATTN_W // LANES):
        ln = slice(j * LANES, (j + 1) * LANES)
        _store_halves(slot.qs, j, ATTN_W, _rotary(q[:, ln], cos_t, sin_dn, sin_up) * (LOG2E * HEAD_DIM ** -0.5))
    yield
    kv = seg(OFF_AK, 2 * KV_W)
    store_kv(_rotary(kv[:, :KV_W], cos_t, sin_dn, sin_up), kv[:, KV_W:])
    yield
    slot.ag[...] = _silu(seg(OFF_AG, ATTN_W))
    yield
    lr = _bf(seg(OFF_LR, LANES))
    log_a = _log_sigmoid(_dot(lr, wup_ref[...]) + bgk_ref[...]) * (1.0 / GATE_TAU)
    b = _chunk_cumsum(log_a, chunk)
    lasts = [b[r + chunk - 1:r + chunk] for r in range(0, rows, chunk)]
    for i, bl in enumerate(lasts):
        slot.dec[i * SUBLANES:(i + 1) * SUBLANES, :] = jnp.broadcast_to(jnp.exp(bl), (SUBLANES, GQK_W))
    b_last = jnp.concatenate([jnp.broadcast_to(bl, (chunk, GQK_W)) for bl in lasts], axis=0)
    q_in = seg(OFF_GQ, GQK_W) * (GLA_DK ** -0.5) * jnp.exp(b)
    for j in range(GQK_W // LANES):
        _store_halves(slot.qin, j, GQK_W, q_in[:, j * LANES:(j + 1) * LANES])
    yield
    gk = seg(OFF_GK, GQK_W)
    slot.kog[...] = _bf(gk * jnp.exp(-b))
    slot.kdec[...] = gk * jnp.exp(b_last - b)
    yield
    slot.gv[...] = _bf(seg(OFF_GV, GV_W))
    yield
    slot.gg[...] = _silu(seg(OFF_GG, GV_W))


def _attend_stages(lo, hi, k_a, k_b, vs, sinks_ref, n_invalid, emit):
    c = lo[0].shape[0]
    n_hi = k_a.shape[0] - LANES
    zrows = jnp.zeros((LANES - n_hi, LANES), jnp.bfloat16)
    ext = lambda a: jnp.concatenate([a, zrows], axis=0)
    s_a = _dot_nt(jnp.concatenate([lo[0], lo[1], hi[2], hi[3]], axis=0), ext(k_a))
    s_b = _dot_nt(jnp.concatenate([hi[0], hi[1], lo[2], lo[3]], axis=0), ext(k_b))
    yield
    lane = lax.broadcasted_iota(jnp.int32, (1, LANES), 1)
    where = {0: (s_a, 0), 2: (s_a, 1), 5: (s_a, 2), 7: (s_a, 3),
             1: (s_b, 0), 3: (s_b, 1), 4: (s_b, 2), 6: (s_b, 3)}
    scores, maxes = [], []
    for h in range(N_HEADS):
        src, blk = where[h]
        s = src[blk * c:(blk + 1) * c]
        pad_row = jnp.where(lane == n_hi, sinks_ref[0, h] * LOG2E, NEG)
        s_lo = s[:, :LANES]
        s_hi = jnp.where(lane >= n_hi, pad_row, s[:, LANES:])
        if n_invalid is not None:
            s_lo = jnp.where(lane < n_invalid, NEG, s_lo)
        scores.append((s_lo, s_hi))
        maxes.append(jnp.max(jnp.maximum(s_lo, s_hi), axis=-1, keepdims=True))
    yield
    v0l, v0r, v1l, v1r = vs
    outs = []
    for h in range(N_HEADS):
        p_lo = jnp.exp2(scores[h][0] - maxes[h])
        p_hi = jnp.exp2(scores[h][1] - maxes[h])
        denom = jnp.sum(p_lo + p_hi, axis=-1, keepdims=True)
        if h < 4:
            vv = v0l if h % 2 == 0 else v0r
        else:
            vv = v1l if h % 2 == 0 else v1r
        o = _dot(_bf(jnp.concatenate([p_lo, p_hi], axis=1)), ext(vv))
        outs.append(o * (1.0 / denom))
    yield
    emit(jnp.concatenate([outs[2 * j] + outs[2 * j + 1] for j in range(4)], axis=1))


def _gla_stages(slot, r0, c, state, gnw_ref, mix_ref):
    rows = slice(r0, r0 + c)
    ci0 = (r0 // c) * SUBLANES
    pad = LANES - c
    lhs, a2, uw, dcol, vw = [], [], [], [], []
    for p in range(GLA_HEADS // 2):
        ln = slice(p * LANES, (p + 1) * LANES)
        vwide = slot.gv[rows, p * 2 * GLA_DV:(p + 1) * 2 * GLA_DV]
        lhs.append(jnp.concatenate([slot.qin[rows, ln], slot.qin[rows, GQK_W + p * LANES:GQK_W + (p + 1) * LANES]],
                                   axis=0))
        a2.append(_dot_nt(lhs[p], slot.kog[rows, ln]))
        kd_t = jnp.concatenate([slot.kdec[rows, ln], jnp.zeros((pad, LANES), jnp.float32)], axis=0).T
        v_pad = jnp.concatenate([vwide, jnp.zeros((pad, 2 * GLA_DV), jnp.bfloat16)], axis=0)
        uw.append(_dot(_bf(kd_t), v_pad))
        dcol.append(jnp.broadcast_to(slot.dec[ci0:ci0 + 1, ln], (LANES, LANES)).T)
        vw.append(vwide)
    yield
    outs = []
    ri = lax.broadcasted_iota(jnp.int32, a2[0].shape, 0) & (c - 1)
    ci = lax.broadcasted_iota(jnp.int32, a2[0].shape, 1)
    for p in range(GLA_HEADS // 2):
        am = _bf(jnp.where(ri >= ci, a2[p], 0.0))
        inter = _dot(lhs[p], _bf(state[p]))
        outs.append(_dot(am[:c], vw[p][:, :GLA_DV]) + inter[:c])
        outs.append(_dot(am[c:], vw[p][:, GLA_DV:]) + inter[c:])
        u_pair = jnp.concatenate([uw[p][:GLA_DK, :GLA_DV], uw[p][GLA_DK:, GLA_DV:]], axis=0)
        state[p] = dcol[p] * state[p] + u_pair
    yield
    gnw = gnw_ref[...]
    for hh, o in enumerate(outs):
        y = o * _rms_scale(o) * gnw * slot.gg[rows, hh * GLA_DV:(hh + 1) * GLA_DV]
        mix_ref[rows, ATTN_W + hh * GLA_DV:ATTN_W + (hh + 1) * GLA_DV] = _bf(y)


def _lockstep(gens):
    live = list(gens)
    while live:
        for g in list(live):
            try:
                next(g)
            except StopIteration:
                live.remove(g)
        yield


def _chunk_stages(slot, chunk_args, c, state_of, sinks_ref, gnw_ref, mix_ref):
    gens = []
    for i, (r0, k_a, k_b, vs, n_invalid) in enumerate(chunk_args):
        rows = slice(r0, r0 + c)

        def emit(attn, rows=rows):
            mix_ref[rows, 0:ATTN_W] = _bf(attn * slot.ag[rows, :])

        lo = [slot.qs[rows, j * LANES:(j + 1) * LANES] for j in range(ATTN_W // LANES)]
        hi = [slot.qs[rows, ATTN_W + j * LANES:ATTN_W + (j + 1) * LANES] for j in range(ATTN_W // LANES)]
        gens.append(_attend_stages(lo, hi, k_a, k_b, vs, sinks_ref, n_invalid, emit))
        gens.append(_gla_stages(slot, r0, c, state_of(i), gnw_ref, mix_ref))
    return _lockstep(gens)


def _finish(x, mix_ref, wout_ref, wpost_ref):
    mix = _dot(mix_ref[...], wout_ref[...])
    return x + mix * _rms_scale(mix) * wpost_ref[...]


def _run_order(order, p_gen, c_gen):
    for ch in order:
        next(p_gen if ch == "P" else c_gen, None)
    for g in (p_gen, c_gen):
        for _ in g:
            pass


def _prompt_kernel(nt, sinks_ref, xp_ref, xc_ref, cos_ref, sdn_ref, sup_ref, wpre_ref, win_ref, wup_ref, bgk_ref,
                   gnw_ref, wout_ref, wpost_ref,
                   y_ref, kout_ref, vout_ref, sout_ref, *scratch):
    slots = (_make_slot(scratch[:N_SLOT_REFS]), _make_slot(scratch[N_SLOT_REFS:2 * N_SLOT_REFS]))
    s_ref, mix_ref = scratch[2 * N_SLOT_REFS:]
    s = pl.program_id(0)
    tt = xp_ref.shape[1]
    t_c = jnp.maximum(s - 1, 0) % nt

    @pl.when(s == 0)
    def _():
        for r in scratch[N_SLOT_REFS:]:
            r[...] = jnp.zeros_like(r)

    def body(slot_p, slot_c):
        def store_kv(k, v):
            kout_ref[0] = k[tt - WINDOW:]
            vout_ref[0] = v[tt - WINDOW:]
            for r, rc, val in zip(slot_p.kv, slot_c.kv, _kv_variants(k, v)):
                r[WINDOW:WINDOW + tt, :] = val
                r[0:WINDOW, :] = rc[tt:tt + WINDOW, :]

        p_gen = _project_stages(xp_ref[0], (cos_ref, sdn_ref, sup_ref), wpre_ref, win_ref, wup_ref, bgk_ref,
                                CHUNK, slot_p, store_kv)

        def c_stages():
            keep = t_c != 0
            state = [jnp.where(keep, s_ref[p * LANES:(p + 1) * LANES, :], 0.0) for p in range(GLA_HEADS // 2)]
            win = WINDOW + CHUNK
            chunk_args = []
            for r0 in range(0, tt, CHUNK):
                n_invalid = WINDOW - (t_c * tt + r0) if r0 < WINDOW else None
                chunk_args.append((r0, slot_c.kv[0][r0:r0 + win, :], slot_c.kv[1][r0:r0 + win, :],
                                   [r[r0:r0 + win, :] for r in slot_c.kv[2:]], n_invalid))
            yield from _chunk_stages(slot_c, chunk_args, CHUNK, lambda i: state, sinks_ref, gnw_ref, mix_ref)
            y_ref[0] = _finish(xc_ref[0], mix_ref, wout_ref, wpost_ref)
            for p in range(GLA_HEADS // 2):
                s_ref[p * LANES:(p + 1) * LANES, :] = state[p]
                sout_ref[0, p * LANES:(p + 1) * LANES, :] = state[p]

        _run_order(PROMPT_ORDER, p_gen, c_stages())

    @pl.when(s % 2 == 0)
    def _():
        body(slots[0], slots[1])

    @pl.when(s % 2 == 1)
    def _():
        body(slots[1], slots[0])


def _sample_kernel(sinks_ref, x_ref, cos_ref, sdn_ref, sup_ref, ck_ref, cv_ref, s0_ref, wpre_ref, win_ref,
                   wup_ref, bgk_ref, gnw_ref, wout_ref, wpost_ref,
                   y_ref, kout_ref, vout_ref, sout_ref, *scratch):
    slot = _make_slot(scratch[:N_SLOT_REFS])
    mix_ref = scratch[N_SLOT_REFS]
    nb = ck_ref.shape[0]
    c = x_ref.shape[0] // nb
    win = WINDOW + c

    def store_kv(k, v):
        kout_ref[...] = k
        vout_ref[...] = v
        for i in range(nb):
            rows = slice(i * c, (i + 1) * c)
            kv = _kv_variants(jnp.concatenate([ck_ref[i], k[rows]], axis=0),
                              jnp.concatenate([cv_ref[i], v[rows]], axis=0))
            for r, val in zip(slot.kv, kv):
                r[i * win:(i + 1) * win, :] = val

    x = x_ref[...]
    for _ in _project_stages(x, (cos_ref, sdn_ref, sup_ref), wpre_ref, win_ref, wup_ref, bgk_ref, c, slot, store_kv):
        pass
    states = [[s0_ref[i, p * LANES:(p + 1) * LANES, :] for p in range(GLA_HEADS // 2)] for i in range(nb)]
    chunk_args = [(i * c, slot.kv[0][i * win:(i + 1) * win, :], slot.kv[1][i * win:(i + 1) * win, :],
                   [r[i * win:(i + 1) * win, :] for r in slot.kv[2:]], None) for i in range(nb)]
    for _ in _chunk_stages(slot, chunk_args, c, lambda i: states[i], sinks_ref, gnw_ref, mix_ref):
        pass
    for i in range(nb):
        for p in range(GLA_HEADS // 2):
            sout_ref[i, p * LANES:(p + 1) * LANES, :] = states[i][p]
    y_ref[...] = _finish(x, mix_ref, wout_ref, wpost_ref)


def _rotary_tables(pos):
    half = ROT_DIM // 2
    inv = ROPE_THETA ** (-jnp.arange(half, dtype=jnp.float32) * (2.0 / ROT_DIM))
    ang = pos.astype(jnp.float32)[:, None] * inv[None, :]
    cos, sin = jnp.cos(ang), jnp.sin(ang)
    t = pos.shape[0]
    one = jnp.ones((t, HEAD_DIM - ROT_DIM), jnp.float32)
    zero = jnp.zeros((t, HEAD_DIM - ROT_DIM), jnp.float32)
    z8 = jnp.zeros((t, half), jnp.float32)
    cos_t = jnp.concatenate([cos, cos, one], axis=1)
    sin_dn = jnp.concatenate([z8, sin, zero], axis=1)
    sin_up = jnp.concatenate([-sin, z8, zero], axis=1)
    rep = LANES // HEAD_DIM
    return tuple(jnp.tile(a, (1, rep)) for a in (cos_t, sin_dn, sin_up))


def _const_spec(shape):
    zeros = (0,) * len(shape)
    return pl.BlockSpec(shape, lambda i: zeros)


def kernel(x_prompt, x_sample, cache_k, cache_v, state_gla, norm_pre_w, w_in, attn_sinks, w_gk_up, b_gk,
           gla_norm_w, w_out, norm_post_w):
    bsz, t_p, _ = x_prompt.shape
    dec_b, t_s, _ = x_sample.shape
    assert w_in.shape[0] == 1, "single layer"
    assert t_p % PROMPT_TILE == 0 and PROMPT_TILE % CHUNK == 0 and PROMPT_TILE >= WINDOW
    assert dec_b % SAMPLE_SEQS == 0 and t_s & (t_s - 1) == 0

    win = _bf(jnp.pad(w_in[0], ((0, 0), (0, D_IN_PAD - D_IN))))
    wup = _bf(jnp.pad(w_gk_up[0], ((0, LANES - GLA_RANK), (0, 0))))
    wout = _bf(w_out[0])
    wpre = norm_pre_w[0][None, :]
    wpost = norm_post_w[0][None, :]
    bgk = b_gk[0][None, :]
    gnw = gla_norm_w[0][None, :]
    sinks = attn_sinks[0][None, :]
    smem = pl.BlockSpec(memory_space=pltpu.SMEM)

    weights_specs = [
        _const_spec((1, D_MODEL)), _const_spec((D_MODEL, D_IN_PAD)), _const_spec((LANES, GQK_W)),
        _const_spec((1, GQK_W)), _const_spec((1, GLA_DV)), _const_spec((D_MODEL, D_MODEL)),
        _const_spec((1, D_MODEL))]

    tt = PROMPT_TILE
    nt = t_p // tt
    n_tiles = bsz * nt
    cos_p, sdn_p, sup_p = _rotary_tables(jnp.arange(t_p))

    def tile_p(s):
        return jnp.minimum(s, n_tiles - 1)

    def tile_c(s):
        return jnp.maximum(s - 1, 0)

    tab_spec = pl.BlockSpec((tt, LANES), lambda s: (tile_p(s) % nt, 0))
    x_spec = lambda tile: pl.BlockSpec((1, tt, D_MODEL), lambda s: (tile(s) // nt, tile(s) % nt, 0))
    seq_spec = lambda tile, d1, d2: pl.BlockSpec((1, d1, d2), lambda s: (tile(s) // nt, 0, 0))
    y_p, k_p, v_p, s_p = pl.pallas_call(
        functools.partial(_prompt_kernel, nt),
        grid=(n_tiles + 1,),
        in_specs=[smem, x_spec(tile_p), x_spec(tile_c), tab_spec, tab_spec, tab_spec] + weights_specs,
        out_specs=[x_spec(tile_c), seq_spec(tile_p, WINDOW, KV_W), seq_spec(tile_p, WINDOW, KV_W),
                   seq_spec(tile_c, S_ROWS, GLA_DV)],
        out_shape=[jax.ShapeDtypeStruct((bsz, t_p, D_MODEL), jnp.float32),
                   jax.ShapeDtypeStruct((bsz, WINDOW, KV_W), jnp.float32),
                   jax.ShapeDtypeStruct((bsz, WINDOW, KV_W), jnp.float32),
                   jax.ShapeDtypeStruct((bsz, S_ROWS, GLA_DV), jnp.float32)],
        scratch_shapes=_slot_shapes(tt, WINDOW + tt, tt // CHUNK) * 2
        + [pltpu.VMEM((S_ROWS, GLA_DV), jnp.float32), pltpu.VMEM((tt, D_MODEL), jnp.bfloat16)],
        compiler_params=pltpu.CompilerParams(dimension_semantics=("arbitrary",), vmem_limit_bytes=VMEM_LIMIT),
        name="prompt_layer",
    )(sinks, x_prompt, x_prompt, cos_p, sdn_p, sup_p, wpre, win, wup, bgk, gnw, wout, wpost)

    nb = SAMPLE_SEQS
    rows = nb * t_s
    tabs = [jnp.tile(a, (nb, 1)) for a in _rotary_tables(PAST_LEN + jnp.arange(t_s))]
    tab_spec = pl.BlockSpec((rows, LANES), lambda i: (0, 0))
    seq3 = lambda d1, d2: pl.BlockSpec((nb, d1, d2), lambda i: (i, 0, 0))
    tok = lambda d: pl.BlockSpec((rows, d), lambda i: (i, 0))
    y_s, k_s, v_s, s_s = pl.pallas_call(
        _sample_kernel,
        grid=(dec_b // nb,),
        in_specs=[smem, tok(D_MODEL), tab_spec, tab_spec, tab_spec, seq3(WINDOW, KV_W), seq3(WINDOW, KV_W),
                  seq3(S_ROWS, GLA_DV)] + weights_specs,
        out_specs=[tok(D_MODEL), tok(KV_W), tok(KV_W), seq3(S_ROWS, GLA_DV)],
        out_shape=[jax.ShapeDtypeStruct((dec_b * t_s, D_MODEL), jnp.float32),
                   jax.ShapeDtypeStruct((dec_b * t_s, KV_W), jnp.float32),
                   jax.ShapeDtypeStruct((dec_b * t_s, KV_W), jnp.float32),
                   jax.ShapeDtypeStruct((dec_b, S_ROWS, GLA_DV), jnp.float32)],
        scratch_shapes=_slot_shapes(rows, nb * (WINDOW + t_s), nb) + [pltpu.VMEM((rows, D_MODEL), jnp.bfloat16)],
        compiler_params=pltpu.CompilerParams(dimension_semantics=("arbitrary",), vmem_limit_bytes=VMEM_LIMIT),
        name="sample_layer",
    )(sinks, x_sample.reshape(dec_b * t_s, D_MODEL), *tabs,
      cache_k[0].reshape(dec_b, WINDOW, KV_W), cache_v[0].reshape(dec_b, WINDOW, KV_W),
      state_gla[0].reshape(dec_b, S_ROWS, GLA_DV), wpre, win, wup, bgk, gnw, wout, wpost)

    kv5 = lambda a, b, t: a.reshape(1, b, t, N_KV_HEADS, HEAD_DIM)
    st5 = lambda a, b: a.reshape(1, b, GLA_HEADS, GLA_DK, GLA_DV)
    return (y_p, y_s.reshape(dec_b, t_s, D_MODEL),
            kv5(k_p, bsz, WINDOW), kv5(v_p, bsz, WINDOW), st5(s_p, bsz),
            kv5(k_s, dec_b, t_s), kv5(v_s, dec_b, t_s), st5(s_s, dec_b))
```

```python
import collections
import functools

import jax
import jax.numpy as jnp
from jax import lax
from jax.experimental import pallas as pl
from jax.experimental.pallas import tpu as pltpu

D_MODEL = 1024
CHUNK = 64
WINDOW = 128
HEAD_DIM = 64
N_HEADS = 8
N_KV_HEADS = 2
GQA_GROUP = N_HEADS // N_KV_HEADS
ROT_DIM = 16
ROPE_THETA = 500000.0
GLA_HEADS = 4
GLA_DV = 128
GLA_DK = 64
GLA_RANK = 16
GATE_TAU = 16.0
NORM_EPS = 1e-6
PAST_LEN = 4096

LANES = 128
SUBLANES = 8
ATTN_W = N_HEADS * HEAD_DIM
KV_W = N_KV_HEADS * HEAD_DIM
GQK_W = GLA_HEADS * GLA_DK
GV_W = GLA_HEADS * GLA_DV
S_ROWS = GLA_HEADS * GLA_DK
OFF_AQ = 0
OFF_AK = OFF_AQ + ATTN_W
OFF_AV = OFF_AK + KV_W
OFF_AG = OFF_AV + KV_W
OFF_GQ = OFF_AG + ATTN_W
OFF_GK = OFF_GQ + GQK_W
OFF_GV = OFF_GK + GQK_W
OFF_GG = OFF_GV + GV_W
OFF_LR = OFF_GG + GV_W
D_IN = OFF_LR + GLA_RANK
D_IN_PAD = OFF_LR + LANES

PROMPT_TILE = 256
SAMPLE_SEQS = 8
VMEM_LIMIT = 48 * 1024 * 1024
PROMPT_ORDER = "CPCPCPCPCPPPP"
CHUNK_GROUP = 4

NEG = -1e30
LOG2E = 1.4426950408889634

Slot = collections.namedtuple("Slot", "qs kv ag qin kog kdec dec gv gg")


def _slot_shapes(rows, kv_rows, n_chunks):
    bf, f32 = jnp.bfloat16, jnp.float32
    return ([pltpu.VMEM((rows, 2 * ATTN_W), bf)] + [pltpu.VMEM((kv_rows, LANES), bf) for _ in range(6)]
            + [pltpu.VMEM((rows, ATTN_W), f32), pltpu.VMEM((rows, 2 * GQK_W), bf), pltpu.VMEM((rows, GQK_W), bf),
               pltpu.VMEM((rows, GQK_W), f32), pltpu.VMEM((n_chunks * SUBLANES, GQK_W), f32),
               pltpu.VMEM((rows, GV_W), bf), pltpu.VMEM((rows, GV_W), f32)])


N_SLOT_REFS = 14


def _make_slot(refs):
    return Slot(refs[0], tuple(refs[1:7]), *refs[7:14])


def _dot(a, b):
    return jnp.dot(a, b, preferred_element_type=jnp.float32)


def _dot_nt(a, b):
    return lax.dot_general(a, b, (((1,), (1,)), ((), ())), preferred_element_type=jnp.float32)


def _bf(x):
    return x.astype(jnp.bfloat16)


def _lane_masks():
    lane = lax.broadcasted_iota(jnp.int32, (1, LANES), 1)
    return lane < HEAD_DIM, lane >= HEAD_DIM


def _silu(x):
    return x * (1.0 / (1.0 + jnp.exp(-x)))


def _log_sigmoid(x):
    return jnp.minimum(x, 0.0) - jnp.log(1.0 + jnp.exp(-jnp.abs(x)))


def _rms_scale(x):
    return lax.rsqrt(jnp.mean(x * x, axis=-1, keepdims=True) + NORM_EPS)


def _chunk_cumsum(g, chunk):
    row = lax.broadcasted_iota(jnp.int32, g.shape, 0) & (chunk - 1)
    s = 1
    while s < chunk:
        g = g + jnp.where(row >= s, pltpu.roll(g, s, 0), 0.0)
        s *= 2
    return g


def _rotary(xcol, cos_t, sin_dn, sin_up):
    return (xcol * cos_t + pltpu.roll(xcol, ROT_DIM // 2, 1) * sin_dn
            + pltpu.roll(xcol, LANES - ROT_DIM // 2, 1) * sin_up)


def _kv_variants(k, v):
    m_lo, m_hi = _lane_masks()
    k_sw = pltpu.roll(k, HEAD_DIM, 1)
    v_sw = pltpu.roll(v, HEAD_DIM, 1)
    return (_bf(k), _bf(k_sw),
            _bf(jnp.where(m_lo, v, 0.0)), _bf(jnp.where(m_hi, v_sw, 0.0)),
            _bf(jnp.where(m_lo, v_sw, 0.0)), _bf(jnp.where(m_hi, v, 0.0)))


def _store_halves(ref, j, width, col):
    m_lo, m_hi = _lane_masks()
    ref[:, j * LANES:(j + 1) * LANES] = _bf(jnp.where(m_lo, col, 0.0))
    ref[:, width + j * LANES:width + (j + 1) * LANES] = _bf(jnp.where(m_hi, col, 0.0))


def _project_stages(x, tabs, wpre_ref, win_ref, wup_ref, bgk_ref, chunk, slot, store_kv):
    rows = x.shape[0]
    h = _bf(x * _rms_scale(x) * wpre_ref[...])
    cos_t, sin_dn, sin_up = (t[...] for t in tabs)

    def seg(off, width):
        return _dot(h, win_ref[:, off:off + width])

    yield
    q = seg(OFF_AQ, ATTN_W)
    for j in range(ATTN_W // LANES):
        ln = slice(j * LANES, (j + 1) * LANES)
        _store_halves(slot.qs, j, ATTN_W, _rotary(q[:, ln], cos_t, sin_dn, sin_up) * (LOG2E * HEAD_DIM ** -0.5))
    yield
    kv = seg(OFF_AK, 2 * KV_W)
    store_kv(_rotary(kv[:, :KV_W], cos_t, sin_dn, sin_up), kv[:, KV_W:])
    yield
    slot.ag[...] = _silu(seg(OFF_AG, ATTN_W))
    yield
    lr = _bf(seg(OFF_LR, LANES))
    log_a = _log_sigmoid(_dot(lr, wup_ref[...]) + bgk_ref[...]) * (1.0 / GATE_TAU)
    b = _chunk_cumsum(log_a, chunk)
    lasts = [b[r + chunk - 1:r + chunk] for r in range(0, rows, chunk)]
    for i, bl in enumerate(lasts):
        slot.dec[i * SUBLANES:(i + 1) * SUBLANES, :] = jnp.broadcast_to(jnp.exp(bl), (SUBLANES, GQK_W))
    b_last = jnp.concatenate([jnp.broadcast_to(bl, (chunk, GQK_W)) for bl in lasts], axis=0)
    q_in = seg(OFF_GQ, GQK_W) * (GLA_DK ** -0.5) * jnp.exp(b)
    for j in range(GQK_W // LANES):
        _store_halves(slot.qin, j, GQK_W, q_in[:, j * LANES:(j + 1) * LANES])
    yield
    gk = seg(OFF_GK, GQK_W)
    slot.kog[...] = _bf(gk * jnp.exp(-b))
    slot.kdec[...] = gk * jnp.exp(b_last - b)
    yield
    slot.gv[...] = _bf(seg(OFF_GV, GV_W))
    yield
    slot.gg[...] = _silu(seg(OFF_GG, GV_W))


def _attend_stages(slot, rows, kv_win, sinks_ref, n_invalid, emit):
    c = rows.stop - rows.start
    lo = [slot.qs[rows, j * LANES:(j + 1) * LANES] for j in range(ATTN_W // LANES)]
    hi = [slot.qs[rows, ATTN_W + j * LANES:ATTN_W + (j + 1) * LANES] for j in range(ATTN_W // LANES)]
    k_a, k_b = kv_win(0), kv_win(1)
    n_hi = k_a.shape[0] - LANES
    zrows = jnp.zeros((LANES - n_hi, LANES), jnp.bfloat16)
    ext = lambda a: jnp.concatenate([a, zrows], axis=0)
    s_a = _dot_nt(jnp.concatenate([lo[0], lo[1], hi[2], hi[3]], axis=0), ext(k_a))
    s_b = _dot_nt(jnp.concatenate([hi[0], hi[1], lo[2], lo[3]], axis=0), ext(k_b))
    yield
    lane = lax.broadcasted_iota(jnp.int32, (1, LANES), 1)
    where = {0: (s_a, 0), 2: (s_a, 1), 5: (s_a, 2), 7: (s_a, 3),
             1: (s_b, 0), 3: (s_b, 1), 4: (s_b, 2), 6: (s_b, 3)}
    outs = []
    for h in range(N_HEADS):
        src, blk = where[h]
        s = src[blk * c:(blk + 1) * c]
        pad_row = jnp.where(lane == n_hi, sinks_ref[0, h] * LOG2E, NEG)
        s_lo = s[:, :LANES]
        s_hi = jnp.where(lane >= n_hi, pad_row, s[:, LANES:])
        if n_invalid is not None:
            s_lo = jnp.where(lane < n_invalid, NEG, s_lo)
        m = jnp.max(jnp.maximum(s_lo, s_hi), axis=-1, keepdims=True)
        p_lo = jnp.exp2(s_lo - m)
        p_hi = jnp.exp2(s_hi - m)
        denom = jnp.sum(p_lo + p_hi, axis=-1, keepdims=True)
        o = _dot(_bf(jnp.concatenate([p_lo, p_hi], axis=1)), ext(kv_win(2 + 2 * (h // GQA_GROUP) + h % 2)))
        outs.append(o * (1.0 / denom))
    yield
    emit(jnp.concatenate([outs[2 * j] + outs[2 * j + 1] for j in range(4)], axis=1))


def _gla_stages(slot, r0, c, state, gnw_ref, mix_ref):
    rows = slice(r0, r0 + c)
    ci0 = (r0 // c) * SUBLANES
    pad = LANES - c
    lhs, a2, uw, dcol, vw = [], [], [], [], []
    for p in range(GLA_HEADS // 2):
        ln = slice(p * LANES, (p + 1) * LANES)
        vwide = slot.gv[rows, p * 2 * GLA_DV:(p + 1) * 2 * GLA_DV]
        lhs.append(jnp.concatenate([slot.qin[rows, ln], slot.qin[rows, GQK_W + p * LANES:GQK_W + (p + 1) * LANES]],
                                   axis=0))
        a2.append(_dot_nt(lhs[p], slot.kog[rows, ln]))
        kd_t = jnp.concatenate([slot.kdec[rows, ln], jnp.zeros((pad, LANES), jnp.float32)], axis=0).T
        v_pad = jnp.concatenate([vwide, jnp.zeros((pad, 2 * GLA_DV), jnp.bfloat16)], axis=0)
        uw.append(_dot(_bf(kd_t), v_pad))
        dcol.append(jnp.broadcast_to(slot.dec[ci0:ci0 + 1, ln], (LANES, LANES)).T)
        vw.append(vwide)
    yield
    outs = []
    ri = lax.broadcasted_iota(jnp.int32, a2[0].shape, 0) & (c - 1)
    ci = lax.broadcasted_iota(jnp.int32, a2[0].shape, 1)
    for p in range(GLA_HEADS // 2):
        am = _bf(jnp.where(ri >= ci, a2[p], 0.0))
        inter = _dot(lhs[p], _bf(state[p]))
        outs.append(_dot(am[:c], vw[p][:, :GLA_DV]) + inter[:c])
        outs.append(_dot(am[c:], vw[p][:, GLA_DV:]) + inter[c:])
        u_pair = jnp.concatenate([uw[p][:GLA_DK, :GLA_DV], uw[p][GLA_DK:, GLA_DV:]], axis=0)
        state[p] = dcol[p] * state[p] + u_pair
    yield
    gnw = gnw_ref[...]
    for hh, o in enumerate(outs):
        y = o * _rms_scale(o) * gnw * slot.gg[rows, hh * GLA_DV:(hh + 1) * GLA_DV]
        mix_ref[rows, ATTN_W + hh * GLA_DV:ATTN_W + (hh + 1) * GLA_DV] = _bf(y)


def _lockstep(gens):
    live = list(gens)
    while live:
        for g in list(live):
            try:
                next(g)
            except StopIteration:
                live.remove(g)
        yield


def _chunk_stages(slot, chunk_args, c, state_of, sinks_ref, gnw_ref, mix_ref):
    gens = []
    for i, (r0, kv_win, n_invalid) in enumerate(chunk_args):
        rows = slice(r0, r0 + c)

        def emit(attn, rows=rows):
            mix_ref[rows, 0:ATTN_W] = _bf(attn * slot.ag[rows, :])

        gens.append(_attend_stages(slot, rows, kv_win, sinks_ref, n_invalid, emit))
        gens.append(_gla_stages(slot, r0, c, state_of(i), gnw_ref, mix_ref))
    for g in range(0, len(gens), 2 * CHUNK_GROUP):
        yield from _lockstep(gens[g:g + 2 * CHUNK_GROUP])


def _finish(x, mix_ref, wout_ref, wpost_ref):
    mix = _dot(mix_ref[...], wout_ref[...])
    return x + mix * _rms_scale(mix) * wpost_ref[...]


def _run_order(order, p_gen, c_gen):
    for ch in order:
        next(p_gen if ch == "P" else c_gen, None)
    for g in (p_gen, c_gen):
        for _ in g:
            pass


def _prompt_kernel(nt, sinks_ref, xp_ref, xc_ref, cos_ref, sdn_ref, sup_ref, wpre_ref, win_ref, wup_ref, bgk_ref,
                   gnw_ref, wout_ref, wpost_ref,
                   y_ref, kout_ref, vout_ref, sout_ref, *scratch):
    slots = (_make_slot(scratch[:N_SLOT_REFS]), _make_slot(scratch[N_SLOT_REFS:2 * N_SLOT_REFS]))
    s_ref, mix_ref = scratch[2 * N_SLOT_REFS:]
    s = pl.program_id(0)
    tt = xp_ref.shape[1]
    t_c = jnp.maximum(s - 1, 0) % nt

    @pl.when(s == 0)
    def _():
        for r in scratch[N_SLOT_REFS:]:
            r[...] = jnp.zeros_like(r)

    def body(slot_p, slot_c):
        def store_kv(k, v):
            kout_ref[0] = k[tt - WINDOW:]
            vout_ref[0] = v[tt - WINDOW:]
            for r, rc, val in zip(slot_p.kv, slot_c.kv, _kv_variants(k, v)):
                r[WINDOW:WINDOW + tt, :] = val
                r[0:WINDOW, :] = rc[tt:tt + WINDOW, :]

        p_gen = _project_stages(xp_ref[0], (cos_ref, sdn_ref, sup_ref), wpre_ref, win_ref, wup_ref, bgk_ref,
                                CHUNK, slot_p, store_kv)

        def c_stages():
            keep = t_c != 0
            state = [jnp.where(keep, s_ref[p * LANES:(p + 1) * LANES, :], 0.0) for p in range(GLA_HEADS // 2)]
            win = WINDOW + CHUNK
            chunk_args = []
            for r0 in range(0, tt, CHUNK):
                n_invalid = WINDOW - (t_c * tt + r0) if r0 < WINDOW else None
                chunk_args.append((r0, lambda i, r0=r0: slot_c.kv[i][r0:r0 + win, :], n_invalid))
            yield from _chunk_stages(slot_c, chunk_args, CHUNK, lambda i: state, sinks_ref, gnw_ref, mix_ref)
            y_ref[0] = _finish(xc_ref[0], mix_ref, wout_ref, wpost_ref)
            for p in range(GLA_HEADS // 2):
                s_ref[p * LANES:(p + 1) * LANES, :] = state[p]
                sout_ref[0, p * LANES:(p + 1) * LANES, :] = state[p]

        _run_order(PROMPT_ORDER, p_gen, c_stages())

    @pl.when(s % 2 == 0)
    def _():
        body(slots[0], slots[1])

    @pl.when(s % 2 == 1)
    def _():
        body(slots[1], slots[0])


def _sample_kernel(sinks_ref, x_ref, cos_ref, sdn_ref, sup_ref, ck_ref, cv_ref, s0_ref, wpre_ref, win_ref,
                   wup_ref, bgk_ref, gnw_ref, wout_ref, wpost_ref,
                   y_ref, kout_ref, vout_ref, sout_ref, *scratch):
    slot = _make_slot(scratch[:N_SLOT_REFS])
    mix_ref = scratch[N_SLOT_REFS]
    nb = ck_ref.shape[0]
    c = x_ref.shape[0] // nb
    win = WINDOW + c

    def store_kv(k, v):
        kout_ref[...] = k
        vout_ref[...] = v
        for i in range(nb):
            rows = slice(i * c, (i + 1) * c)
            kv = _kv_variants(jnp.concatenate([ck_ref[i], k[rows]], axis=0),
                              jnp.concatenate([cv_ref[i], v[rows]], axis=0))
            for r, val in zip(slot.kv, kv):
                r[i * win:(i + 1) * win, :] = val

    x = x_ref[...]
    for _ in _project_stages(x, (cos_ref, sdn_ref, sup_ref), wpre_ref, win_ref, wup_ref, bgk_ref, c, slot, store_kv):
        pass
    states = [[s0_ref[i, p * LANES:(p + 1) * LANES, :] for p in range(GLA_HEADS // 2)] for i in range(nb)]
    chunk_args = [(i * c, lambda j, i=i: slot.kv[j][i * win:(i + 1) * win, :], None) for i in range(nb)]
    for _ in _chunk_stages(slot, chunk_args, c, lambda i: states[i], sinks_ref, gnw_ref, mix_ref):
        pass
    for i in range(nb):
        for p in range(GLA_HEADS // 2):
            sout_ref[i, p * LANES:(p + 1) * LANES, :] = states[i][p]
    y_ref[...] = _finish(x, mix_ref, wout_ref, wpost_ref)


def _rotary_tables(pos):
    half = ROT_DIM // 2
    d = jnp.arange(LANES) % HEAD_DIM
    inv = ROPE_THETA ** (-(d % half).astype(jnp.float32) * (2.0 / ROT_DIM))
    ang = pos.astype(jnp.float32)[:, None] * inv[None, :]
    cos, sin = jnp.cos(ang), jnp.sin(ang)
    cos_t = jnp.where(d < ROT_DIM, cos, 1.0)
    sin_dn = jnp.where((d >= half) & (d < ROT_DIM), sin, 0.0)
    sin_up = jnp.where(d < half, -sin, 0.0)
    return cos_t, sin_dn, sin_up


def _cast_weights_kernel(win_ref, wout_ref, wup_ref, win_o, wout_o, wup_o):
    win_o[:, OFF_LR:] = jnp.zeros((win_o.shape[0], LANES), jnp.bfloat16)
    win_o[:, :D_IN] = _bf(win_ref[...])
    wout_o[...] = _bf(wout_ref[...])
    wup_o[...] = jnp.zeros_like(wup_o)
    wup_o[:GLA_RANK, :] = _bf(wup_ref[...])


def _cast_weights(w_in, w_out, w_up):
    n_blk = 8
    rb = D_MODEL // n_blk
    return pl.pallas_call(
        _cast_weights_kernel,
        grid=(n_blk,),
        in_specs=[pl.BlockSpec((rb, D_IN), lambda i: (i, 0)), pl.BlockSpec((rb, D_MODEL), lambda i: (i, 0)),
                  pl.BlockSpec((GLA_RANK, GQK_W), lambda i: (0, 0))],
        out_specs=[pl.BlockSpec((rb, D_IN_PAD), lambda i: (i, 0)), pl.BlockSpec((rb, D_MODEL), lambda i: (i, 0)),
                   pl.BlockSpec((LANES, GQK_W), lambda i: (0, 0))],
        out_shape=[jax.ShapeDtypeStruct((D_MODEL, D_IN_PAD), jnp.bfloat16),
                   jax.ShapeDtypeStruct((D_MODEL, D_MODEL), jnp.bfloat16),
                   jax.ShapeDtypeStruct((LANES, GQK_W), jnp.bfloat16)],
        compiler_params=pltpu.CompilerParams(dimension_semantics=("arbitrary",)),
        name="cast_weights",
    )(w_in, w_out, w_up)


def _const_spec(shape):
    zeros = (0,) * len(shape)
    return pl.BlockSpec(shape, lambda i: zeros)


def kernel(x_prompt, x_sample, cache_k, cache_v, state_gla, norm_pre_w, w_in, attn_sinks, w_gk_up, b_gk,
           gla_norm_w, w_out, norm_post_w):
    bsz, t_p, _ = x_prompt.shape
    dec_b, t_s, _ = x_sample.shape
    assert w_in.shape[0] == 1, "single layer"
    assert t_p % PROMPT_TILE == 0 and PROMPT_TILE % CHUNK == 0 and PROMPT_TILE >= WINDOW
    assert dec_b % SAMPLE_SEQS == 0 and t_s & (t_s - 1) == 0

    win, wout, wup = _cast_weights(w_in[0], w_out[0], w_gk_up[0])
    wpre = norm_pre_w[0][None, :]
    wpost = norm_post_w[0][None, :]
    bgk = b_gk[0][None, :]
    gnw = gla_norm_w[0][None, :]
    sinks = attn_sinks[0][None, :]
    smem = pl.BlockSpec(memory_space=pltpu.SMEM)

    weights_specs = [
        _const_spec((1, D_MODEL)), _const_spec((D_MODEL, D_IN_PAD)), _const_spec((LANES, GQK_W)),
        _const_spec((1, GQK_W)), _const_spec((1, GLA_DV)), _const_spec((D_MODEL, D_MODEL)),
        _const_spec((1, D_MODEL))]

    tt = PROMPT_TILE
    nt = t_p // tt
    n_tiles = bsz * nt
    cos_p, sdn_p, sup_p = _rotary_tables(jnp.arange(t_p))

    def tile_p(s):
        return jnp.minimum(s, n_tiles - 1)

    def tile_c(s):
        return jnp.maximum(s - 1, 0)

    tab_spec = pl.BlockSpec((tt, LANES), lambda s: (tile_p(s) % nt, 0))
    x_spec = lambda tile: pl.BlockSpec((1, tt, D_MODEL), lambda s: (tile(s) // nt, tile(s) % nt, 0))
    seq_spec = lambda tile, d1, d2: pl.BlockSpec((1, d1, d2), lambda s: (tile(s) // nt, 0, 0))
    y_p, k_p, v_p, s_p = pl.pallas_call(
        functools.partial(_prompt_kernel, nt),
        grid=(n_tiles + 1,),
        in_specs=[smem, x_spec(tile_p), x_spec(tile_c), tab_spec, tab_spec, tab_spec] + weights_specs,
        out_specs=[x_spec(tile_c), seq_spec(tile_p, WINDOW, KV_W), seq_spec(tile_p, WINDOW, KV_W),
                   seq_spec(tile_c, S_ROWS, GLA_DV)],
        out_shape=[jax.ShapeDtypeStruct((bsz, t_p, D_MODEL), jnp.float32),
                   jax.ShapeDtypeStruct((bsz, WINDOW, KV_W), jnp.float32),
                   jax.ShapeDtypeStruct((bsz, WINDOW, KV_W), jnp.float32),
                   jax.ShapeDtypeStruct((bsz, S_ROWS, GLA_DV), jnp.float32)],
        scratch_shapes=_slot_shapes(tt, WINDOW + tt, tt // CHUNK) * 2
        + [pltpu.VMEM((S_ROWS, GLA_DV), jnp.float32), pltpu.VMEM((tt, D_MODEL), jnp.bfloat16)],
        compiler_params=pltpu.CompilerParams(dimension_semantics=("arbitrary",), vmem_limit_bytes=VMEM_LIMIT),
        name="prompt_layer",
    )(sinks, x_prompt, x_prompt, cos_p, sdn_p, sup_p, wpre, win, wup, bgk, gnw, wout, wpost)

    nb = SAMPLE_SEQS
    rows = nb * t_s
    tabs = [jnp.tile(a, (nb, 1)) for a in _rotary_tables(PAST_LEN + jnp.arange(t_s))]
    tab_spec = pl.BlockSpec((rows, LANES), lambda i: (0, 0))
    seq3 = lambda d1, d2: pl.BlockSpec((nb, d1, d2), lambda i: (i, 0, 0))
    tok = lambda d: pl.BlockSpec((rows, d), lambda i: (i, 0))
    y_s, k_s, v_s, s_s = pl.pallas_call(
        _sample_kernel,
        grid=(dec_b // nb,),
        in_specs=[smem, tok(D_MODEL), tab_spec, tab_spec, tab_spec, seq3(WINDOW, KV_W), seq3(WINDOW, KV_W),
                  seq3(S_ROWS, GLA_DV)] + weights_specs,
        out_specs=[tok(D_MODEL), tok(KV_W), tok(KV_W), seq3(S_ROWS, GLA_DV)],
        out_shape=[jax.ShapeDtypeStruct((dec_b * t_s, D_MODEL), jnp.float32),
                   jax.ShapeDtypeStruct((dec_b * t_s, KV_W), jnp.float32),
                   jax.ShapeDtypeStruct((dec_b * t_s, KV_W), jnp.float32),
                   jax.ShapeDtypeStruct((dec_b, S_ROWS, GLA_DV), jnp.float32)],
        scratch_shapes=_slot_shapes(rows, nb * (WINDOW + t_s), nb) + [pltpu.VMEM((rows, D_MODEL), jnp.bfloat16)],
        compiler_params=pltpu.CompilerParams(dimension_semantics=("arbitrary",), vmem_limit_bytes=VMEM_LIMIT),
        name="sample_layer",
    )(sinks, x_sample.reshape(dec_b * t_s, D_MODEL), *tabs,
      cache_k[0].reshape(dec_b, WINDOW, KV_W), cache_v[0].reshape(dec_b, WINDOW, KV_W),
      state_gla[0].reshape(dec_b, S_ROWS, GLA_DV), wpre, win, wup, bgk, gnw, wout, wpost)

    kv5 = lambda a, b, t: a.reshape(1, b, t, N_KV_HEADS, HEAD_DIM)
    st5 = lambda a, b: a.reshape(1, b, GLA_HEADS, GLA_DK, GLA_DV)
    return (y_p, y_s.reshape(dec_b, t_s, D_MODEL),
            kv5(k_p, bsz, WINDOW), kv5(v_p, bsz, WINDOW), st5(s_p, bsz),
            kv5(k_s, dec_b, t_s), kv5(v_s, dec_b, t_s), st5(s_s, dec_b))
```

```python
import collections
import functools

import jax
import jax.numpy as jnp
from jax import lax
from jax.experimental import pallas as pl
from jax.experimental.pallas import tpu as pltpu

D_MODEL = 1024
CHUNK = 64
WINDOW = 128
HEAD_DIM = 64
N_HEADS = 8
N_KV_HEADS = 2
GQA_GROUP = N_HEADS // N_KV_HEADS
ROT_DIM = 16
ROPE_THETA = 500000.0
GLA_HEADS = 4
GLA_DV = 128
GLA_DK = 64
GLA_RANK = 16
GATE_TAU = 16.0
NORM_EPS = 1e-6
PAST_LEN = 4096

LANES = 128
SUBLANES = 8
ATTN_W = N_HEADS * HEAD_DIM
KV_W = N_KV_HEADS * HEAD_DIM
GQK_W = GLA_HEADS * GLA_DK
GV_W = GLA_HEADS * GLA_DV
S_ROWS = GLA_HEADS * GLA_DK
OFF_AQ = 0
OFF_AK = OFF_AQ + ATTN_W
OFF_AV = OFF_AK + KV_W
OFF_AG = OFF_AV + KV_W
OFF_GQ = OFF_AG + ATTN_W
OFF_GK = OFF_GQ + GQK_W
OFF_GV = OFF_GK + GQK_W
OFF_GG = OFF_GV + GV_W
OFF_LR = OFF_GG + GV_W
D_IN = OFF_LR + GLA_RANK
D_IN_PAD = OFF_LR + LANES
CAST_BLOCK = 256
W_IN_ALLOC = -(-D_IN_PAD // CAST_BLOCK) * CAST_BLOCK

PROMPT_TILE = 256
SAMPLE_SEQS = 8
VMEM_LIMIT = 48 * 1024 * 1024
PROMPT_ORDER = "CPCPCPCPCPPPP"
CHUNK_GROUP = 4

NEG = -1e30
LOG2E = 1.4426950408889634

Slot = collections.namedtuple("Slot", "qs kv ag qin kog kdec dec gv gg")


def _slot_shapes(rows, kv_rows, n_chunks):
    bf, f32 = jnp.bfloat16, jnp.float32
    return ([pltpu.VMEM((rows, 2 * ATTN_W), bf)] + [pltpu.VMEM((kv_rows, LANES), bf) for _ in range(6)]
            + [pltpu.VMEM((rows, ATTN_W), f32), pltpu.VMEM((rows, 2 * GQK_W), bf), pltpu.VMEM((rows, GQK_W), bf),
               pltpu.VMEM((rows, GQK_W), f32), pltpu.VMEM((n_chunks * SUBLANES, GQK_W), f32),
               pltpu.VMEM((rows, GV_W), bf), pltpu.VMEM((rows, GV_W), f32)])


N_SLOT_REFS = 14


def _make_slot(refs):
    return Slot(refs[0], tuple(refs[1:7]), *refs[7:14])


def _dot(a, b):
    return jnp.dot(a, b, preferred_element_type=jnp.float32)


def _dot_nt(a, b):
    return lax.dot_general(a, b, (((1,), (1,)), ((), ())), preferred_element_type=jnp.float32)


def _bf(x):
    return x.astype(jnp.bfloat16)


def _lane_masks():
    lane = lax.broadcasted_iota(jnp.int32, (1, LANES), 1)
    return lane < HEAD_DIM, lane >= HEAD_DIM


def _silu(x):
    return x * (1.0 / (1.0 + jnp.exp(-x)))


def _log_sigmoid(x):
    return jnp.minimum(x, 0.0) - jnp.log(1.0 + jnp.exp(-jnp.abs(x)))


def _rms_scale(x):
    return lax.rsqrt(jnp.mean(x * x, axis=-1, keepdims=True) + NORM_EPS)


def _chunk_cumsum(g, chunk):
    row = lax.broadcasted_iota(jnp.int32, g.shape, 0) & (chunk - 1)
    s = 1
    while s < chunk:
        g = g + jnp.where(row >= s, pltpu.roll(g, s, 0), 0.0)
        s *= 2
    return g


def _rotary(xcol, cos_t, sin_dn, sin_up):
    return (xcol * cos_t + pltpu.roll(xcol, ROT_DIM // 2, 1) * sin_dn
            + pltpu.roll(xcol, LANES - ROT_DIM // 2, 1) * sin_up)


def _kv_variants(k, v):
    m_lo, m_hi = _lane_masks()
    k_sw = pltpu.roll(k, HEAD_DIM, 1)
    v_sw = pltpu.roll(v, HEAD_DIM, 1)
    return (_bf(k), _bf(k_sw),
            _bf(jnp.where(m_lo, v, 0.0)), _bf(jnp.where(m_hi, v_sw, 0.0)),
            _bf(jnp.where(m_lo, v_sw, 0.0)), _bf(jnp.where(m_hi, v, 0.0)))


def _store_halves(ref, j, width, col):
    m_lo, m_hi = _lane_masks()
    ref[:, j * LANES:(j + 1) * LANES] = _bf(jnp.where(m_lo, col, 0.0))
    ref[:, width + j * LANES:width + (j + 1) * LANES] = _bf(jnp.where(m_hi, col, 0.0))


def _project_stages(x, tabs, wpre_ref, win_ref, wup_ref, bgk_ref, chunk, slot, store_kv):
    rows = x.shape[0]
    h = _bf(x * _rms_scale(x) * wpre_ref[...])
    cos_t, sin_dn, sin_up = (t[...] for t in tabs)

    def seg(off, width):
        return _dot(h, win_ref[:, off:off + width])

    yield
    q = seg(OFF_AQ, ATTN_W)
    for j in range(ATTN_W // LANES):
        ln = slice(j * LANES, (j + 1) * LANES)
        _store_halves(slot.qs, j, ATTN_W, _rotary(q[:, ln], cos_t, sin_dn, sin_up) * (LOG2E * HEAD_DIM ** -0.5))
    yield
    kv = seg(OFF_AK, 2 * KV_W)
    store_kv(_rotary(kv[:, :KV_W], cos_t, sin_dn, sin_up), kv[:, KV_W:])
    yield
    slot.ag[...] = _silu(seg(OFF_AG, ATTN_W))
    yield
    lr = _bf(seg(OFF_LR, LANES))
    log_a = _log_sigmoid(_dot(lr, wup_ref[...]) + bgk_ref[...]) * (1.0 / GATE_TAU)
    b = _chunk_cumsum(log_a, chunk)
    lasts = [b[r + chunk - 1:r + chunk] for r in range(0, rows, chunk)]
    for i, bl in enumerate(lasts):
        slot.dec[i * SUBLANES:(i + 1) * SUBLANES, :] = jnp.broadcast_to(jnp.exp(bl), (SUBLANES, GQK_W))
    b_last = jnp.concatenate([jnp.broadcast_to(bl, (chunk, GQK_W)) for bl in lasts], axis=0)
    q_in = seg(OFF_GQ, GQK_W) * (GLA_DK ** -0.5) * jnp.exp(b)
    for j in range(GQK_W // LANES):
        _store_halves(slot.qin, j, GQK_W, q_in[:, j * LANES:(j + 1) * LANES])
    yield
    gk = seg(OFF_GK, GQK_W)
    slot.kog[...] = _bf(gk * jnp.exp(-b))
    slot.kdec[...] = gk * jnp.exp(b_last - b)
    yield
    slot.gv[...] = _bf(seg(OFF_GV, GV_W))
    yield
    slot.gg[...] = _silu(seg(OFF_GG, GV_W))


def _attend_stages(slot, rows, kv_win, sinks_ref, n_invalid, emit):
    c = rows.stop - rows.start
    lo = [slot.qs[rows, j * LANES:(j + 1) * LANES] for j in range(ATTN_W // LANES)]
    hi = [slot.qs[rows, ATTN_W + j * LANES:ATTN_W + (j + 1) * LANES] for j in range(ATTN_W // LANES)]
    k_a, k_b = kv_win(0), kv_win(1)
    n_hi = k_a.shape[0] - LANES
    zrows = jnp.zeros((LANES - n_hi, LANES), jnp.bfloat16)
    ext = lambda a: jnp.concatenate([a, zrows], axis=0)
    s_a = _dot_nt(jnp.concatenate([lo[0], lo[1], hi[2], hi[3]], axis=0), ext(k_a))
    s_b = _dot_nt(jnp.concatenate([hi[0], hi[1], lo[2], lo[3]], axis=0), ext(k_b))
    yield
    lane = lax.broadcasted_iota(jnp.int32, (1, LANES), 1)
    where = {0: (s_a, 0), 2: (s_a, 1), 5: (s_a, 2), 7: (s_a, 3),
             1: (s_b, 0), 3: (s_b, 1), 4: (s_b, 2), 6: (s_b, 3)}
    outs = []
    for h in range(N_HEADS):
        src, blk = where[h]
        s = src[blk * c:(blk + 1) * c]
        pad_row = jnp.where(lane == n_hi, sinks_ref[0, h] * LOG2E, NEG)
        s_lo = s[:, :LANES]
        s_hi = jnp.where(lane >= n_hi, pad_row, s[:, LANES:])
        if n_invalid is not None:
            s_lo = jnp.where(lane < n_invalid, NEG, s_lo)
        m = jnp.max(jnp.maximum(s_lo, s_hi), axis=-1, keepdims=True)
        p_lo = jnp.exp2(s_lo - m)
        p_hi = jnp.exp2(s_hi - m)
        denom = jnp.sum(p_lo + p_hi, axis=-1, keepdims=True)
        o = _dot(_bf(jnp.concatenate([p_lo, p_hi], axis=1)), ext(kv_win(2 + 2 * (h // GQA_GROUP) + h % 2)))
        outs.append(o * (1.0 / denom))
    yield
    emit(jnp.concatenate([outs[2 * j] + outs[2 * j + 1] for j in range(4)], axis=1))


def _gla_stages(slot, r0, c, state, gnw_ref, mix_ref):
    rows = slice(r0, r0 + c)
    ci0 = (r0 // c) * SUBLANES
    pad = LANES - c
    lhs, a2, uw, dcol, vw = [], [], [], [], []
    for p in range(GLA_HEADS // 2):
        ln = slice(p * LANES, (p + 1) * LANES)
        vwide = slot.gv[rows, p * 2 * GLA_DV:(p + 1) * 2 * GLA_DV]
        lhs.append(jnp.concatenate([slot.qin[rows, ln], slot.qin[rows, GQK_W + p * LANES:GQK_W + (p + 1) * LANES]],
                                   axis=0))
        a2.append(_dot_nt(lhs[p], slot.kog[rows, ln]))
        kd_t = jnp.concatenate([slot.kdec[rows, ln], jnp.zeros((pad, LANES), jnp.float32)], axis=0).T
        v_pad = jnp.concatenate([vwide, jnp.zeros((pad, 2 * GLA_DV), jnp.bfloat16)], axis=0)
        uw.append(_dot(_bf(kd_t), v_pad))
        dcol.append(jnp.broadcast_to(slot.dec[ci0:ci0 + 1, ln], (LANES, LANES)).T)
        vw.append(vwide)
    yield
    outs = []
    ri = lax.broadcasted_iota(jnp.int32, a2[0].shape, 0) & (c - 1)
    ci = lax.broadcasted_iota(jnp.int32, a2[0].shape, 1)
    for p in range(GLA_HEADS // 2):
        am = _bf(jnp.where(ri >= ci, a2[p], 0.0))
        sb = _bf(state[p])
        for i in range(2):
            hr = slice(i * c, (i + 1) * c)
            outs.append(_dot(jnp.concatenate([lhs[p][hr], am[hr]], axis=1),
                             jnp.concatenate([sb, vw[p][:, i * GLA_DV:(i + 1) * GLA_DV]], axis=0)))
        u_pair = jnp.concatenate([uw[p][:GLA_DK, :GLA_DV], uw[p][GLA_DK:, GLA_DV:]], axis=0)
        state[p] = dcol[p] * state[p] + u_pair
    yield
    gnw = gnw_ref[...]
    for hh, o in enumerate(outs):
        y = o * _rms_scale(o) * gnw * slot.gg[rows, hh * GLA_DV:(hh + 1) * GLA_DV]
        mix_ref[rows, ATTN_W + hh * GLA_DV:ATTN_W + (hh + 1) * GLA_DV] = _bf(y)


def _lockstep(gens):
    live = list(gens)
    while live:
        for g in list(live):
            try:
                next(g)
            except StopIteration:
                live.remove(g)
        yield


def _chunk_stages(slot, chunk_args, c, state_of, sinks_ref, gnw_ref, mix_ref):
    gens = []
    for i, (r0, kv_win, n_invalid) in enumerate(chunk_args):
        rows = slice(r0, r0 + c)

        def emit(attn, rows=rows):
            mix_ref[rows, 0:ATTN_W] = _bf(attn * slot.ag[rows, :])

        gens.append(_attend_stages(slot, rows, kv_win, sinks_ref, n_invalid, emit))
        gens.append(_gla_stages(slot, r0, c, state_of(i), gnw_ref, mix_ref))
    for g in range(0, len(gens), 2 * CHUNK_GROUP):
        yield from _lockstep(gens[g:g + 2 * CHUNK_GROUP])


def _finish(x, mix_ref, wout_ref, wpost_ref):
    mix = _dot(mix_ref[...], wout_ref[...])
    return x + mix * _rms_scale(mix) * wpost_ref[...]


def _run_order(order, p_gen, c_gen):
    for ch in order:
        next(p_gen if ch == "P" else c_gen, None)
    for g in (p_gen, c_gen):
        for _ in g:
            pass


def _prompt_kernel(nt, sinks_ref, xp_ref, xc_ref, cos_ref, sdn_ref, sup_ref, wpre_ref, win_ref, wup_ref, bgk_ref,
                   gnw_ref, wout_ref, wpost_ref,
                   y_ref, kout_ref, vout_ref, sout_ref, *scratch):
    slots = (_make_slot(scratch[:N_SLOT_REFS]), _make_slot(scratch[N_SLOT_REFS:2 * N_SLOT_REFS]))
    s_ref, mix_ref = scratch[2 * N_SLOT_REFS:]
    s = pl.program_id(0)
    tt = xp_ref.shape[1]
    t_c = jnp.maximum(s - 1, 0) % nt

    @pl.when(s == 0)
    def _():
        for r in scratch[N_SLOT_REFS:]:
            r[...] = jnp.zeros_like(r)

    def body(slot_p, slot_c):
        def store_kv(k, v):
            kout_ref[0] = k[tt - WINDOW:]
            vout_ref[0] = v[tt - WINDOW:]
            for r, rc, val in zip(slot_p.kv, slot_c.kv, _kv_variants(k, v)):
                r[WINDOW:WINDOW + tt, :] = val
                r[0:WINDOW, :] = rc[tt:tt + WINDOW, :]

        p_gen = _project_stages(xp_ref[0], (cos_ref, sdn_ref, sup_ref), wpre_ref, win_ref, wup_ref, bgk_ref,
                                CHUNK, slot_p, store_kv)

        def c_stages():
            keep = t_c != 0
            state = [jnp.where(keep, s_ref[p * LANES:(p + 1) * LANES, :], 0.0) for p in range(GLA_HEADS // 2)]
            win = WINDOW + CHUNK
            chunk_args = []
            for r0 in range(0, tt, CHUNK):
                n_invalid = WINDOW - (t_c * tt + r0) if r0 < WINDOW else None
                chunk_args.append((r0, lambda i, r0=r0: slot_c.kv[i][r0:r0 + win, :], n_invalid))
            yield from _chunk_stages(slot_c, chunk_args, CHUNK, lambda i: state, sinks_ref, gnw_ref, mix_ref)
            y_ref[0] = _finish(xc_ref[0], mix_ref, wout_ref, wpost_ref)
            for p in range(GLA_HEADS // 2):
                s_ref[p * LANES:(p + 1) * LANES, :] = state[p]
                sout_ref[0, p * LANES:(p + 1) * LANES, :] = state[p]

        _run_order(PROMPT_ORDER, p_gen, c_stages())

    @pl.when(s % 2 == 0)
    def _():
        body(slots[0], slots[1])

    @pl.when(s % 2 == 1)
    def _():
        body(slots[1], slots[0])


def _sample_kernel(sinks_ref, x_ref, cos_ref, sdn_ref, sup_ref, ck_ref, cv_ref, s0_ref, wpre_ref, win_ref,
                   wup_ref, bgk_ref, gnw_ref, wout_ref, wpost_ref,
                   y_ref, kout_ref, vout_ref, sout_ref, *scratch):
    slot = _make_slot(scratch[:N_SLOT_REFS])
    mix_ref = scratch[N_SLOT_REFS]
    nb = ck_ref.shape[0]
    c = x_ref.shape[0] // nb
    win = WINDOW + c

    def store_kv(k, v):
        kout_ref[...] = k
        vout_ref[...] = v
        for i in range(nb):
            rows = slice(i * c, (i + 1) * c)
            kv = _kv_variants(jnp.concatenate([ck_ref[i], k[rows]], axis=0),
                              jnp.concatenate([cv_ref[i], v[rows]], axis=0))
            for r, val in zip(slot.kv, kv):
                r[i * win:(i + 1) * win, :] = val

    x = x_ref[...]
    for _ in _project_stages(x, (cos_ref, sdn_ref, sup_ref), wpre_ref, win_ref, wup_ref, bgk_ref, c, slot, store_kv):
        pass
    states = [[s0_ref[i, p * LANES:(p + 1) * LANES, :] for p in range(GLA_HEADS // 2)] for i in range(nb)]
    chunk_args = [(i * c, lambda j, i=i: slot.kv[j][i * win:(i + 1) * win, :], None) for i in range(nb)]
    for _ in _chunk_stages(slot, chunk_args, c, lambda i: states[i], sinks_ref, gnw_ref, mix_ref):
        pass
    for i in range(nb):
        for p in range(GLA_HEADS // 2):
            sout_ref[i, p * LANES:(p + 1) * LANES, :] = states[i][p]
    y_ref[...] = _finish(x, mix_ref, wout_ref, wpost_ref)


def _rotary_tables(pos):
    half = ROT_DIM // 2
    inv = ROPE_THETA ** (-jnp.arange(half, dtype=jnp.float32) * (2.0 / ROT_DIM))
    ang = pos.astype(jnp.float32)[:, None] * inv[None, :]
    cos, sin = jnp.cos(ang), jnp.sin(ang)
    d = jnp.arange(LANES) % HEAD_DIM
    cos_t = (d >= ROT_DIM).astype(jnp.float32)[None, :]
    sin_dn = sin_up = jnp.zeros((1, LANES), jnp.float32)
    for i in range(half):
        lo_i = (d == i).astype(jnp.float32)[None, :]
        hi_i = (d == half + i).astype(jnp.float32)[None, :]
        cos_t = cos_t + cos[:, i:i + 1] * (lo_i + hi_i)
        sin_dn = sin_dn + sin[:, i:i + 1] * hi_i
        sin_up = sin_up - sin[:, i:i + 1] * lo_i
    return cos_t, sin_dn, sin_up


def _cast_w_in_kernel(wt_ref, out_ref):
    row = lax.broadcasted_iota(jnp.int32, wt_ref.shape, 0) + pl.program_id(0) * CAST_BLOCK
    out_ref[...] = _bf(jnp.where(row < D_IN, wt_ref[...], 0.0).T)


def _cast_w_in(w_in_t):
    return pl.pallas_call(
        _cast_w_in_kernel,
        grid=(W_IN_ALLOC // CAST_BLOCK,),
        in_specs=[pl.BlockSpec((CAST_BLOCK, D_MODEL), lambda i: (i, 0))],
        out_specs=pl.BlockSpec((D_MODEL, CAST_BLOCK), lambda i: (0, i)),
        out_shape=jax.ShapeDtypeStruct((D_MODEL, W_IN_ALLOC), jnp.bfloat16),
        compiler_params=pltpu.CompilerParams(dimension_semantics=("arbitrary",)),
        name="cast_w_in",
    )(w_in_t)


def _const_spec(shape):
    zeros = (0,) * len(shape)
    return pl.BlockSpec(shape, lambda i: zeros)


def kernel(x_prompt, x_sample, cache_k, cache_v, state_gla, norm_pre_w, w_in, attn_sinks, w_gk_up, b_gk,
           gla_norm_w, w_out, norm_post_w):
    bsz, t_p, _ = x_prompt.shape
    dec_b, t_s, _ = x_sample.shape
    assert w_in.shape[0] == 1, "single layer"
    assert t_p % PROMPT_TILE == 0 and PROMPT_TILE % CHUNK == 0 and PROMPT_TILE >= WINDOW
    assert dec_b % SAMPLE_SEQS == 0 and t_s & (t_s - 1) == 0

    win = _cast_w_in(w_in[0].T)
    wup = _bf(jnp.pad(w_gk_up[0], ((0, LANES - GLA_RANK), (0, 0))))
    wout = _bf(w_out[0])
    wpre = norm_pre_w[0][None, :]
    wpost = norm_post_w[0][None, :]
    bgk = b_gk[0][None, :]
    gnw = gla_norm_w[0][None, :]
    sinks = attn_sinks[0][None, :]
    smem = pl.BlockSpec(memory_space=pltpu.SMEM)

    weights_specs = [
        _const_spec((1, D_MODEL)), _const_spec((D_MODEL, D_IN_PAD)), _const_spec((LANES, GQK_W)),
        _const_spec((1, GQK_W)), _const_spec((1, GLA_DV)), _const_spec((D_MODEL, D_MODEL)),
        _const_spec((1, D_MODEL))]

    tt = PROMPT_TILE
    nt = t_p // tt
    n_tiles = bsz * nt
    cos_p, sdn_p, sup_p = _rotary_tables(jnp.arange(t_p))

    def tile_p(s):
        return jnp.minimum(s, n_tiles - 1)

    def tile_c(s):
        return jnp.maximum(s - 1, 0)

    tab_spec = pl.BlockSpec((tt, LANES), lambda s: (tile_p(s) % nt, 0))
    x_spec = lambda tile: pl.BlockSpec((1, tt, D_MODEL), lambda s: (tile(s) // nt, tile(s) % nt, 0))
    seq_spec = lambda tile, d1, d2: pl.BlockSpec((1, d1, d2), lambda s: (tile(s) // nt, 0, 0))
    y_p, k_p, v_p, s_p = pl.pallas_call(
        functools.partial(_prompt_kernel, nt),
        grid=(n_tiles + 1,),
        in_specs=[smem, x_spec(tile_p), x_spec(tile_c), tab_spec, tab_spec, tab_spec] + weights_specs,
        out_specs=[x_spec(tile_c), seq_spec(tile_p, WINDOW, KV_W), seq_spec(tile_p, WINDOW, KV_W),
                   seq_spec(tile_c, S_ROWS, GLA_DV)],
        out_shape=[jax.ShapeDtypeStruct((bsz, t_p, D_MODEL), jnp.float32),
                   jax.ShapeDtypeStruct((bsz, WINDOW, KV_W), jnp.float32),
                   jax.ShapeDtypeStruct((bsz, WINDOW, KV_W), jnp.float32),
                   jax.ShapeDtypeStruct((bsz, S_ROWS, GLA_DV), jnp.float32)],
        scratch_shapes=_slot_shapes(tt, WINDOW + tt, tt // CHUNK) * 2
        + [pltpu.VMEM((S_ROWS, GLA_DV), jnp.float32), pltpu.VMEM((tt, D_MODEL), jnp.bfloat16)],
        compiler_params=pltpu.CompilerParams(dimension_semantics=("arbitrary",), vmem_limit_bytes=VMEM_LIMIT),
        name="prompt_layer",
    )(sinks, x_prompt, x_prompt, cos_p, sdn_p, sup_p, wpre, win, wup, bgk, gnw, wout, wpost)

    nb = SAMPLE_SEQS
    rows = nb * t_s
    tabs = [jnp.tile(a, (nb, 1)) for a in _rotary_tables(PAST_LEN + jnp.arange(t_s))]
    tab_spec = pl.BlockSpec((rows, LANES), lambda i: (0, 0))
    seq3 = lambda d1, d2: pl.BlockSpec((nb, d1, d2), lambda i: (i, 0, 0))
    tok = lambda d: pl.BlockSpec((rows, d), lambda i: (i, 0))
    y_s, k_s, v_s, s_s = pl.pallas_call(
        _sample_kernel,
        grid=(dec_b // nb,),
        in_specs=[smem, tok(D_MODEL), tab_spec, tab_spec, tab_spec, seq3(WINDOW, KV_W), seq3(WINDOW, KV_W),
                  seq3(S_ROWS, GLA_DV)] + weights_specs,
        out_specs=[tok(D_MODEL), tok(KV_W), tok(KV_W), seq3(S_ROWS, GLA_DV)],
        out_shape=[jax.ShapeDtypeStruct((dec_b * t_s, D_MODEL), jnp.float32),
                   jax.ShapeDtypeStruct((dec_b * t_s, KV_W), jnp.float32),
                   jax.ShapeDtypeStruct((dec_b * t_s, KV_W), jnp.float32),
                   jax.ShapeDtypeStruct((dec_b, S_ROWS, GLA_DV), jnp.float32)],
        scratch_shapes=_slot_shapes(rows, nb * (WINDOW + t_s), nb) + [pltpu.VMEM((rows, D_MODEL), jnp.bfloat16)],
        compiler_params=pltpu.CompilerParams(dimension_semantics=("arbitrary",), vmem_limit_bytes=VMEM_LIMIT),
        name="sample_layer",
    )(sinks, x_sample.reshape(dec_b * t_s, D_MODEL), *tabs,
      cache_k[0].reshape(dec_b, WINDOW, KV_W), cache_v[0].reshape(dec_b, WINDOW, KV_W),
      state_gla[0].reshape(dec_b, S_ROWS, GLA_DV), wpre, win, wup, bgk, gnw, wout, wpost)

    kv5 = lambda a, b, t: a.reshape(1, b, t, N_KV_HEADS, HEAD_DIM)
    st5 = lambda a, b: a.reshape(1, b, GLA_HEADS, GLA_DK, GLA_DV)
    return (y_p, y_s.reshape(dec_b, t_s, D_MODEL),
            kv5(k_p, bsz, WINDOW), kv5(v_p, bsz, WINDOW), st5(s_p, bsz),
            kv5(k_s, dec_b, t_s), kv5(v_s, dec_b, t_s), st5(s_s, dec_b))
```

```python
import collections
import functools

import jax
import jax.numpy as jnp
from jax import lax
from jax.experimental import pallas as pl
from jax.experimental.pallas import tpu as pltpu

D_MODEL = 1024
CHUNK = 64
WINDOW = 128
HEAD_DIM = 64
N_HEADS = 8
N_KV_HEADS = 2
GQA_GROUP = N_HEADS // N_KV_HEADS
ROT_DIM = 16
ROPE_THETA = 500000.0
GLA_HEADS = 4
GLA_DV = 128
GLA_DK = 64
GLA_RANK = 16
GATE_TAU = 16.0
NORM_EPS = 1e-6
PAST_LEN = 4096

LANES = 128
SUBLANES = 8
ATTN_W = N_HEADS * HEAD_DIM
KV_W = N_KV_HEADS * HEAD_DIM
GQK_W = GLA_HEADS * GLA_DK
GV_W = GLA_HEADS * GLA_DV
S_ROWS = GLA_HEADS * GLA_DK
OFF_AQ = 0
OFF_AK = OFF_AQ + ATTN_W
OFF_AV = OFF_AK + KV_W
OFF_AG = OFF_AV + KV_W
OFF_GQ = OFF_AG + ATTN_W
OFF_GK = OFF_GQ + GQK_W
OFF_GV = OFF_GK + GQK_W
OFF_GG = OFF_GV + GV_W
OFF_LR = OFF_GG + GV_W
D_IN = OFF_LR + GLA_RANK
D_IN_PAD = OFF_LR + LANES
CAST_BLOCK = 1024
W_IN_ALLOC = -(-D_IN_PAD // CAST_BLOCK) * CAST_BLOCK

PROMPT_TILE = 512
SAMPLE_SEQS = 8
VMEM_LIMIT = 56 * 1024 * 1024
PROMPT_ORDER = "CPCPCPCPCPCPCPCPC"
CHUNK_GROUP = 4

NEG = -1e30
LOG2E = 1.4426950408889634

Slot = collections.namedtuple("Slot", "qs kv ag qin kog kdec dec gv gg")


def _slot_shapes(rows, kv_rows, n_chunks):
    bf, f32 = jnp.bfloat16, jnp.float32
    return ([pltpu.VMEM((rows, 2 * ATTN_W), bf)] + [pltpu.VMEM((kv_rows, LANES), bf) for _ in range(6)]
            + [pltpu.VMEM((rows, ATTN_W), f32), pltpu.VMEM((rows, 2 * GQK_W), bf), pltpu.VMEM((rows, GQK_W), bf),
               pltpu.VMEM((rows, GQK_W), f32), pltpu.VMEM((n_chunks * SUBLANES, GQK_W), f32),
               pltpu.VMEM((rows, GV_W), bf), pltpu.VMEM((rows, GV_W), f32)])


N_SLOT_REFS = 14


def _make_slot(refs):
    return Slot(refs[0], tuple(refs[1:7]), *refs[7:14])


def _dot(a, b):
    return jnp.dot(a, b, preferred_element_type=jnp.float32)


def _dot_nt(a, b):
    return lax.dot_general(a, b, (((1,), (1,)), ((), ())), preferred_element_type=jnp.float32)


def _bf(x):
    return x.astype(jnp.bfloat16)


def _lane_masks():
    lane = lax.broadcasted_iota(jnp.int32, (1, LANES), 1)
    return lane < HEAD_DIM, lane >= HEAD_DIM


def _silu(x):
    return x * (1.0 / (1.0 + jnp.exp(-x)))


def _log_sigmoid(x):
    return jnp.minimum(x, 0.0) - jnp.log(1.0 + jnp.exp(-jnp.abs(x)))


def _rms_scale(x):
    return lax.rsqrt(jnp.mean(x * x, axis=-1, keepdims=True) + NORM_EPS)


def _chunk_cumsum(g, chunk):
    row = lax.broadcasted_iota(jnp.int32, g.shape, 0) & (chunk - 1)
    s = 1
    while s < chunk:
        g = g + jnp.where(row >= s, pltpu.roll(g, s, 0), 0.0)
        s *= 2
    return g


def _rotary(xcol, cos_t, sin_dn, sin_up):
    return (xcol * cos_t + pltpu.roll(xcol, ROT_DIM // 2, 1) * sin_dn
            + pltpu.roll(xcol, LANES - ROT_DIM // 2, 1) * sin_up)


def _kv_variants(k, v):
    m_lo, m_hi = _lane_masks()
    k_sw = pltpu.roll(k, HEAD_DIM, 1)
    v_sw = pltpu.roll(v, HEAD_DIM, 1)
    return (_bf(k), _bf(k_sw),
            _bf(jnp.where(m_lo, v, 0.0)), _bf(jnp.where(m_hi, v_sw, 0.0)),
            _bf(jnp.where(m_lo, v_sw, 0.0)), _bf(jnp.where(m_hi, v, 0.0)))


def _store_halves(ref, j, width, col):
    m_lo, m_hi = _lane_masks()
    ref[:, j * LANES:(j + 1) * LANES] = _bf(jnp.where(m_lo, col, 0.0))
    ref[:, width + j * LANES:width + (j + 1) * LANES] = _bf(jnp.where(m_hi, col, 0.0))


def _project_stages(x, tabs, wpre_ref, win_ref, wup_ref, bgk_ref, chunk, slot, store_kv):
    rows = x.shape[0]
    h = _bf(x * _rms_scale(x) * wpre_ref[...])
    cos_t, sin_dn, sin_up = (t[...] for t in tabs)

    def seg(off, width):
        return _dot(h, win_ref[:, off:off + width])

    yield
    q = seg(OFF_AQ, ATTN_W)
    for j in range(ATTN_W // LANES):
        ln = slice(j * LANES, (j + 1) * LANES)
        _store_halves(slot.qs, j, ATTN_W, _rotary(q[:, ln], cos_t, sin_dn, sin_up) * (LOG2E * HEAD_DIM ** -0.5))
    yield
    kv = seg(OFF_AK, 2 * KV_W)
    store_kv(_rotary(kv[:, :KV_W], cos_t, sin_dn, sin_up), kv[:, KV_W:])
    yield
    slot.ag[...] = _silu(seg(OFF_AG, ATTN_W))
    yield
    lr = _bf(seg(OFF_LR, LANES))
    log_a = _log_sigmoid(_dot(lr, wup_ref[...]) + bgk_ref[...]) * (1.0 / GATE_TAU)
    b = _chunk_cumsum(log_a, chunk)
    lasts = [b[r + chunk - 1:r + chunk] for r in range(0, rows, chunk)]
    for i, bl in enumerate(lasts):
        slot.dec[i * SUBLANES:(i + 1) * SUBLANES, :] = jnp.broadcast_to(jnp.exp(bl), (SUBLANES, GQK_W))
    b_last = jnp.concatenate([jnp.broadcast_to(bl, (chunk, GQK_W)) for bl in lasts], axis=0)
    q_in = seg(OFF_GQ, GQK_W) * (GLA_DK ** -0.5) * jnp.exp(b)
    for j in range(GQK_W // LANES):
        _store_halves(slot.qin, j, GQK_W, q_in[:, j * LANES:(j + 1) * LANES])
    yield
    gk = seg(OFF_GK, GQK_W)
    slot.kog[...] = _bf(gk * jnp.exp(-b))
    slot.kdec[...] = gk * jnp.exp(b_last - b)
    yield
    slot.gv[...] = _bf(seg(OFF_GV, GV_W))
    yield
    slot.gg[...] = _silu(seg(OFF_GG, GV_W))


def _attend_stages(slot, rows, kv_win, sinks_ref, n_invalid, emit):
    c = rows.stop - rows.start
    lo = [slot.qs[rows, j * LANES:(j + 1) * LANES] for j in range(ATTN_W // LANES)]
    hi = [slot.qs[rows, ATTN_W + j * LANES:ATTN_W + (j + 1) * LANES] for j in range(ATTN_W // LANES)]
    k_a, k_b = kv_win(0), kv_win(1)
    n_hi = k_a.shape[0] - LANES
    zrows = jnp.zeros((LANES - n_hi, LANES), jnp.bfloat16)
    ext = lambda a: jnp.concatenate([a, zrows], axis=0)
    s_a = _dot_nt(jnp.concatenate([lo[0], lo[1], hi[2], hi[3]], axis=0), ext(k_a))
    s_b = _dot_nt(jnp.concatenate([hi[0], hi[1], lo[2], lo[3]], axis=0), ext(k_b))
    yield
    lane = lax.broadcasted_iota(jnp.int32, (1, LANES), 1)
    where = {0: (s_a, 0), 2: (s_a, 1), 5: (s_a, 2), 7: (s_a, 3),
             1: (s_b, 0), 3: (s_b, 1), 4: (s_b, 2), 6: (s_b, 3)}
    outs = []
    for h in range(N_HEADS):
        src, blk = where[h]
        s = src[blk * c:(blk + 1) * c]
        pad_row = jnp.where(lane == n_hi, sinks_ref[0, h] * LOG2E, NEG)
        s_lo = s[:, :LANES]
        s_hi = jnp.where(lane >= n_hi, pad_row, s[:, LANES:])
        if n_invalid is not None:
            s_lo = jnp.where(lane < n_invalid, NEG, s_lo)
        m = jnp.max(jnp.maximum(s_lo, s_hi), axis=-1, keepdims=True)
        p_lo = jnp.exp2(s_lo - m)
        p_hi = jnp.exp2(s_hi - m)
        denom = jnp.sum(p_lo + p_hi, axis=-1, keepdims=True)
        o = _dot(_bf(jnp.concatenate([p_lo, p_hi], axis=1)), ext(kv_win(2 + 2 * (h // GQA_GROUP) + h % 2)))
        outs.append(o * (1.0 / denom))
    yield
    emit(jnp.concatenate([outs[2 * j] + outs[2 * j + 1] for j in range(4)], axis=1))


def _gla_stages(slot, r0, c, state, gnw_ref, mix_ref):
    rows = slice(r0, r0 + c)
    ci0 = (r0 // c) * SUBLANES
    pad = LANES - c
    lhs, a2, uw, dcol, vw = [], [], [], [], []
    for p in range(GLA_HEADS // 2):
        ln = slice(p * LANES, (p + 1) * LANES)
        vwide = slot.gv[rows, p * 2 * GLA_DV:(p + 1) * 2 * GLA_DV]
        lhs.append(jnp.concatenate([slot.qin[rows, ln], slot.qin[rows, GQK_W + p * LANES:GQK_W + (p + 1) * LANES]],
                                   axis=0))
        a2.append(_dot_nt(lhs[p], slot.kog[rows, ln]))
        kd_t = jnp.concatenate([slot.kdec[rows, ln], jnp.zeros((pad, LANES), jnp.float32)], axis=0).T
        v_pad = jnp.concatenate([vwide, jnp.zeros((pad, 2 * GLA_DV), jnp.bfloat16)], axis=0)
        uw.append(_dot(_bf(kd_t), v_pad))
        dcol.append(jnp.broadcast_to(slot.dec[ci0:ci0 + 1, ln], (LANES, LANES)).T)
        vw.append(vwide)
    yield
    outs = []
    ri = lax.broadcasted_iota(jnp.int32, a2[0].shape, 0) & (c - 1)
    ci = lax.broadcasted_iota(jnp.int32, a2[0].shape, 1)
    for p in range(GLA_HEADS // 2):
        am = _bf(jnp.where(ri >= ci, a2[p], 0.0))
        sb = _bf(state[p])
        for i in range(2):
            hr = slice(i * c, (i + 1) * c)
            outs.append(_dot(jnp.concatenate([lhs[p][hr], am[hr]], axis=1),
                             jnp.concatenate([sb, vw[p][:, i * GLA_DV:(i + 1) * GLA_DV]], axis=0)))
        u_pair = jnp.concatenate([uw[p][:GLA_DK, :GLA_DV], uw[p][GLA_DK:, GLA_DV:]], axis=0)
        state[p] = dcol[p] * state[p] + u_pair
    yield
    gnw = gnw_ref[...]
    for hh, o in enumerate(outs):
        y = o * _rms_scale(o) * gnw * slot.gg[rows, hh * GLA_DV:(hh + 1) * GLA_DV]
        mix_ref[rows, ATTN_W + hh * GLA_DV:ATTN_W + (hh + 1) * GLA_DV] = _bf(y)


def _lockstep(gens):
    live = list(gens)
    while live:
        for g in list(live):
            try:
                next(g)
            except StopIteration:
                live.remove(g)
        yield


def _chunk_stages(slot, chunk_args, c, state_of, sinks_ref, gnw_ref, mix_ref):
    gens = []
    for i, (r0, kv_win, n_invalid) in enumerate(chunk_args):
        rows = slice(r0, r0 + c)

        def emit(attn, rows=rows):
            mix_ref[rows, 0:ATTN_W] = _bf(attn * slot.ag[rows, :])

        gens.append(_attend_stages(slot, rows, kv_win, sinks_ref, n_invalid, emit))
        gens.append(_gla_stages(slot, r0, c, state_of(i), gnw_ref, mix_ref))
    for g in range(0, len(gens), 2 * CHUNK_GROUP):
        yield from _lockstep(gens[g:g + 2 * CHUNK_GROUP])


def _finish(x, mix_ref, wout_ref, wpost_ref):
    mix = _dot(mix_ref[...], wout_ref[...])
    return x + mix * _rms_scale(mix) * wpost_ref[...]


def _run_order(order, p_gen, c_gen):
    for ch in order:
        next(p_gen if ch == "P" else c_gen, None)
    for g in (p_gen, c_gen):
        for _ in g:
            pass


def _prompt_kernel(nt, sinks_ref, xp_ref, xc_ref, cos_ref, sdn_ref, sup_ref, wpre_ref, win_ref, wup_ref, bgk_ref,
                   gnw_ref, wout_ref, wpost_ref,
                   y_ref, kout_ref, vout_ref, sout_ref, *scratch):
    slots = (_make_slot(scratch[:N_SLOT_REFS]), _make_slot(scratch[N_SLOT_REFS:2 * N_SLOT_REFS]))
    s_ref, mix_ref = scratch[2 * N_SLOT_REFS:]
    s = pl.program_id(0)
    tt = xp_ref.shape[1]
    t_c = jnp.maximum(s - 1, 0) % nt

    @pl.when(s == 0)
    def _():
        for r in scratch[N_SLOT_REFS:]:
            r[...] = jnp.zeros_like(r)

    def body(slot_p, slot_c):
        def store_kv(k, v):
            kout_ref[0] = k[tt - WINDOW:]
            vout_ref[0] = v[tt - WINDOW:]
            for r, rc, val in zip(slot_p.kv, slot_c.kv, _kv_variants(k, v)):
                r[WINDOW:WINDOW + tt, :] = val
                r[0:WINDOW, :] = rc[tt:tt + WINDOW, :]

        p_gen = _project_stages(xp_ref[0], (cos_ref, sdn_ref, sup_ref), wpre_ref, win_ref, wup_ref, bgk_ref,
                                CHUNK, slot_p, store_kv)

        def c_stages():
            keep = t_c != 0
            state = [jnp.where(keep, s_ref[p * LANES:(p + 1) * LANES, :], 0.0) for p in range(GLA_HEADS // 2)]
            win = WINDOW + CHUNK
            chunk_args = []
            for r0 in range(0, tt, CHUNK):
                n_invalid = WINDOW - (t_c * tt + r0) if r0 < WINDOW else None
                chunk_args.append((r0, lambda i, r0=r0: slot_c.kv[i][r0:r0 + win, :], n_invalid))
            yield from _chunk_stages(slot_c, chunk_args, CHUNK, lambda i: state, sinks_ref, gnw_ref, mix_ref)
            y_ref[0] = _finish(xc_ref[0], mix_ref, wout_ref, wpost_ref)
            for p in range(GLA_HEADS // 2):
                s_ref[p * LANES:(p + 1) * LANES, :] = state[p]
                sout_ref[0, p * LANES:(p + 1) * LANES, :] = state[p]

        _run_order(PROMPT_ORDER, p_gen, c_stages())

    @pl.when(s % 2 == 0)
    def _():
        body(slots[0], slots[1])

    @pl.when(s % 2 == 1)
    def _():
        body(slots[1], slots[0])


def _sample_kernel(sinks_ref, x_ref, cos_ref, sdn_ref, sup_ref, ck_ref, cv_ref, s0_ref, wpre_ref, win_ref,
                   wup_ref, bgk_ref, gnw_ref, wout_ref, wpost_ref,
                   y_ref, kout_ref, vout_ref, sout_ref, *scratch):
    slot = _make_slot(scratch[:N_SLOT_REFS])
    mix_ref = scratch[N_SLOT_REFS]
    nb = ck_ref.shape[0]
    c = x_ref.shape[0] // nb
    win = WINDOW + c

    def store_kv(k, v):
        kout_ref[...] = k
        vout_ref[...] = v
        for i in range(nb):
            rows = slice(i * c, (i + 1) * c)
            kv = _kv_variants(jnp.concatenate([ck_ref[i], k[rows]], axis=0),
                              jnp.concatenate([cv_ref[i], v[rows]], axis=0))
            for r, val in zip(slot.kv, kv):
                r[i * win:(i + 1) * win, :] = val

    x = x_ref[...]
    for _ in _project_stages(x, (cos_ref, sdn_ref, sup_ref), wpre_ref, win_ref, wup_ref, bgk_ref, c, slot, store_kv):
        pass
    states = [[s0_ref[i, p * LANES:(p + 1) * LANES, :] for p in range(GLA_HEADS // 2)] for i in range(nb)]
    chunk_args = [(i * c, lambda j, i=i: slot.kv[j][i * win:(i + 1) * win, :], None) for i in range(nb)]
    for _ in _chunk_stages(slot, chunk_args, c, lambda i: states[i], sinks_ref, gnw_ref, mix_ref):
        pass
    for i in range(nb):
        for p in range(GLA_HEADS // 2):
            sout_ref[i, p * LANES:(p + 1) * LANES, :] = states[i][p]
    y_ref[...] = _finish(x, mix_ref, wout_ref, wpost_ref)


def _rotary_tables(pos):
    half = ROT_DIM // 2
    inv = ROPE_THETA ** (-jnp.arange(half, dtype=jnp.float32) * (2.0 / ROT_DIM))
    ang = pos.astype(jnp.float32)[:, None] * inv[None, :]
    cos, sin = jnp.cos(ang), jnp.sin(ang)
    d = jnp.arange(LANES) % HEAD_DIM
    cos_t = (d >= ROT_DIM).astype(jnp.float32)[None, :]
    sin_dn = sin_up = jnp.zeros((1, LANES), jnp.float32)
    for i in range(half):
        lo_i = (d == i).astype(jnp.float32)[None, :]
        hi_i = (d == half + i).astype(jnp.float32)[None, :]
        cos_t = cos_t + cos[:, i:i + 1] * (lo_i + hi_i)
        sin_dn = sin_dn + sin[:, i:i + 1] * hi_i
        sin_up = sin_up - sin[:, i:i + 1] * lo_i
    return cos_t, sin_dn, sin_up


def _cast_w_in_kernel(wt_ref, out_ref):
    row = lax.broadcasted_iota(jnp.int32, wt_ref.shape, 0) + pl.program_id(0) * CAST_BLOCK
    out_ref[...] = _bf(jnp.where(row < D_IN, wt_ref[...], 0.0).T)


def _cast_w_in(w_in_t):
    return pl.pallas_call(
        _cast_w_in_kernel,
        grid=(W_IN_ALLOC // CAST_BLOCK,),
        in_specs=[pl.BlockSpec((CAST_BLOCK, D_MODEL), lambda i: (i, 0))],
        out_specs=pl.BlockSpec((D_MODEL, CAST_BLOCK), lambda i: (0, i)),
        out_shape=jax.ShapeDtypeStruct((D_MODEL, W_IN_ALLOC), jnp.bfloat16),
        compiler_params=pltpu.CompilerParams(dimension_semantics=("arbitrary",)),
        name="cast_w_in",
    )(w_in_t)


def _const_spec(shape):
    zeros = (0,) * len(shape)
    return pl.BlockSpec(shape, lambda i: zeros, pipeline_mode=pl.Buffered(1))


def kernel(x_prompt, x_sample, cache_k, cache_v, state_gla, norm_pre_w, w_in, attn_sinks, w_gk_up, b_gk,
           gla_norm_w, w_out, norm_post_w):
    bsz, t_p, _ = x_prompt.shape
    dec_b, t_s, _ = x_sample.shape
    assert w_in.shape[0] == 1, "single layer"
    assert t_p % PROMPT_TILE == 0 and PROMPT_TILE % CHUNK == 0 and PROMPT_TILE >= WINDOW
    assert dec_b % SAMPLE_SEQS == 0 and t_s & (t_s - 1) == 0

    win = _cast_w_in(w_in[0].T)
    wup = _bf(jnp.pad(w_gk_up[0], ((0, LANES - GLA_RANK), (0, 0))))
    wout = _bf(w_out[0])
    wpre = norm_pre_w[0][None, :]
    wpost = norm_post_w[0][None, :]
    bgk = b_gk[0][None, :]
    gnw = gla_norm_w[0][None, :]
    sinks = attn_sinks[0][None, :]
    smem = pl.BlockSpec(memory_space=pltpu.SMEM)

    weights_specs = [
        _const_spec((1, D_MODEL)), _const_spec((D_MODEL, D_IN_PAD)), _const_spec((LANES, GQK_W)),
        _const_spec((1, GQK_W)), _const_spec((1, GLA_DV)), _const_spec((D_MODEL, D_MODEL)),
        _const_spec((1, D_MODEL))]

    tt = PROMPT_TILE
    nt = t_p // tt
    n_tiles = bsz * nt
    cos_p, sdn_p, sup_p = _rotary_tables(jnp.arange(t_p))

    def tile_p(s):
        return jnp.minimum(s, n_tiles - 1)

    def tile_c(s):
        return jnp.maximum(s - 1, 0)

    tab_spec = pl.BlockSpec((tt, LANES), lambda s: (tile_p(s) % nt, 0))
    x_spec = lambda tile: pl.BlockSpec((1, tt, D_MODEL), lambda s: (tile(s) // nt, tile(s) % nt, 0))
    seq_spec = lambda tile, d1, d2: pl.BlockSpec((1, d1, d2), lambda s: (tile(s) // nt, 0, 0))
    y_p, k_p, v_p, s_p = pl.pallas_call(
        functools.partial(_prompt_kernel, nt),
        grid=(n_tiles + 1,),
        in_specs=[smem, x_spec(tile_p), x_spec(tile_c), tab_spec, tab_spec, tab_spec] + weights_specs,
        out_specs=[x_spec(tile_c), seq_spec(tile_p, WINDOW, KV_W), seq_spec(tile_p, WINDOW, KV_W),
                   seq_spec(tile_c, S_ROWS, GLA_DV)],
        out_shape=[jax.ShapeDtypeStruct((bsz, t_p, D_MODEL), jnp.float32),
                   jax.ShapeDtypeStruct((bsz, WINDOW, KV_W), jnp.float32),
                   jax.ShapeDtypeStruct((bsz, WINDOW, KV_W), jnp.float32),
                   jax.ShapeDtypeStruct((bsz, S_ROWS, GLA_DV), jnp.float32)],
        scratch_shapes=_slot_shapes(tt, WINDOW + tt, tt // CHUNK) * 2
        + [pltpu.VMEM((S_ROWS, GLA_DV), jnp.float32), pltpu.VMEM((tt, D_MODEL), jnp.bfloat16)],
        compiler_params=pltpu.CompilerParams(dimension_semantics=("arbitrary",), vmem_limit_bytes=VMEM_LIMIT),
        name="prompt_layer",
    )(sinks, x_prompt, x_prompt, cos_p, sdn_p, sup_p, wpre, win, wup, bgk, gnw, wout, wpost)

    nb = SAMPLE_SEQS
    rows = nb * t_s
    tabs = [jnp.tile(a, (nb, 1)) for a in _rotary_tables(PAST_LEN + jnp.arange(t_s))]
    tab_spec = pl.BlockSpec((rows, LANES), lambda i: (0, 0))
    seq3 = lambda d1, d2: pl.BlockSpec((nb, d1, d2), lambda i: (i, 0, 0))
    tok = lambda d: pl.BlockSpec((rows, d), lambda i: (i, 0))
    y_s, k_s, v_s, s_s = pl.pallas_call(
        _sample_kernel,
        grid=(dec_b // nb,),
        in_specs=[smem, tok(D_MODEL), tab_spec, tab_spec, tab_spec, seq3(WINDOW, KV_W), seq3(WINDOW, KV_W),
                  seq3(S_ROWS, GLA_DV)] + weights_specs,
        out_specs=[tok(D_MODEL), tok(KV_W), tok(KV_W), seq3(S_ROWS, GLA_DV)],
        out_shape=[jax.ShapeDtypeStruct((dec_b * t_s, D_MODEL), jnp.float32),
                   jax.ShapeDtypeStruct((dec_b * t_s, KV_W), jnp.float32),
                   jax.ShapeDtypeStruct((dec_b * t_s, KV_W), jnp.float32),
                   jax.ShapeDtypeStruct((dec_b, S_ROWS, GLA_DV), jnp.float32)],
        scratch_shapes=_slot_shapes(rows, nb * (WINDOW + t_s), nb) + [pltpu.VMEM((rows, D_MODEL), jnp.bfloat16)],
        compiler_params=pltpu.CompilerParams(dimension_semantics=("arbitrary",), vmem_limit_bytes=VMEM_LIMIT),
        name="sample_layer",
    )(sinks, x_sample.reshape(dec_b * t_s, D_MODEL), *tabs,
      cache_k[0].reshape(dec_b, WINDOW, KV_W), cache_v[0].reshape(dec_b, WINDOW, KV_W),
      state_gla[0].reshape(dec_b, S_ROWS, GLA_DV), wpre, win, wup, bgk, gnw, wout, wpost)

    kv5 = lambda a, b, t: a.reshape(1, b, t, N_KV_HEADS, HEAD_DIM)
    st5 = lambda a, b: a.reshape(1, b, GLA_HEADS, GLA_DK, GLA_DV)
    return (y_p, y_s.reshape(dec_b, t_s, D_MODEL),
            kv5(k_p, bsz, WINDOW), kv5(v_p, bsz, WINDOW), st5(s_p, bsz),
            kv5(k_s, dec_b, t_s), kv5(v_s, dec_b, t_s), st5(s_s, dec_b))
```

```python
import collections
import functools

import jax
import jax.numpy as jnp
from jax import lax
from jax.experimental import pallas as pl
from jax.experimental.pallas import tpu as pltpu

D_MODEL = 1024
CHUNK = 64
WINDOW = 128
HEAD_DIM = 64
N_HEADS = 8
N_KV_HEADS = 2
GQA_GROUP = N_HEADS // N_KV_HEADS
ROT_DIM = 16
ROPE_THETA = 500000.0
GLA_HEADS = 4
GLA_DV = 128
GLA_DK = 64
GLA_RANK = 16
GATE_TAU = 16.0
NORM_EPS = 1e-6
PAST_LEN = 4096

LANES = 128
SUBLANES = 8
ATTN_W = N_HEADS * HEAD_DIM
KV_W = N_KV_HEADS * HEAD_DIM
GQK_W = GLA_HEADS * GLA_DK
GV_W = GLA_HEADS * GLA_DV
S_ROWS = GLA_HEADS * GLA_DK
OFF_AQ = 0
OFF_AK = OFF_AQ + ATTN_W
OFF_AV = OFF_AK + KV_W
OFF_AG = OFF_AV + KV_W
OFF_GQ = OFF_AG + ATTN_W
OFF_GK = OFF_GQ + GQK_W
OFF_GV = OFF_GK + GQK_W
OFF_GG = OFF_GV + GV_W
OFF_LR = OFF_GG + GV_W
D_IN = OFF_LR + GLA_RANK
D_IN_PAD = OFF_LR + LANES
CAST_BLOCK = 1024
W_IN_ALLOC = -(-D_IN_PAD // CAST_BLOCK) * CAST_BLOCK

PROMPT_TILE = 256
SAMPLE_SEQS = 8
VMEM_LIMIT = 48 * 1024 * 1024
PROMPT_ORDER = "CPFPCPPPCPPP"
CHUNK_GROUP = 4

NEG = -1e30
LOG2E = 1.4426950408889634

Slot = collections.namedtuple("Slot", "qs kv ag qin kog kdec dec gv gg mix")


def _slot_shapes(rows, kv_rows, n_chunks):
    bf, f32 = jnp.bfloat16, jnp.float32
    return ([pltpu.VMEM((rows, 2 * ATTN_W), bf)] + [pltpu.VMEM((kv_rows, LANES), bf) for _ in range(6)]
            + [pltpu.VMEM((rows, ATTN_W), f32), pltpu.VMEM((rows, 2 * GQK_W), bf), pltpu.VMEM((rows, GQK_W), bf),
               pltpu.VMEM((rows, GQK_W), f32), pltpu.VMEM((n_chunks * SUBLANES, GQK_W), f32),
               pltpu.VMEM((rows, GV_W), bf), pltpu.VMEM((rows, GV_W), f32), pltpu.VMEM((rows, D_MODEL), bf)])


N_SLOT_REFS = 15


def _make_slot(refs):
    return Slot(refs[0], tuple(refs[1:7]), *refs[7:15])


def _dot(a, b):
    return jnp.dot(a, b, preferred_element_type=jnp.float32)


def _dot_nt(a, b):
    return lax.dot_general(a, b, (((1,), (1,)), ((), ())), preferred_element_type=jnp.float32)


def _bf(x):
    return x.astype(jnp.bfloat16)


def _lane_masks():
    lane = lax.broadcasted_iota(jnp.int32, (1, LANES), 1)
    return lane < HEAD_DIM, lane >= HEAD_DIM


def _silu(x):
    return x * (1.0 / (1.0 + jnp.exp(-x)))


def _log_sigmoid(x):
    return jnp.minimum(x, 0.0) - jnp.log(1.0 + jnp.exp(-jnp.abs(x)))


def _rms_scale(x):
    return lax.rsqrt(jnp.mean(x * x, axis=-1, keepdims=True) + NORM_EPS)


def _chunk_cumsum(g, chunk):
    row = lax.broadcasted_iota(jnp.int32, g.shape, 0) & (chunk - 1)
    s = 1
    while s < chunk:
        g = g + jnp.where(row >= s, pltpu.roll(g, s, 0), 0.0)
        s *= 2
    return g


def _rotary(xcol, cos_t, sin_dn, sin_up):
    return (xcol * cos_t + pltpu.roll(xcol, ROT_DIM // 2, 1) * sin_dn
            + pltpu.roll(xcol, LANES - ROT_DIM // 2, 1) * sin_up)


def _kv_variants(k, v):
    m_lo, m_hi = _lane_masks()
    k_sw = pltpu.roll(k, HEAD_DIM, 1)
    v_sw = pltpu.roll(v, HEAD_DIM, 1)
    return (_bf(k), _bf(k_sw),
            _bf(jnp.where(m_lo, v, 0.0)), _bf(jnp.where(m_hi, v_sw, 0.0)),
            _bf(jnp.where(m_lo, v_sw, 0.0)), _bf(jnp.where(m_hi, v, 0.0)))


def _store_halves(ref, j, width, col):
    m_lo, m_hi = _lane_masks()
    ref[:, j * LANES:(j + 1) * LANES] = _bf(jnp.where(m_lo, col, 0.0))
    ref[:, width + j * LANES:width + (j + 1) * LANES] = _bf(jnp.where(m_hi, col, 0.0))


def _project_stages(x, tabs, wpre_ref, win_ref, wup_ref, bgk_ref, chunk, slot, store_kv):
    rows = x.shape[0]
    h = _bf(x * _rms_scale(x) * wpre_ref[...])
    cos_t, sin_dn, sin_up = (t[...] for t in tabs)

    def seg(off, width):
        return _dot(h, win_ref[:, off:off + width])

    yield
    q = seg(OFF_AQ, ATTN_W)
    for j in range(ATTN_W // LANES):
        ln = slice(j * LANES, (j + 1) * LANES)
        _store_halves(slot.qs, j, ATTN_W, _rotary(q[:, ln], cos_t, sin_dn, sin_up) * (LOG2E * HEAD_DIM ** -0.5))
    yield
    kv = seg(OFF_AK, 2 * KV_W)
    store_kv(_rotary(kv[:, :KV_W], cos_t, sin_dn, sin_up), kv[:, KV_W:])
    yield
    slot.ag[...] = _silu(seg(OFF_AG, ATTN_W))
    yield
    lr = _bf(seg(OFF_LR, LANES))
    log_a = _log_sigmoid(_dot(lr, wup_ref[...]) + bgk_ref[...]) * (1.0 / GATE_TAU)
    b = _chunk_cumsum(log_a, chunk)
    lasts = [b[r + chunk - 1:r + chunk] for r in range(0, rows, chunk)]
    for i, bl in enumerate(lasts):
        slot.dec[i * SUBLANES:(i + 1) * SUBLANES, :] = jnp.broadcast_to(jnp.exp(bl), (SUBLANES, GQK_W))
    b_last = jnp.concatenate([jnp.broadcast_to(bl, (chunk, GQK_W)) for bl in lasts], axis=0)
    q_in = seg(OFF_GQ, GQK_W) * (GLA_DK ** -0.5) * jnp.exp(b)
    for j in range(GQK_W // LANES):
        _store_halves(slot.qin, j, GQK_W, q_in[:, j * LANES:(j + 1) * LANES])
    yield
    gk = seg(OFF_GK, GQK_W)
    slot.kog[...] = _bf(gk * jnp.exp(-b))
    slot.kdec[...] = gk * jnp.exp(b_last - b)
    yield
    slot.gv[...] = _bf(seg(OFF_GV, GV_W))
    yield
    slot.gg[...] = _silu(seg(OFF_GG, GV_W))


def _attend_stages(slot, rows, kv_win, sinks_ref, n_invalid, emit):
    c = rows.stop - rows.start
    lo = [slot.qs[rows, j * LANES:(j + 1) * LANES] for j in range(ATTN_W // LANES)]
    hi = [slot.qs[rows, ATTN_W + j * LANES:ATTN_W + (j + 1) * LANES] for j in range(ATTN_W // LANES)]
    k_a, k_b = kv_win(0), kv_win(1)
    n_hi = k_a.shape[0] - LANES
    zrows = jnp.zeros((LANES - n_hi, LANES), jnp.bfloat16)
    ext = lambda a: jnp.concatenate([a, zrows], axis=0)
    s_a = _dot_nt(jnp.concatenate([lo[0], lo[1], hi[2], hi[3]], axis=0), ext(k_a))
    s_b = _dot_nt(jnp.concatenate([hi[0], hi[1], lo[2], lo[3]], axis=0), ext(k_b))
    yield
    lane = lax.broadcasted_iota(jnp.int32, (1, LANES), 1)
    where = {0: (s_a, 0), 2: (s_a, 1), 5: (s_a, 2), 7: (s_a, 3),
             1: (s_b, 0), 3: (s_b, 1), 4: (s_b, 2), 6: (s_b, 3)}
    outs = []
    for h in range(N_HEADS):
        src, blk = where[h]
        s = src[blk * c:(blk + 1) * c]
        pad_row = jnp.where(lane == n_hi, sinks_ref[0, h] * LOG2E, NEG)
        s_lo = s[:, :LANES]
        s_hi = jnp.where(lane >= n_hi, pad_row, s[:, LANES:])
        if n_invalid is not None:
            s_lo = jnp.where(lane < n_invalid, NEG, s_lo)
        m = jnp.max(jnp.maximum(s_lo, s_hi), axis=-1, keepdims=True)
        p_lo = jnp.exp2(s_lo - m)
        p_hi = jnp.exp2(s_hi - m)
        denom = jnp.sum(p_lo + p_hi, axis=-1, keepdims=True)
        o = _dot(_bf(jnp.concatenate([p_lo, p_hi], axis=1)), ext(kv_win(2 + 2 * (h // GQA_GROUP) + h % 2)))
        outs.append(o * (1.0 / denom))
    yield
    emit(jnp.concatenate([outs[2 * j] + outs[2 * j + 1] for j in range(4)], axis=1))


def _gla_stages(slot, r0, c, state, gnw_ref):
    rows = slice(r0, r0 + c)
    ci0 = (r0 // c) * SUBLANES
    pad = LANES - c
    lhs, a2, uw, dcol, vw = [], [], [], [], []
    for p in range(GLA_HEADS // 2):
        ln = slice(p * LANES, (p + 1) * LANES)
        vwide = slot.gv[rows, p * 2 * GLA_DV:(p + 1) * 2 * GLA_DV]
        lhs.append(jnp.concatenate([slot.qin[rows, ln], slot.qin[rows, GQK_W + p * LANES:GQK_W + (p + 1) * LANES]],
                                   axis=0))
        a2.append(_dot_nt(lhs[p], slot.kog[rows, ln]))
        kd_t = jnp.concatenate([slot.kdec[rows, ln], jnp.zeros((pad, LANES), jnp.float32)], axis=0).T
        v_pad = jnp.concatenate([vwide, jnp.zeros((pad, 2 * GLA_DV), jnp.bfloat16)], axis=0)
        uw.append(_dot(_bf(kd_t), v_pad))
        dcol.append(jnp.broadcast_to(slot.dec[ci0:ci0 + 1, ln], (LANES, LANES)).T)
        vw.append(vwide)
    yield
    outs = []
    ri = lax.broadcasted_iota(jnp.int32, a2[0].shape, 0) & (c - 1)
    ci = lax.broadcasted_iota(jnp.int32, a2[0].shape, 1)
    for p in range(GLA_HEADS // 2):
        am = _bf(jnp.where(ri >= ci, a2[p], 0.0))
        sb = _bf(state[p])
        for i in range(2):
            hr = slice(i * c, (i + 1) * c)
            outs.append(_dot(jnp.concatenate([lhs[p][hr], am[hr]], axis=1),
                             jnp.concatenate([sb, vw[p][:, i * GLA_DV:(i + 1) * GLA_DV]], axis=0)))
        u_pair = jnp.concatenate([uw[p][:GLA_DK, :GLA_DV], uw[p][GLA_DK:, GLA_DV:]], axis=0)
        state[p] = dcol[p] * state[p] + u_pair
    yield
    gnw = gnw_ref[...]
    for hh, o in enumerate(outs):
        y = o * _rms_scale(o) * gnw * slot.gg[rows, hh * GLA_DV:(hh + 1) * GLA_DV]
        slot.mix[rows, ATTN_W + hh * GLA_DV:ATTN_W + (hh + 1) * GLA_DV] = _bf(y)


def _lockstep(gens):
    live = list(gens)
    while live:
        for g in list(live):
            try:
                next(g)
            except StopIteration:
                live.remove(g)
        yield


def _chunk_stages(slot, chunk_args, c, state_of, sinks_ref, gnw_ref):
    gens = []
    for i, (r0, kv_win, n_invalid) in enumerate(chunk_args):
        rows = slice(r0, r0 + c)

        def emit(attn, rows=rows):
            slot.mix[rows, 0:ATTN_W] = _bf(attn * slot.ag[rows, :])

        gens.append(_attend_stages(slot, rows, kv_win, sinks_ref, n_invalid, emit))
        gens.append(_gla_stages(slot, r0, c, state_of(i), gnw_ref))
    for g in range(0, len(gens), 2 * CHUNK_GROUP):
        yield from _lockstep(gens[g:g + 2 * CHUNK_GROUP])


def _finish(x, mix_ref, wout_ref, wpost_ref):
    mix = _dot(mix_ref[...], wout_ref[...])
    return x + mix * _rms_scale(mix) * wpost_ref[...]


def _run_order(order, gens):
    for ch in order:
        next(gens[ch], None)
    for g in gens.values():
        for _ in g:
            pass


def _prompt_kernel(nt, n_tiles, sinks_ref, xp_ref, xf_ref, cos_ref, sdn_ref, sup_ref, wpre_ref, win_ref, wup_ref,
                   bgk_ref, gnw_ref, wout_ref, wpost_ref,
                   y_ref, kout_ref, vout_ref, sout_ref, *scratch):
    slots = (_make_slot(scratch[:N_SLOT_REFS]), _make_slot(scratch[N_SLOT_REFS:2 * N_SLOT_REFS]))
    s_ref = scratch[2 * N_SLOT_REFS]
    s = pl.program_id(0)
    tt = xp_ref.shape[1]
    t_c = jnp.maximum(s - 1, 0) % nt
    c_valid = s <= n_tiles

    @pl.when(s == 0)
    def _():
        for r in scratch:
            r[...] = jnp.zeros_like(r)

    def body(slot_p, slot_c):
        def store_kv(k, v):
            kout_ref[0] = k[tt - WINDOW:]
            vout_ref[0] = v[tt - WINDOW:]
            for r, rc, val in zip(slot_p.kv, slot_c.kv, _kv_variants(k, v)):
                r[WINDOW:WINDOW + tt, :] = val
                r[0:WINDOW, :] = rc[tt:tt + WINDOW, :]

        p_gen = _project_stages(xp_ref[0], (cos_ref, sdn_ref, sup_ref), wpre_ref, win_ref, wup_ref, bgk_ref,
                                CHUNK, slot_p, store_kv)

        def c_stages():
            old = [s_ref[p * LANES:(p + 1) * LANES, :] for p in range(GLA_HEADS // 2)]
            keep = t_c != 0
            state = [jnp.where(keep, o, 0.0) for o in old]
            win = WINDOW + CHUNK
            chunk_args = []
            for r0 in range(0, tt, CHUNK):
                n_invalid = WINDOW - (t_c * tt + r0) if r0 < WINDOW else None
                chunk_args.append((r0, lambda i, r0=r0: slot_c.kv[i][r0:r0 + win, :], n_invalid))
            yield from _chunk_stages(slot_c, chunk_args, CHUNK, lambda i: state, sinks_ref, gnw_ref)
            for p in range(GLA_HEADS // 2):
                new = jnp.where(c_valid, state[p], old[p])
                s_ref[p * LANES:(p + 1) * LANES, :] = new
                sout_ref[0, p * LANES:(p + 1) * LANES, :] = new

        def f_stages():
            y_ref[0] = _finish(xf_ref[0], slot_p.mix, wout_ref, wpost_ref)
            yield

        _run_order(PROMPT_ORDER, {"F": f_stages(), "C": c_stages(), "P": p_gen})

    @pl.when(s % 2 == 0)
    def _():
        body(slots[0], slots[1])

    @pl.when(s % 2 == 1)
    def _():
        body(slots[1], slots[0])


def _sample_kernel(sinks_ref, x_ref, cos_ref, sdn_ref, sup_ref, ck_ref, cv_ref, s0_ref, wpre_ref, win_ref,
                   wup_ref, bgk_ref, gnw_ref, wout_ref, wpost_ref,
                   y_ref, kout_ref, vout_ref, sout_ref, *scratch):
    slot = _make_slot(scratch[:N_SLOT_REFS])
    nb = ck_ref.shape[0]
    c = x_ref.shape[0] // nb
    win = WINDOW + c

    def store_kv(k, v):
        kout_ref[...] = k
        vout_ref[...] = v
        for i in range(nb):
            rows = slice(i * c, (i + 1) * c)
            kv = _kv_variants(jnp.concatenate([ck_ref[i], k[rows]], axis=0),
                              jnp.concatenate([cv_ref[i], v[rows]], axis=0))
            for r, val in zip(slot.kv, kv):
                r[i * win:(i + 1) * win, :] = val

    x = x_ref[...]
    for _ in _project_stages(x, (cos_ref, sdn_ref, sup_ref), wpre_ref, win_ref, wup_ref, bgk_ref, c, slot, store_kv):
        pass
    states = [[s0_ref[i, p * LANES:(p + 1) * LANES, :] for p in range(GLA_HEADS // 2)] for i in range(nb)]
    chunk_args = [(i * c, lambda j, i=i: slot.kv[j][i * win:(i + 1) * win, :], None) for i in range(nb)]
    for _ in _chunk_stages(slot, chunk_args, c, lambda i: states[i], sinks_ref, gnw_ref):
        pass
    for i in range(nb):
        for p in range(GLA_HEADS // 2):
            sout_ref[i, p * LANES:(p + 1) * LANES, :] = states[i][p]
    y_ref[...] = _finish(x, slot.mix, wout_ref, wpost_ref)


def _rotary_tables(pos):
    half = ROT_DIM // 2
    inv = ROPE_THETA ** (-jnp.arange(half, dtype=jnp.float32) * (2.0 / ROT_DIM))
    ang = pos.astype(jnp.float32)[:, None] * inv[None, :]
    cos, sin = jnp.cos(ang), jnp.sin(ang)
    d = jnp.arange(LANES) % HEAD_DIM
    cos_t = (d >= ROT_DIM).astype(jnp.float32)[None, :]
    sin_dn = sin_up = jnp.zeros((1, LANES), jnp.float32)
    for i in range(half):
        lo_i = (d == i).astype(jnp.float32)[None, :]
        hi_i = (d == half + i).astype(jnp.float32)[None, :]
        cos_t = cos_t + cos[:, i:i + 1] * (lo_i + hi_i)
        sin_dn = sin_dn + sin[:, i:i + 1] * hi_i
        sin_up = sin_up - sin[:, i:i + 1] * lo_i
    return cos_t, sin_dn, sin_up


def _cast_w_in_kernel(wt_ref, out_ref):
    row = lax.broadcasted_iota(jnp.int32, wt_ref.shape, 0) + pl.program_id(0) * CAST_BLOCK
    out_ref[...] = _bf(jnp.where(row < D_IN, wt_ref[...], 0.0).T)


def _cast_w_in(w_in_t):
    return pl.pallas_call(
        _cast_w_in_kernel,
        grid=(W_IN_ALLOC // CAST_BLOCK,),
        in_specs=[pl.BlockSpec((CAST_BLOCK, D_MODEL), lambda i: (i, 0))],
        out_specs=pl.BlockSpec((D_MODEL, CAST_BLOCK), lambda i: (0, i)),
        out_shape=jax.ShapeDtypeStruct((D_MODEL, W_IN_ALLOC), jnp.bfloat16),
        compiler_params=pltpu.CompilerParams(dimension_semantics=("arbitrary",)),
        name="cast_w_in",
    )(w_in_t)


def _const_spec(shape):
    zeros = (0,) * len(shape)
    return pl.BlockSpec(shape, lambda i: zeros, pipeline_mode=pl.Buffered(1))


def kernel(x_prompt, x_sample, cache_k, cache_v, state_gla, norm_pre_w, w_in, attn_sinks, w_gk_up, b_gk,
           gla_norm_w, w_out, norm_post_w):
    bsz, t_p, _ = x_prompt.shape
    dec_b, t_s, _ = x_sample.shape
    assert w_in.shape[0] == 1, "single layer"
    assert t_p % PROMPT_TILE == 0 and PROMPT_TILE % CHUNK == 0 and PROMPT_TILE >= WINDOW
    assert dec_b % SAMPLE_SEQS == 0 and t_s & (t_s - 1) == 0

    win = _cast_w_in(w_in[0].T)
    wup = _bf(jnp.pad(w_gk_up[0], ((0, LANES - GLA_RANK), (0, 0))))
    wout = _bf(w_out[0])
    wpre = norm_pre_w[0][None, :]
    wpost = norm_post_w[0][None, :]
    bgk = b_gk[0][None, :]
    gnw = gla_norm_w[0][None, :]
    sinks = attn_sinks[0][None, :]
    smem = pl.BlockSpec(memory_space=pltpu.SMEM)

    weights_specs = [
        _const_spec((1, D_MODEL)), _const_spec((D_MODEL, D_IN_PAD)), _const_spec((LANES, GQK_W)),
        _const_spec((1, GQK_W)), _const_spec((1, GLA_DV)), _const_spec((D_MODEL, D_MODEL)),
        _const_spec((1, D_MODEL))]

    tt = PROMPT_TILE
    nt = t_p // tt
    n_tiles = bsz * nt
    cos_p, sdn_p, sup_p = _rotary_tables(jnp.arange(t_p))

    def tile_p(s):
        return jnp.minimum(s, n_tiles - 1)

    def tile_c(s):
        return jnp.clip(s - 1, 0, n_tiles - 1)

    def tile_f(s):
        return jnp.maximum(s - 2, 0)

    tab_spec = pl.BlockSpec((tt, LANES), lambda s: (tile_p(s) % nt, 0))
    x_spec = lambda tile: pl.BlockSpec((1, tt, D_MODEL), lambda s: (tile(s) // nt, tile(s) % nt, 0))
    seq_spec = lambda tile, d1, d2: pl.BlockSpec((1, d1, d2), lambda s: (tile(s) // nt, 0, 0))
    y_p, k_p, v_p, s_p = pl.pallas_call(
        functools.partial(_prompt_kernel, nt, n_tiles),
        grid=(n_tiles + 2,),
        in_specs=[smem, x_spec(tile_p), x_spec(tile_f), tab_spec, tab_spec, tab_spec] + weights_specs,
        out_specs=[x_spec(tile_f), seq_spec(tile_p, WINDOW, KV_W), seq_spec(tile_p, WINDOW, KV_W),
                   seq_spec(tile_c, S_ROWS, GLA_DV)],
        out_shape=[jax.ShapeDtypeStruct((bsz, t_p, D_MODEL), jnp.float32),
                   jax.ShapeDtypeStruct((bsz, WINDOW, KV_W), jnp.float32),
                   jax.ShapeDtypeStruct((bsz, WINDOW, KV_W), jnp.float32),
                   jax.ShapeDtypeStruct((bsz, S_ROWS, GLA_DV), jnp.float32)],
        scratch_shapes=_slot_shapes(tt, WINDOW + tt, tt // CHUNK) * 2
        + [pltpu.VMEM((S_ROWS, GLA_DV), jnp.float32)],
        compiler_params=pltpu.CompilerParams(dimension_semantics=("arbitrary",), vmem_limit_bytes=VMEM_LIMIT),
        name="prompt_layer",
    )(sinks, x_prompt, x_prompt, cos_p, sdn_p, sup_p, wpre, win, wup, bgk, gnw, wout, wpost)

    nb = SAMPLE_SEQS
    rows = nb * t_s
    tabs = [jnp.tile(a, (nb, 1)) for a in _rotary_tables(PAST_LEN + jnp.arange(t_s))]
    tab_spec = pl.BlockSpec((rows, LANES), lambda i: (0, 0))
    seq3 = lambda d1, d2: pl.BlockSpec((nb, d1, d2), lambda i: (i, 0, 0))
    tok = lambda d: pl.BlockSpec((rows, d), lambda i: (i, 0))
    y_s, k_s, v_s, s_s = pl.pallas_call(
        _sample_kernel,
        grid=(dec_b // nb,),
        in_specs=[smem, tok(D_MODEL), tab_spec, tab_spec, tab_spec, seq3(WINDOW, KV_W), seq3(WINDOW, KV_W),
                  seq3(S_ROWS, GLA_DV)] + weights_specs,
        out_specs=[tok(D_MODEL), tok(KV_W), tok(KV_W), seq3(S_ROWS, GLA_DV)],
        out_shape=[jax.ShapeDtypeStruct((dec_b * t_s, D_MODEL), jnp.float32),
                   jax.ShapeDtypeStruct((dec_b * t_s, KV_W), jnp.float32),
                   jax.ShapeDtypeStruct((dec_b * t_s, KV_W), jnp.float32),
                   jax.ShapeDtypeStruct((dec_b, S_ROWS, GLA_DV), jnp.float32)],
        scratch_shapes=_slot_shapes(rows, nb * (WINDOW + t_s), nb),
        compiler_params=pltpu.CompilerParams(dimension_semantics=("arbitrary",), vmem_limit_bytes=VMEM_LIMIT),
        name="sample_layer",
    )(sinks, x_sample.reshape(dec_b * t_s, D_MODEL), *tabs,
      cache_k[0].reshape(dec_b, WINDOW, KV_W), cache_v[0].reshape(dec_b, WINDOW, KV_W),
      state_gla[0].reshape(dec_b, S_ROWS, GLA_DV), wpre, win, wup, bgk, gnw, wout, wpost)

    kv5 = lambda a, b, t: a.reshape(1, b, t, N_KV_HEADS, HEAD_DIM)
    st5 = lambda a, b: a.reshape(1, b, GLA_HEADS, GLA_DK, GLA_DV)
    return (y_p, y_s.reshape(dec_b, t_s, D_MODEL),
            kv5(k_p, bsz, WINDOW), kv5(v_p, bsz, WINDOW), st5(s_p, bsz),
            kv5(k_s, dec_b, t_s), kv5(v_s, dec_b, t_s), st5(s_s, dec_b))
```

```python
import collections
import functools

import jax
import jax.numpy as jnp
from jax import lax
from jax.experimental import pallas as pl
from jax.experimental.pallas import tpu as pltpu

D_MODEL = 1024
CHUNK = 64
WINDOW = 128
HEAD_DIM = 64
N_HEADS = 8
N_KV_HEADS = 2
GQA_GROUP = N_HEADS // N_KV_HEADS
ROT_DIM = 16
ROPE_THETA = 500000.0
GLA_HEADS = 4
GLA_DV = 128
GLA_DK = 64
GLA_RANK = 16
GATE_TAU = 16.0
NORM_EPS = 1e-6
PAST_LEN = 4096

LANES = 128
SUBLANES = 8
ATTN_W = N_HEADS * HEAD_DIM
KV_W = N_KV_HEADS * HEAD_DIM
GQK_W = GLA_HEADS * GLA_DK
GV_W = GLA_HEADS * GLA_DV
S_ROWS = GLA_HEADS * GLA_DK
OFF_AQ = 0
OFF_AK = OFF_AQ + ATTN_W
OFF_AV = OFF_AK + KV_W
OFF_AG = OFF_AV + KV_W
OFF_GQ = OFF_AG + ATTN_W
OFF_GK = OFF_GQ + GQK_W
OFF_GV = OFF_GK + GQK_W
OFF_GG = OFF_GV + GV_W
OFF_LR = OFF_GG + GV_W
D_IN = OFF_LR + GLA_RANK
D_IN_PAD = OFF_LR + LANES
CAST_BLOCK = 1024
W_IN_ALLOC = -(-D_IN_PAD // CAST_BLOCK) * CAST_BLOCK

PROMPT_TILE = 256
SAMPLE_SEQS = 8
VMEM_LIMIT = 48 * 1024 * 1024
PROMPT_ORDER = "CPCPCPCPCPPPP"
CHUNK_GROUP = 4

NEG = -1e30
LOG2E = 1.4426950408889634

Slot = collections.namedtuple("Slot", "qs kv ag qin kog kdec dec gv gg")


def _slot_shapes(rows, kv_rows, n_chunks):
    bf, f32 = jnp.bfloat16, jnp.float32
    return ([pltpu.VMEM((rows, 2 * ATTN_W), bf)] + [pltpu.VMEM((kv_rows, LANES), bf) for _ in range(6)]
            + [pltpu.VMEM((rows, ATTN_W), f32), pltpu.VMEM((rows, 2 * GQK_W), bf), pltpu.VMEM((rows, GQK_W), bf),
               pltpu.VMEM((rows, GQK_W), f32), pltpu.VMEM((n_chunks * SUBLANES, GQK_W), f32),
               pltpu.VMEM((rows, GV_W), bf), pltpu.VMEM((rows, GV_W), f32)])


N_SLOT_REFS = 14


def _make_slot(refs):
    return Slot(refs[0], tuple(refs[1:7]), *refs[7:14])


def _dot(a, b):
    return jnp.dot(a, b, preferred_element_type=jnp.float32)


def _dot_nt(a, b):
    return lax.dot_general(a, b, (((1,), (1,)), ((), ())), preferred_element_type=jnp.float32)


def _bf(x):
    return x.astype(jnp.bfloat16)


def _lane_masks():
    lane = lax.broadcasted_iota(jnp.int32, (1, LANES), 1)
    return lane < HEAD_DIM, lane >= HEAD_DIM


def _silu(x):
    return x * (1.0 / (1.0 + jnp.exp(-x)))


def _log_sigmoid(x):
    return jnp.minimum(x, 0.0) - jnp.log(1.0 + jnp.exp(-jnp.abs(x)))


def _rms_scale(x):
    return lax.rsqrt(jnp.mean(x * x, axis=-1, keepdims=True) + NORM_EPS)


def _chunk_cumsum(g, chunk):
    row = lax.broadcasted_iota(jnp.int32, g.shape, 0) & (chunk - 1)
    s = 1
    while s < chunk:
        g = g + jnp.where(row >= s, pltpu.roll(g, s, 0), 0.0)
        s *= 2
    return g


def _rotary(xcol, cos_t, sin_dn, sin_up):
    return (xcol * cos_t + pltpu.roll(xcol, ROT_DIM // 2, 1) * sin_dn
            + pltpu.roll(xcol, LANES - ROT_DIM // 2, 1) * sin_up)


def _rotary_lanes(cos_off, sin_off, cos_base, sin_base):
    d = lax.broadcasted_iota(jnp.int32, (1, LANES), 1) & (HEAD_DIM - 1)
    cb, sb = cos_base[0:1, :], sin_base[0:1, :]
    co, so = cos_off[...], sin_off[...]
    cos = cb * co - sb * so
    sin = sb * co + cb * so
    half = ROT_DIM // 2
    return (jnp.where(d < ROT_DIM, cos, 1.0), jnp.where((d >= half) & (d < ROT_DIM), sin, 0.0),
            jnp.where(d < half, -sin, 0.0))


def _kv_variants(k, v):
    m_lo, m_hi = _lane_masks()
    k_sw = pltpu.roll(k, HEAD_DIM, 1)
    v_sw = pltpu.roll(v, HEAD_DIM, 1)
    return (_bf(k), _bf(k_sw),
            _bf(jnp.where(m_lo, v, 0.0)), _bf(jnp.where(m_hi, v_sw, 0.0)),
            _bf(jnp.where(m_lo, v_sw, 0.0)), _bf(jnp.where(m_hi, v, 0.0)))


def _store_halves(ref, j, width, col):
    m_lo, m_hi = _lane_masks()
    ref[:, j * LANES:(j + 1) * LANES] = _bf(jnp.where(m_lo, col, 0.0))
    ref[:, width + j * LANES:width + (j + 1) * LANES] = _bf(jnp.where(m_hi, col, 0.0))


def _project_stages(x, tabs, wpre_ref, win_ref, wup_ref, bgk_ref, chunk, slot, store_kv):
    rows = x.shape[0]
    h = _bf(x * _rms_scale(x) * wpre_ref[...])
    cos_t, sin_dn, sin_up = _rotary_lanes(*tabs)

    def seg(off, width):
        return _dot(h, win_ref[:, off:off + width])

    yield
    q = seg(OFF_AQ, ATTN_W)
    for j in range(ATTN_W // LANES):
        ln = slice(j * LANES, (j + 1) * LANES)
        _store_halves(slot.qs, j, ATTN_W, _rotary(q[:, ln], cos_t, sin_dn, sin_up) * (LOG2E * HEAD_DIM ** -0.5))
    yield
    kv = seg(OFF_AK, 2 * KV_W)
    store_kv(_rotary(kv[:, :KV_W], cos_t, sin_dn, sin_up), kv[:, KV_W:])
    yield
    slot.ag[...] = _silu(seg(OFF_AG, ATTN_W))
    yield
    lr = _bf(seg(OFF_LR, LANES))
    log_a = _log_sigmoid(_dot(lr, wup_ref[...]) + bgk_ref[...]) * (1.0 / GATE_TAU)
    b = _chunk_cumsum(log_a, chunk)
    lasts = [b[r + chunk - 1:r + chunk] for r in range(0, rows, chunk)]
    for i, bl in enumerate(lasts):
        slot.dec[i * SUBLANES:(i + 1) * SUBLANES, :] = jnp.broadcast_to(jnp.exp(bl), (SUBLANES, GQK_W))
    b_last = jnp.concatenate([jnp.broadcast_to(bl, (chunk, GQK_W)) for bl in lasts], axis=0)
    q_in = seg(OFF_GQ, GQK_W) * (GLA_DK ** -0.5) * jnp.exp(b)
    for j in range(GQK_W // LANES):
        _store_halves(slot.qin, j, GQK_W, q_in[:, j * LANES:(j + 1) * LANES])
    yield
    gk = seg(OFF_GK, GQK_W)
    slot.kog[...] = _bf(gk * jnp.exp(-b))
    slot.kdec[...] = gk * jnp.exp(b_last - b)
    yield
    slot.gv[...] = _bf(seg(OFF_GV, GV_W))
    yield
    slot.gg[...] = _silu(seg(OFF_GG, GV_W))


def _attend_stages(slot, rows, kv_win, sinks_ref, n_invalid, emit):
    c = rows.stop - rows.start
    lo = [slot.qs[rows, j * LANES:(j + 1) * LANES] for j in range(ATTN_W // LANES)]
    hi = [slot.qs[rows, ATTN_W + j * LANES:ATTN_W + (j + 1) * LANES] for j in range(ATTN_W // LANES)]
    k_a, k_b = kv_win(0), kv_win(1)
    n_hi = k_a.shape[0] - LANES
    zrows = jnp.zeros((LANES - n_hi, LANES), jnp.bfloat16)
    ext = lambda a: jnp.concatenate([a, zrows], axis=0)
    s_a = _dot_nt(jnp.concatenate([lo[0], lo[1], hi[2], hi[3]], axis=0), ext(k_a))
    s_b = _dot_nt(jnp.concatenate([hi[0], hi[1], lo[2], lo[3]], axis=0), ext(k_b))
    yield
    lane = lax.broadcasted_iota(jnp.int32, (1, LANES), 1)
    where = {0: (s_a, 0), 2: (s_a, 1), 5: (s_a, 2), 7: (s_a, 3),
             1: (s_b, 0), 3: (s_b, 1), 4: (s_b, 2), 6: (s_b, 3)}
    outs = []
    for h in range(N_HEADS):
        src, blk = where[h]
        s = src[blk * c:(blk + 1) * c]
        pad_row = jnp.where(lane == n_hi, sinks_ref[0, h] * LOG2E, NEG)
        s_lo = s[:, :LANES]
        s_hi = jnp.where(lane >= n_hi, pad_row, s[:, LANES:])
        if n_invalid is not None:
            s_lo = jnp.where(lane < n_invalid, NEG, s_lo)
        m = jnp.max(jnp.maximum(s_lo, s_hi), axis=-1, keepdims=True)
        p_lo = jnp.exp2(s_lo - m)
        p_hi = jnp.exp2(s_hi - m)
        denom = jnp.sum(p_lo + p_hi, axis=-1, keepdims=True)
        o = _dot(_bf(jnp.concatenate([p_lo, p_hi], axis=1)), ext(kv_win(2 + 2 * (h // GQA_GROUP) + h % 2)))
        outs.append(o * (1.0 / denom))
    yield
    emit(jnp.concatenate([outs[2 * j] + outs[2 * j + 1] for j in range(4)], axis=1))


def _gla_stages(slot, r0, c, state, gnw_ref, mix_ref):
    rows = slice(r0, r0 + c)
    ci0 = (r0 // c) * SUBLANES
    pad = LANES - c
    lhs, a2, uw, dcol, vw = [], [], [], [], []
    for p in range(GLA_HEADS // 2):
        ln = slice(p * LANES, (p + 1) * LANES)
        vwide = slot.gv[rows, p * 2 * GLA_DV:(p + 1) * 2 * GLA_DV]
        lhs.append(jnp.concatenate([slot.qin[rows, ln], slot.qin[rows, GQK_W + p * LANES:GQK_W + (p + 1) * LANES]],
                                   axis=0))
        a2.append(_dot_nt(lhs[p], slot.kog[rows, ln]))
        kd_t = jnp.concatenate([slot.kdec[rows, ln], jnp.zeros((pad, LANES), jnp.float32)], axis=0).T
        v_pad = jnp.concatenate([vwide, jnp.zeros((pad, 2 * GLA_DV), jnp.bfloat16)], axis=0)
        uw.append(_dot(_bf(kd_t), v_pad))
        dcol.append(jnp.broadcast_to(slot.dec[ci0:ci0 + 1, ln], (LANES, LANES)).T)
        vw.append(vwide)
    yield
    outs = []
    ri = lax.broadcasted_iota(jnp.int32, a2[0].shape, 0) & (c - 1)
    ci = lax.broadcasted_iota(jnp.int32, a2[0].shape, 1)
    for p in range(GLA_HEADS // 2):
        am = _bf(jnp.where(ri >= ci, a2[p], 0.0))
        sb = _bf(state[p])
        for i in range(2):
            hr = slice(i * c, (i + 1) * c)
            outs.append(_dot(jnp.concatenate([lhs[p][hr], am[hr]], axis=1),
                             jnp.concatenate([sb, vw[p][:, i * GLA_DV:(i + 1) * GLA_DV]], axis=0)))
        u_pair = jnp.concatenate([uw[p][:GLA_DK, :GLA_DV], uw[p][GLA_DK:, GLA_DV:]], axis=0)
        state[p] = dcol[p] * state[p] + u_pair
    yield
    gnw = gnw_ref[...]
    for hh, o in enumerate(outs):
        y = o * _rms_scale(o) * gnw * slot.gg[rows, hh * GLA_DV:(hh + 1) * GLA_DV]
        mix_ref[rows, ATTN_W + hh * GLA_DV:ATTN_W + (hh + 1) * GLA_DV] = _bf(y)


def _lockstep(gens):
    live = list(gens)
    while live:
        for g in list(live):
            try:
                next(g)
            except StopIteration:
                live.remove(g)
        yield


def _chunk_stages(slot, chunk_args, c, state_of, sinks_ref, gnw_ref, mix_ref):
    gens = []
    for i, (r0, kv_win, n_invalid) in enumerate(chunk_args):
        rows = slice(r0, r0 + c)

        def emit(attn, rows=rows):
            mix_ref[rows, 0:ATTN_W] = _bf(attn * slot.ag[rows, :])

        gens.append(_attend_stages(slot, rows, kv_win, sinks_ref, n_invalid, emit))
        gens.append(_gla_stages(slot, r0, c, state_of(i), gnw_ref, mix_ref))
    for g in range(0, len(gens), 2 * CHUNK_GROUP):
        yield from _lockstep(gens[g:g + 2 * CHUNK_GROUP])


def _finish(x, mix_ref, wout_ref, wpost_ref):
    mix = _dot(mix_ref[...], wout_ref[...])
    return x + mix * _rms_scale(mix) * wpost_ref[...]


def _run_order(order, p_gen, c_gen):
    for ch in order:
        next(p_gen if ch == "P" else c_gen, None)
    for g in (p_gen, c_gen):
        for _ in g:
            pass


def _prompt_kernel(nt, sinks_ref, xp_ref, xc_ref, coff_ref, soff_ref, cbase_ref, sbase_ref, wpre_ref, win_ref, wup_ref,
                   bgk_ref, gnw_ref, wout_ref, wpost_ref,
                   y_ref, kout_ref, vout_ref, sout_ref, *scratch):
    slots = (_make_slot(scratch[:N_SLOT_REFS]), _make_slot(scratch[N_SLOT_REFS:2 * N_SLOT_REFS]))
    s_ref, mix_ref = scratch[2 * N_SLOT_REFS:]
    s = pl.program_id(0)
    tt = xp_ref.shape[1]
    t_c = jnp.maximum(s - 1, 0) % nt

    @pl.when(s == 0)
    def _():
        for r in scratch[N_SLOT_REFS:]:
            r[...] = jnp.zeros_like(r)

    def body(slot_p, slot_c):
        def store_kv(k, v):
            kout_ref[0] = k[tt - WINDOW:]
            vout_ref[0] = v[tt - WINDOW:]
            for r, rc, val in zip(slot_p.kv, slot_c.kv, _kv_variants(k, v)):
                r[WINDOW:WINDOW + tt, :] = val
                r[0:WINDOW, :] = rc[tt:tt + WINDOW, :]

        p_gen = _project_stages(xp_ref[0], (coff_ref, soff_ref, cbase_ref, sbase_ref), wpre_ref, win_ref, wup_ref,
                                bgk_ref, CHUNK, slot_p, store_kv)

        def c_stages():
            keep = t_c != 0
            state = [jnp.where(keep, s_ref[p * LANES:(p + 1) * LANES, :], 0.0) for p in range(GLA_HEADS // 2)]
            win = WINDOW + CHUNK
            chunk_args = []
            for r0 in range(0, tt, CHUNK):
                n_invalid = WINDOW - (t_c * tt + r0) if r0 < WINDOW else None
                chunk_args.append((r0, lambda i, r0=r0: slot_c.kv[i][r0:r0 + win, :], n_invalid))
            yield from _chunk_stages(slot_c, chunk_args, CHUNK, lambda i: state, sinks_ref, gnw_ref, mix_ref)
            y_ref[0] = _finish(xc_ref[0], mix_ref, wout_ref, wpost_ref)
            for p in range(GLA_HEADS // 2):
                s_ref[p * LANES:(p + 1) * LANES, :] = state[p]
                sout_ref[0, p * LANES:(p + 1) * LANES, :] = state[p]

        _run_order(PROMPT_ORDER, p_gen, c_stages())

    @pl.when(s % 2 == 0)
    def _():
        body(slots[0], slots[1])

    @pl.when(s % 2 == 1)
    def _():
        body(slots[1], slots[0])


def _sample_kernel(sinks_ref, x_ref, coff_ref, soff_ref, cbase_ref, sbase_ref, ck_ref, cv_ref, s0_ref, wpre_ref,
                   win_ref, wup_ref, bgk_ref, gnw_ref, wout_ref, wpost_ref,
                   y_ref, kout_ref, vout_ref, sout_ref, *scratch):
    slot = _make_slot(scratch[:N_SLOT_REFS])
    mix_ref = scratch[N_SLOT_REFS]
    nb = ck_ref.shape[0]
    c = x_ref.shape[0] // nb
    win = WINDOW + c

    def store_kv(k, v):
        kout_ref[...] = k
        vout_ref[...] = v
        for i in range(nb):
            rows = slice(i * c, (i + 1) * c)
            kv = _kv_variants(jnp.concatenate([ck_ref[i], k[rows]], axis=0),
                              jnp.concatenate([cv_ref[i], v[rows]], axis=0))
            for r, val in zip(slot.kv, kv):
                r[i * win:(i + 1) * win, :] = val

    x = x_ref[...]
    for _ in _project_stages(x, (coff_ref, soff_ref, cbase_ref, sbase_ref), wpre_ref, win_ref, wup_ref, bgk_ref, c, slot,
                             store_kv):
        pass
    states = [[s0_ref[i, p * LANES:(p + 1) * LANES, :] for p in range(GLA_HEADS // 2)] for i in range(nb)]
    chunk_args = [(i * c, lambda j, i=i: slot.kv[j][i * win:(i + 1) * win, :], None) for i in range(nb)]
    for _ in _chunk_stages(slot, chunk_args, c, lambda i: states[i], sinks_ref, gnw_ref, mix_ref):
        pass
    for i in range(nb):
        for p in range(GLA_HEADS // 2):
            sout_ref[i, p * LANES:(p + 1) * LANES, :] = states[i][p]
    y_ref[...] = _finish(x, mix_ref, wout_ref, wpost_ref)


def _rotary_angles(offsets, bases):
    half = ROT_DIM // 2
    d = jnp.arange(LANES) % HEAD_DIM
    inv = ROPE_THETA ** (-(d % half).astype(jnp.float32) * (2.0 / ROT_DIM))
    pos = jnp.concatenate([offsets, jnp.repeat(bases, SUBLANES)]).astype(jnp.float32)
    ang = pos[:, None] * inv[None, :]
    return jnp.cos(ang), jnp.sin(ang)


def _cast_w_in_kernel(wt_ref, out_ref):
    row = lax.broadcasted_iota(jnp.int32, wt_ref.shape, 0) + pl.program_id(0) * CAST_BLOCK
    out_ref[...] = _bf(jnp.where(row < D_IN, wt_ref[...], 0.0).T)


def _cast_w_in(w_in_t):
    return pl.pallas_call(
        _cast_w_in_kernel,
        grid=(W_IN_ALLOC // CAST_BLOCK,),
        in_specs=[pl.BlockSpec((CAST_BLOCK, D_MODEL), lambda i: (i, 0))],
        out_specs=pl.BlockSpec((D_MODEL, CAST_BLOCK), lambda i: (0, i)),
        out_shape=jax.ShapeDtypeStruct((D_MODEL, W_IN_ALLOC), jnp.bfloat16),
        compiler_params=pltpu.CompilerParams(dimension_semantics=("arbitrary",)),
        name="cast_w_in",
    )(w_in_t)


def _const_spec(shape):
    zeros = (0,) * len(shape)
    return pl.BlockSpec(shape, lambda i: zeros, pipeline_mode=pl.Buffered(1))


def kernel(x_prompt, x_sample, cache_k, cache_v, state_gla, norm_pre_w, w_in, attn_sinks, w_gk_up, b_gk,
           gla_norm_w, w_out, norm_post_w):
    bsz, t_p, _ = x_prompt.shape
    dec_b, t_s, _ = x_sample.shape
    assert w_in.shape[0] == 1, "single layer"
    assert t_p % PROMPT_TILE == 0 and PROMPT_TILE % CHUNK == 0 and PROMPT_TILE >= WINDOW
    assert dec_b % SAMPLE_SEQS == 0 and t_s & (t_s - 1) == 0

    win = _cast_w_in(w_in[0].T)
    wup = _bf(jnp.pad(w_gk_up[0], ((0, LANES - GLA_RANK), (0, 0))))
    wout = _bf(w_out[0])
    wpre = norm_pre_w[0][None, :]
    wpost = norm_post_w[0][None, :]
    bgk = b_gk[0][None, :]
    gnw = gla_norm_w[0][None, :]
    sinks = attn_sinks[0][None, :]
    smem = pl.BlockSpec(memory_space=pltpu.SMEM)

    weights_specs = [
        _const_spec((1, D_MODEL)), _const_spec((D_MODEL, D_IN_PAD)), _const_spec((LANES, GQK_W)),
        _const_spec((1, GQK_W)), _const_spec((1, GLA_DV)), _const_spec((D_MODEL, D_MODEL)),
        _const_spec((1, D_MODEL))]

    tt = PROMPT_TILE
    nt = t_p // tt
    n_tiles = bsz * nt
    nb = SAMPLE_SEQS
    rows = nb * t_s
    assert tt % rows == 0 and (tt + rows) % SUBLANES == 0
    cos_all, sin_all = _rotary_angles(jnp.concatenate([jnp.arange(tt), jnp.tile(jnp.arange(t_s), nb)]),
                                      jnp.concatenate([jnp.arange(nt) * tt, jnp.array([PAST_LEN])]))
    base_blk = (tt + rows) // SUBLANES

    def tile_p(s):
        return jnp.minimum(s, n_tiles - 1)

    def tile_c(s):
        return jnp.maximum(s - 1, 0)

    off_spec = pl.BlockSpec((tt, LANES), lambda s: (0, 0))
    base_spec = pl.BlockSpec((SUBLANES, LANES), lambda s: (base_blk + tile_p(s) % nt, 0))
    x_spec = lambda tile: pl.BlockSpec((1, tt, D_MODEL), lambda s: (tile(s) // nt, tile(s) % nt, 0))
    seq_spec = lambda tile, d1, d2: pl.BlockSpec((1, d1, d2), lambda s: (tile(s) // nt, 0, 0))
    y_p, k_p, v_p, s_p = pl.pallas_call(
        functools.partial(_prompt_kernel, nt),
        grid=(n_tiles + 1,),
        in_specs=[smem, x_spec(tile_p), x_spec(tile_c), off_spec, off_spec, base_spec, base_spec] + weights_specs,
        out_specs=[x_spec(tile_c), seq_spec(tile_p, WINDOW, KV_W), seq_spec(tile_p, WINDOW, KV_W),
                   seq_spec(tile_c, S_ROWS, GLA_DV)],
        out_shape=[jax.ShapeDtypeStruct((bsz, t_p, D_MODEL), jnp.float32),
                   jax.ShapeDtypeStruct((bsz, WINDOW, KV_W), jnp.float32),
                   jax.ShapeDtypeStruct((bsz, WINDOW, KV_W), jnp.float32),
                   jax.ShapeDtypeStruct((bsz, S_ROWS, GLA_DV), jnp.float32)],
        scratch_shapes=_slot_shapes(tt, WINDOW + tt, tt // CHUNK) * 2
        + [pltpu.VMEM((S_ROWS, GLA_DV), jnp.float32), pltpu.VMEM((tt, D_MODEL), jnp.bfloat16)],
        compiler_params=pltpu.CompilerParams(dimension_semantics=("arbitrary",), vmem_limit_bytes=VMEM_LIMIT),
        name="prompt_layer",
    )(sinks, x_prompt, x_prompt, cos_all, sin_all, cos_all, sin_all, wpre, win, wup, bgk, gnw, wout, wpost)

    off_spec = pl.BlockSpec((rows, LANES), lambda i: (tt // rows, 0))
    base_spec = pl.BlockSpec((SUBLANES, LANES), lambda i: (base_blk + nt, 0))
    seq3 = lambda d1, d2: pl.BlockSpec((nb, d1, d2), lambda i: (i, 0, 0))
    tok = lambda d: pl.BlockSpec((rows, d), lambda i: (i, 0))
    y_s, k_s, v_s, s_s = pl.pallas_call(
        _sample_kernel,
        grid=(dec_b // nb,),
        in_specs=[smem, tok(D_MODEL), off_spec, off_spec, base_spec, base_spec, seq3(WINDOW, KV_W), seq3(WINDOW, KV_W),
                  seq3(S_ROWS, GLA_DV)] + weights_specs,
        out_specs=[tok(D_MODEL), tok(KV_W), tok(KV_W), seq3(S_ROWS, GLA_DV)],
        out_shape=[jax.ShapeDtypeStruct((dec_b * t_s, D_MODEL), jnp.float32),
                   jax.ShapeDtypeStruct((dec_b * t_s, KV_W), jnp.float32),
                   jax.ShapeDtypeStruct((dec_b * t_s, KV_W), jnp.float32),
                   jax.ShapeDtypeStruct((dec_b, S_ROWS, GLA_DV), jnp.float32)],
        scratch_shapes=_slot_shapes(rows, nb * (WINDOW + t_s), nb) + [pltpu.VMEM((rows, D_MODEL), jnp.bfloat16)],
        compiler_params=pltpu.CompilerParams(dimension_semantics=("arbitrary",), vmem_limit_bytes=VMEM_LIMIT),
        name="sample_layer",
    )(sinks, x_sample.reshape(dec_b * t_s, D_MODEL), cos_all, sin_all, cos_all, sin_all,
      cache_k[0].reshape(dec_b, WINDOW, KV_W), cache_v[0].reshape(dec_b, WINDOW, KV_W),
      state_gla[0].reshape(dec_b, S_ROWS, GLA_DV), wpre, win, wup, bgk, gnw, wout, wpost)

    kv5 = lambda a, b, t: a.reshape(1, b, t, N_KV_HEADS, HEAD_DIM)
    st5 = lambda a, b: a.reshape(1, b, GLA_HEADS, GLA_DK, GLA_DV)
    return (y_p, y_s.reshape(dec_b, t_s, D_MODEL),
            kv5(k_p, bsz, WINDOW), kv5(v_p, bsz, WINDOW), st5(s_p, bsz),
            kv5(k_s, dec_b, t_s), kv5(v_s, dec_b, t_s), st5(s_s, dec_b))
```

```python
import collections
import functools

import jax
import jax.numpy as jnp
from jax import lax
from jax.experimental import pallas as pl
from jax.experimental.pallas import tpu as pltpu

D_MODEL = 1024
CHUNK = 64
WINDOW = 128
HEAD_DIM = 64
N_HEADS = 8
N_KV_HEADS = 2
GQA_GROUP = N_HEADS // N_KV_HEADS
ROT_DIM = 16
ROPE_THETA = 500000.0
GLA_HEADS = 4
GLA_DV = 128
GLA_DK = 64
GLA_RANK = 16
GATE_TAU = 16.0
NORM_EPS = 1e-6
PAST_LEN = 4096

LANES = 128
SUBLANES = 8
ATTN_W = N_HEADS * HEAD_DIM
KV_W = N_KV_HEADS * HEAD_DIM
GQK_W = GLA_HEADS * GLA_DK
GV_W = GLA_HEADS * GLA_DV
S_ROWS = GLA_HEADS * GLA_DK
OFF_AQ = 0
OFF_AK = OFF_AQ + ATTN_W
OFF_AV = OFF_AK + KV_W
OFF_AG = OFF_AV + KV_W
OFF_GQ = OFF_AG + ATTN_W
OFF_GK = OFF_GQ + GQK_W
OFF_GV = OFF_GK + GQK_W
OFF_GG = OFF_GV + GV_W
OFF_LR = OFF_GG + GV_W
D_IN = OFF_LR + GLA_RANK
D_IN_PAD = OFF_LR + LANES
CAST_BLOCK = 1024
W_IN_ALLOC = -(-D_IN_PAD // CAST_BLOCK) * CAST_BLOCK

PROMPT_TILE = 256
SAMPLE_SEQS = 8
VMEM_LIMIT = 48 * 1024 * 1024
PROMPT_ORDER = "CPCPCPCPCPPPP"
CHUNK_GROUP = 4

NEG = -1e30
LOG2E = 1.4426950408889634

Slot = collections.namedtuple("Slot", "qs kv ag qin kog kdec dec gv gg")


def _slot_shapes(rows, kv_rows, n_chunks):
    bf, f32 = jnp.bfloat16, jnp.float32
    return ([pltpu.VMEM((rows, 2 * ATTN_W), bf)] + [pltpu.VMEM((kv_rows, LANES), bf) for _ in range(6)]
            + [pltpu.VMEM((rows, ATTN_W), f32), pltpu.VMEM((rows, 2 * GQK_W), bf), pltpu.VMEM((rows, GQK_W), bf),
               pltpu.VMEM((rows, GQK_W), f32), pltpu.VMEM((n_chunks * SUBLANES, GQK_W), f32),
               pltpu.VMEM((rows, GV_W), bf), pltpu.VMEM((rows, GV_W), f32)])


N_SLOT_REFS = 14


def _make_slot(refs):
    return Slot(refs[0], tuple(refs[1:7]), *refs[7:14])


def _dot(a, b):
    return jnp.dot(a, b, preferred_element_type=jnp.float32)


def _dot_nt(a, b):
    return lax.dot_general(a, b, (((1,), (1,)), ((), ())), preferred_element_type=jnp.float32)


def _bf(x):
    return x.astype(jnp.bfloat16)


def _lane_masks():
    lane = lax.broadcasted_iota(jnp.int32, (1, LANES), 1)
    return lane < HEAD_DIM, lane >= HEAD_DIM


def _silu(x):
    return x * (1.0 / (1.0 + jnp.exp(-x)))


def _log_sigmoid_scaled(x, scale):
    soft = jnp.log2(1.0 + jnp.exp2(jnp.abs(x) * -LOG2E))
    return jnp.minimum(x, 0.0) * scale - soft * (scale / LOG2E)


def _rms_scale(x):
    return lax.rsqrt(jnp.mean(x * x, axis=-1, keepdims=True) + NORM_EPS)


def _chunk_cumsum(g, chunk):
    row = lax.broadcasted_iota(jnp.int32, g.shape, 0) & (chunk - 1)
    s = 1
    while s < chunk:
        g = g + jnp.where(row >= s, pltpu.roll(g, s, 0), 0.0)
        s *= 2
    return g


def _rotary(xcol, cos_t, sin_dn, sin_up):
    return (xcol * cos_t + pltpu.roll(xcol, ROT_DIM // 2, 1) * sin_dn
            + pltpu.roll(xcol, LANES - ROT_DIM // 2, 1) * sin_up)


def _rotary_lanes(cos_off, sin_off, cos_base, sin_base):
    d = lax.broadcasted_iota(jnp.int32, (1, LANES), 1) & (HEAD_DIM - 1)
    half = ROT_DIM // 2
    rot, up, dn = d < ROT_DIM, d < half, (d >= half) & (d < ROT_DIM)
    cb, sb = cos_base[0:1, :], sin_base[0:1, :]
    co, so = cos_off[...], sin_off[...]
    cos_t = jnp.where(rot, cb, 0.0) * co - jnp.where(rot, sb, 0.0) * so + jnp.where(rot, 0.0, 1.0)
    sin_dn = jnp.where(dn, sb, 0.0) * co + jnp.where(dn, cb, 0.0) * so
    sin_up = jnp.where(up, -sb, 0.0) * co + jnp.where(up, -cb, 0.0) * so
    return cos_t, sin_dn, sin_up


def _kv_variants(k, v):
    m_lo, m_hi = _lane_masks()
    k_sw = pltpu.roll(k, HEAD_DIM, 1)
    v_sw = pltpu.roll(v, HEAD_DIM, 1)
    return (_bf(k), _bf(k_sw),
            _bf(jnp.where(m_lo, v, 0.0)), _bf(jnp.where(m_hi, v_sw, 0.0)),
            _bf(jnp.where(m_lo, v_sw, 0.0)), _bf(jnp.where(m_hi, v, 0.0)))


def _store_halves(ref, j, width, col):
    m_lo, m_hi = _lane_masks()
    ref[:, j * LANES:(j + 1) * LANES] = _bf(jnp.where(m_lo, col, 0.0))
    ref[:, width + j * LANES:width + (j + 1) * LANES] = _bf(jnp.where(m_hi, col, 0.0))


def _project_stages(x, tabs, wpre_ref, win_ref, wup_ref, bgk_ref, chunk, slot, store_kv):
    rows = x.shape[0]
    h = _bf(x * _rms_scale(x) * wpre_ref[...])
    cos_t, sin_dn, sin_up = _rotary_lanes(*tabs)

    def seg(off, width):
        return _dot(h, win_ref[:, off:off + width])

    yield
    q = seg(OFF_AQ, ATTN_W)
    for j in range(ATTN_W // LANES):
        ln = slice(j * LANES, (j + 1) * LANES)
        _store_halves(slot.qs, j, ATTN_W, _rotary(q[:, ln], cos_t, sin_dn, sin_up) * (LOG2E * HEAD_DIM ** -0.5))
    yield
    kv = seg(OFF_AK, 2 * KV_W)
    store_kv(_rotary(kv[:, :KV_W], cos_t, sin_dn, sin_up), kv[:, KV_W:])
    yield
    slot.ag[...] = _silu(seg(OFF_AG, ATTN_W))
    yield
    lr = _bf(seg(OFF_LR, LANES))
    log_a = _log_sigmoid_scaled(_dot(lr, wup_ref[...]) + bgk_ref[...], 1.0 / GATE_TAU)
    b = _chunk_cumsum(log_a, chunk)
    lasts = [b[r + chunk - 1:r + chunk] for r in range(0, rows, chunk)]
    for i, bl in enumerate(lasts):
        slot.dec[i * SUBLANES:(i + 1) * SUBLANES, :] = jnp.broadcast_to(jnp.exp(bl), (SUBLANES, GQK_W))
    b_last = jnp.concatenate([jnp.broadcast_to(bl, (chunk, GQK_W)) for bl in lasts], axis=0)
    q_in = seg(OFF_GQ, GQK_W) * (GLA_DK ** -0.5) * jnp.exp(b)
    for j in range(GQK_W // LANES):
        _store_halves(slot.qin, j, GQK_W, q_in[:, j * LANES:(j + 1) * LANES])
    yield
    gk = seg(OFF_GK, GQK_W)
    slot.kog[...] = _bf(gk * jnp.exp(-b))
    slot.kdec[...] = gk * jnp.exp(b_last - b)
    yield
    slot.gv[...] = _bf(seg(OFF_GV, GV_W))
    yield
    slot.gg[...] = _silu(seg(OFF_GG, GV_W))


def _attend_stages(slot, rows, kv_win, sinks_ref, n_invalid, emit):
    c = rows.stop - rows.start
    lo = [slot.qs[rows, j * LANES:(j + 1) * LANES] for j in range(ATTN_W // LANES)]
    hi = [slot.qs[rows, ATTN_W + j * LANES:ATTN_W + (j + 1) * LANES] for j in range(ATTN_W // LANES)]
    k_a, k_b = kv_win(0), kv_win(1)
    n_hi = k_a.shape[0] - LANES
    zrows = jnp.zeros((LANES - n_hi, LANES), jnp.bfloat16)
    ext = lambda a: jnp.concatenate([a, zrows], axis=0)
    s_a = _dot_nt(jnp.concatenate([lo[0], lo[1], hi[2], hi[3]], axis=0), ext(k_a))
    s_b = _dot_nt(jnp.concatenate([hi[0], hi[1], lo[2], lo[3]], axis=0), ext(k_b))
    yield
    lane = lax.broadcasted_iota(jnp.int32, (1, LANES), 1)
    where = {0: (s_a, 0), 2: (s_a, 1), 5: (s_a, 2), 7: (s_a, 3),
             1: (s_b, 0), 3: (s_b, 1), 4: (s_b, 2), 6: (s_b, 3)}
    probs, rdenom = {}, {}
    for h in range(N_HEADS):
        src, blk = where[h]
        s = src[blk * c:(blk + 1) * c]
        pad_row = jnp.where(lane == n_hi, sinks_ref[0, h] * LOG2E, NEG)
        s_lo = s[:, :LANES]
        s_hi = jnp.where(lane >= n_hi, pad_row, s[:, LANES:])
        if n_invalid is not None:
            s_lo = jnp.where(lane < n_invalid, NEG, s_lo)
        m = jnp.max(jnp.maximum(s_lo, s_hi), axis=-1, keepdims=True)
        p_lo = jnp.exp2(s_lo - m)
        p_hi = jnp.exp2(s_hi - m)
        rdenom[h] = 1.0 / jnp.sum(p_lo + p_hi, axis=-1, keepdims=True)
        probs[h] = _bf(jnp.concatenate([p_lo, p_hi], axis=1))
    outs = {}
    for g in range(N_KV_HEADS):
        for par in range(2):
            ha, hb = g * GQA_GROUP + par, g * GQA_GROUP + par + 2
            o2 = _dot(jnp.concatenate([probs[ha], probs[hb]], axis=0), ext(kv_win(2 + 2 * g + par)))
            outs[ha] = o2[:c] * rdenom[ha]
            outs[hb] = o2[c:] * rdenom[hb]
    yield
    emit(jnp.concatenate([outs[2 * j] + outs[2 * j + 1] for j in range(4)], axis=1))


def _gla_stages(slot, r0, c, state, gnw_ref, mix_ref):
    rows = slice(r0, r0 + c)
    ci0 = (r0 // c) * SUBLANES
    lhs, a2, uw, dcol, vw = [], [], [], [], []
    for p in range(GLA_HEADS // 2):
        ln = slice(p * LANES, (p + 1) * LANES)
        vwide = slot.gv[rows, p * 2 * GLA_DV:(p + 1) * 2 * GLA_DV]
        lhs.append(jnp.concatenate([slot.qin[rows, ln], slot.qin[rows, GQK_W + p * LANES:GQK_W + (p + 1) * LANES]],
                                   axis=0))
        a2.append(_dot_nt(lhs[p], slot.kog[rows, ln]))
        uw.append(_dot(_bf(slot.kdec[rows, ln].T), vwide))
        dcol.append(jnp.broadcast_to(slot.dec[ci0:ci0 + 1, ln], (LANES, LANES)).T)
        vw.append(vwide)
    yield
    outs = []
    ri = lax.broadcasted_iota(jnp.int32, a2[0].shape, 0) & (c - 1)
    ci = lax.broadcasted_iota(jnp.int32, a2[0].shape, 1)
    for p in range(GLA_HEADS // 2):
        am = _bf(jnp.where(ri >= ci, a2[p], 0.0))
        sb = _bf(state[p])
        for i in range(2):
            hr = slice(i * c, (i + 1) * c)
            outs.append(_dot(jnp.concatenate([lhs[p][hr], am[hr]], axis=1),
                             jnp.concatenate([sb, vw[p][:, i * GLA_DV:(i + 1) * GLA_DV]], axis=0)))
        u_pair = jnp.concatenate([uw[p][:GLA_DK, :GLA_DV], uw[p][GLA_DK:, GLA_DV:]], axis=0)
        state[p] = dcol[p] * state[p] + u_pair
    yield
    gnw = gnw_ref[...]
    for hh, o in enumerate(outs):
        y = o * _rms_scale(o) * gnw * slot.gg[rows, hh * GLA_DV:(hh + 1) * GLA_DV]
        mix_ref[rows, ATTN_W + hh * GLA_DV:ATTN_W + (hh + 1) * GLA_DV] = _bf(y)


def _lockstep(gens):
    live = list(gens)
    while live:
        for g in list(live):
            try:
                next(g)
            except StopIteration:
                live.remove(g)
        yield


def _chunk_stages(slot, chunk_args, c, state_of, sinks_ref, gnw_ref, mix_ref):
    gens = []
    for i, (r0, kv_win, n_invalid) in enumerate(chunk_args):
        rows = slice(r0, r0 + c)

        def emit(attn, rows=rows):
            mix_ref[rows, 0:ATTN_W] = _bf(attn * slot.ag[rows, :])

        gens.append(_attend_stages(slot, rows, kv_win, sinks_ref, n_invalid, emit))
        gens.append(_gla_stages(slot, r0, c, state_of(i), gnw_ref, mix_ref))
    for g in range(0, len(gens), 2 * CHUNK_GROUP):
        yield from _lockstep(gens[g:g + 2 * CHUNK_GROUP])


def _finish(x, mix_ref, wout_ref, wpost_ref):
    mix = _dot(mix_ref[...], wout_ref[...])
    return x + mix * _rms_scale(mix) * wpost_ref[...]


def _run_order(order, p_gen, c_gen):
    for ch in order:
        next(p_gen if ch == "P" else c_gen, None)
    for g in (p_gen, c_gen):
        for _ in g:
            pass


def _prompt_kernel(nt, n_tiles, sinks_ref, xp_ref, xc_ref, coff_ref, soff_ref, cbase_ref, sbase_ref, wpre_ref, win_ref,
                   wup_ref, bgk_ref, gnw_ref, wout_ref, wpost_ref,
                   y_ref, kout_ref, vout_ref, sout_ref, *scratch):
    slots = (_make_slot(scratch[:N_SLOT_REFS]), _make_slot(scratch[N_SLOT_REFS:2 * N_SLOT_REFS]))
    s_ref, mix_ref, woutb_ref = scratch[2 * N_SLOT_REFS:]
    s = pl.program_id(0)
    tt = xp_ref.shape[1]
    t_c = jnp.maximum(s - 1, 0) % nt

    def body(slot_p, slot_c, project=True, chunks=True):
        def store_kv(k, v):
            kout_ref[0] = k[tt - WINDOW:]
            vout_ref[0] = v[tt - WINDOW:]
            for r, rc, val in zip(slot_p.kv, slot_c.kv, _kv_variants(k, v)):
                r[WINDOW:WINDOW + tt, :] = val
                r[0:WINDOW, :] = rc[tt:tt + WINDOW, :]

        p_gen = (_project_stages(xp_ref[0], (coff_ref, soff_ref, cbase_ref, sbase_ref), wpre_ref, win_ref, wup_ref,
                                 bgk_ref, CHUNK, slot_p, store_kv) if project else iter(()))

        def c_stages():
            keep = t_c != 0
            state = [jnp.where(keep, s_ref[p * LANES:(p + 1) * LANES, :], 0.0) for p in range(GLA_HEADS // 2)]
            win = WINDOW + CHUNK
            chunk_args = []
            for r0 in range(0, tt, CHUNK):
                n_invalid = WINDOW - (t_c * tt + r0) if r0 < WINDOW else None
                chunk_args.append((r0, lambda i, r0=r0: slot_c.kv[i][r0:r0 + win, :], n_invalid))
            yield from _chunk_stages(slot_c, chunk_args, CHUNK, lambda i: state, sinks_ref, gnw_ref, mix_ref)
            y_ref[0] = _finish(xc_ref[0], mix_ref, woutb_ref, wpost_ref)
            for p in range(GLA_HEADS // 2):
                s_ref[p * LANES:(p + 1) * LANES, :] = state[p]
                sout_ref[0, p * LANES:(p + 1) * LANES, :] = state[p]

        _run_order(PROMPT_ORDER if project and chunks else "", p_gen, c_stages() if chunks else iter(()))

    @pl.when(s == 0)
    def _():
        for r in slots[1].kv + (s_ref,):
            r[...] = jnp.zeros_like(r)
        woutb_ref[...] = _bf(wout_ref[...])
        body(slots[0], slots[1], chunks=False)

    @pl.when(s == n_tiles)
    def _():
        body(slots[n_tiles % 2], slots[1 - n_tiles % 2], project=False)

    middle = (s > 0) & (s < n_tiles)

    @pl.when(middle & (s % 2 == 0))
    def _():
        body(slots[0], slots[1])

    @pl.when(middle & (s % 2 == 1))
    def _():
        body(slots[1], slots[0])


def _sample_kernel(sinks_ref, x_ref, coff_ref, soff_ref, cbase_ref, sbase_ref, ck_ref, cv_ref, s0_ref, wpre_ref,
                   win_ref, wup_ref, bgk_ref, gnw_ref, wout_ref, wpost_ref,
                   y_ref, kout_ref, vout_ref, sout_ref, *scratch):
    slot = _make_slot(scratch[:N_SLOT_REFS])
    mix_ref, woutb_ref = scratch[N_SLOT_REFS:]
    nb = ck_ref.shape[0]

    @pl.when(pl.program_id(0) == 0)
    def _():
        woutb_ref[...] = _bf(wout_ref[...])

    c = x_ref.shape[0] // nb
    win = WINDOW + c

    def store_kv(k, v):
        kout_ref[...] = k
        vout_ref[...] = v
        for i in range(nb):
            rows = slice(i * c, (i + 1) * c)
            kv = _kv_variants(jnp.concatenate([ck_ref[i], k[rows]], axis=0),
                              jnp.concatenate([cv_ref[i], v[rows]], axis=0))
            for r, val in zip(slot.kv, kv):
                r[i * win:(i + 1) * win, :] = val

    x = x_ref[...]
    for _ in _project_stages(x, (coff_ref, soff_ref, cbase_ref, sbase_ref), wpre_ref, win_ref, wup_ref, bgk_ref, c, slot,
                             store_kv):
        pass
    states = [[s0_ref[i, p * LANES:(p + 1) * LANES, :] for p in range(GLA_HEADS // 2)] for i in range(nb)]
    chunk_args = [(i * c, lambda j, i=i: slot.kv[j][i * win:(i + 1) * win, :], None) for i in range(nb)]
    for _ in _chunk_stages(slot, chunk_args, c, lambda i: states[i], sinks_ref, gnw_ref, mix_ref):
        pass
    for i in range(nb):
        for p in range(GLA_HEADS // 2):
            sout_ref[i, p * LANES:(p + 1) * LANES, :] = states[i][p]
    y_ref[...] = _finish(x, mix_ref, woutb_ref, wpost_ref)


def _rotary_angles(offsets, bases):
    half = ROT_DIM // 2
    d = jnp.arange(LANES) % HEAD_DIM
    inv = ROPE_THETA ** (-(d % half).astype(jnp.float32) * (2.0 / ROT_DIM))
    pos = jnp.concatenate([offsets, jnp.repeat(bases, SUBLANES)]).astype(jnp.float32)
    ang = pos[:, None] * inv[None, :]
    return jnp.cos(ang), jnp.sin(ang)


def _cast_w_in_kernel(wt_ref, out_ref):
    row = lax.broadcasted_iota(jnp.int32, wt_ref.shape, 0) + pl.program_id(0) * CAST_BLOCK
    out_ref[...] = _bf(jnp.where(row < D_IN, wt_ref[...], 0.0).T)


def _cast_w_in(w_in_t):
    return pl.pallas_call(
        _cast_w_in_kernel,
        grid=(W_IN_ALLOC // CAST_BLOCK,),
        in_specs=[pl.BlockSpec((CAST_BLOCK, D_MODEL), lambda i: (i, 0))],
        out_specs=pl.BlockSpec((D_MODEL, CAST_BLOCK), lambda i: (0, i)),
        out_shape=jax.ShapeDtypeStruct((D_MODEL, W_IN_ALLOC), jnp.bfloat16),
        compiler_params=pltpu.CompilerParams(dimension_semantics=("arbitrary",)),
        name="cast_w_in",
    )(w_in_t)


def _const_spec(shape):
    zeros = (0,) * len(shape)
    return pl.BlockSpec(shape, lambda i: zeros, pipeline_mode=pl.Buffered(1))


def kernel(x_prompt, x_sample, cache_k, cache_v, state_gla, norm_pre_w, w_in, attn_sinks, w_gk_up, b_gk,
           gla_norm_w, w_out, norm_post_w):
    bsz, t_p, _ = x_prompt.shape
    dec_b, t_s, _ = x_sample.shape
    assert w_in.shape[0] == 1, "single layer"
    assert t_p % PROMPT_TILE == 0 and PROMPT_TILE % CHUNK == 0 and PROMPT_TILE >= WINDOW
    assert dec_b % SAMPLE_SEQS == 0 and t_s & (t_s - 1) == 0

    win = _cast_w_in(w_in[0].T)
    wout = w_out[0]
    wup = _bf(jnp.pad(w_gk_up[0], ((0, LANES - GLA_RANK), (0, 0))))
    wpre = norm_pre_w[0][None, :]
    wpost = norm_post_w[0][None, :]
    bgk = b_gk[0][None, :]
    gnw = gla_norm_w[0][None, :]
    sinks = attn_sinks[0][None, :]
    smem = pl.BlockSpec(memory_space=pltpu.SMEM)

    weights_specs = [
        _const_spec((1, D_MODEL)), _const_spec((D_MODEL, D_IN_PAD)), _const_spec((LANES, GQK_W)),
        _const_spec((1, GQK_W)), _const_spec((1, GLA_DV)), _const_spec((D_MODEL, D_MODEL)),
        _const_spec((1, D_MODEL))]

    tt = PROMPT_TILE
    nt = t_p // tt
    n_tiles = bsz * nt
    nb = SAMPLE_SEQS
    rows = nb * t_s
    assert tt % rows == 0 and (tt + rows) % SUBLANES == 0
    cos_all, sin_all = _rotary_angles(jnp.concatenate([jnp.arange(tt), jnp.tile(jnp.arange(t_s), nb)]),
                                      jnp.concatenate([jnp.arange(nt) * tt, jnp.array([PAST_LEN])]))
    base_blk = (tt + rows) // SUBLANES

    def tile_p(s):
        return jnp.minimum(s, n_tiles - 1)

    def tile_c(s):
        return jnp.maximum(s - 1, 0)

    off_spec = pl.BlockSpec((tt, LANES), lambda s: (0, 0))
    base_spec = pl.BlockSpec((SUBLANES, LANES), lambda s: (base_blk + tile_p(s) % nt, 0))
    x_spec = lambda tile: pl.BlockSpec((1, tt, D_MODEL), lambda s: (tile(s) // nt, tile(s) % nt, 0))
    seq_spec = lambda tile, d1, d2: pl.BlockSpec((1, d1, d2), lambda s: (tile(s) // nt, 0, 0))
    y_p, k_p, v_p, s_p = pl.pallas_call(
        functools.partial(_prompt_kernel, nt, n_tiles),
        grid=(n_tiles + 1,),
        in_specs=[smem, x_spec(tile_p), x_spec(tile_c), off_spec, off_spec, base_spec, base_spec] + weights_specs,
        out_specs=[x_spec(tile_c), seq_spec(tile_p, WINDOW, KV_W), seq_spec(tile_p, WINDOW, KV_W),
                   seq_spec(tile_c, S_ROWS, GLA_DV)],
        out_shape=[jax.ShapeDtypeStruct((bsz, t_p, D_MODEL), jnp.float32),
                   jax.ShapeDtypeStruct((bsz, WINDOW, KV_W), jnp.float32),
                   jax.ShapeDtypeStruct((bsz, WINDOW, KV_W), jnp.float32),
                   jax.ShapeDtypeStruct((bsz, S_ROWS, GLA_DV), jnp.float32)],
        scratch_shapes=_slot_shapes(tt, WINDOW + tt, tt // CHUNK) * 2
        + [pltpu.VMEM((S_ROWS, GLA_DV), jnp.float32), pltpu.VMEM((tt, D_MODEL), jnp.bfloat16),
           pltpu.VMEM((D_MODEL, D_MODEL), jnp.bfloat16)],
        compiler_params=pltpu.CompilerParams(dimension_semantics=("arbitrary",), vmem_limit_bytes=VMEM_LIMIT),
        name="prompt_layer",
    )(sinks, x_prompt, x_prompt, cos_all, sin_all, cos_all, sin_all, wpre, win, wup, bgk, gnw, wout, wpost)

    off_spec = pl.BlockSpec((rows, LANES), lambda i: (tt // rows, 0))
    base_spec = pl.BlockSpec((SUBLANES, LANES), lambda i: (base_blk + nt, 0))
    seq3 = lambda d1, d2: pl.BlockSpec((nb, d1, d2), lambda i: (i, 0, 0))
    tok = lambda d: pl.BlockSpec((rows, d), lambda i: (i, 0))
    y_s, k_s, v_s, s_s = pl.pallas_call(
        _sample_kernel,
        grid=(dec_b // nb,),
        in_specs=[smem, tok(D_MODEL), off_spec, off_spec, base_spec, base_spec, seq3(WINDOW, KV_W), seq3(WINDOW, KV_W),
                  seq3(S_ROWS, GLA_DV)] + weights_specs,
        out_specs=[tok(D_MODEL), tok(KV_W), tok(KV_W), seq3(S_ROWS, GLA_DV)],
        out_shape=[jax.ShapeDtypeStruct((dec_b * t_s, D_MODEL), jnp.float32),
                   jax.ShapeDtypeStruct((dec_b * t_s, KV_W), jnp.float32),
                   jax.ShapeDtypeStruct((dec_b * t_s, KV_W), jnp.float32),
                   jax.ShapeDtypeStruct((dec_b, S_ROWS, GLA_DV), jnp.float32)],
        scratch_shapes=_slot_shapes(rows, nb * (WINDOW + t_s), nb)
        + [pltpu.VMEM((rows, D_MODEL), jnp.bfloat16), pltpu.VMEM((D_MODEL, D_MODEL), jnp.bfloat16)],
        compiler_params=pltpu.CompilerParams(dimension_semantics=("arbitrary",), vmem_limit_bytes=VMEM_LIMIT),
        name="sample_layer",
    )(sinks, x_sample.reshape(dec_b * t_s, D_MODEL), cos_all, sin_all, cos_all, sin_all,
      cache_k[0].reshape(dec_b, WINDOW, KV_W), cache_v[0].reshape(dec_b, WINDOW, KV_W),
      state_gla[0].reshape(dec_b, S_ROWS, GLA_DV), wpre, win, wup, bgk, gnw, wout, wpost)

    kv5 = lambda a, b, t: a.reshape(1, b, t, N_KV_HEADS, HEAD_DIM)
    st5 = lambda a, b: a.reshape(1, b, GLA_HEADS, GLA_DK, GLA_DV)
    return (y_p, y_s.reshape(dec_b, t_s, D_MODEL),
            kv5(k_p, bsz, WINDOW), kv5(v_p, bsz, WINDOW), st5(s_p, bsz),
            kv5(k_s, dec_b, t_s), kv5(v_s, dec_b, t_s), st5(s_s, dec_b))
```

```python
import collections
import functools

import jax
import jax.numpy as jnp
from jax import lax
from jax.experimental import pallas as pl
from jax.experimental.pallas import tpu as pltpu

D_MODEL = 1024
CHUNK = 64
WINDOW = 128
HEAD_DIM = 64
N_HEADS = 8
N_KV_HEADS = 2
GQA_GROUP = N_HEADS // N_KV_HEADS
ROT_DIM = 16
ROPE_THETA = 500000.0
GLA_HEADS = 4
GLA_DV = 128
GLA_DK = 64
GLA_RANK = 16
GATE_TAU = 16.0
NORM_EPS = 1e-6
PAST_LEN = 4096

LANES = 128
SUBLANES = 8
ATTN_W = N_HEADS * HEAD_DIM
KV_W = N_KV_HEADS * HEAD_DIM
GQK_W = GLA_HEADS * GLA_DK
GV_W = GLA_HEADS * GLA_DV
S_ROWS = GLA_HEADS * GLA_DK
OFF_AQ = 0
OFF_AK = OFF_AQ + ATTN_W
OFF_AV = OFF_AK + KV_W
OFF_AG = OFF_AV + KV_W
OFF_GQ = OFF_AG + ATTN_W
OFF_GK = OFF_GQ + GQK_W
OFF_GV = OFF_GK + GQK_W
OFF_GG = OFF_GV + GV_W
OFF_LR = OFF_GG + GV_W
D_IN = OFF_LR + GLA_RANK
D_IN_PAD = OFF_LR + LANES
CAST_BLOCK = 1024
W_IN_ALLOC = -(-D_IN_PAD // CAST_BLOCK) * CAST_BLOCK

PROMPT_TILE = 256
SAMPLE_SEQS = 8
VMEM_LIMIT = 48 * 1024 * 1024
STEP_ORDER = "CPCPCPCPCPPPP"
CHUNK_GROUP = 4

NEG = -1e30
LOG2E = 1.4426950408889634

Slot = collections.namedtuple("Slot", "qs kv ag qin kog kdec dec gv gg")


def _slot_shapes(rows, kv_rows, n_chunks):
    bf, f32 = jnp.bfloat16, jnp.float32
    return ([pltpu.VMEM((rows, 2 * ATTN_W), bf)] + [pltpu.VMEM((kv_rows, LANES), bf) for _ in range(6)]
            + [pltpu.VMEM((rows, ATTN_W), f32), pltpu.VMEM((rows, 2 * GQK_W), bf), pltpu.VMEM((rows, GQK_W), bf),
               pltpu.VMEM((rows, GQK_W), f32), pltpu.VMEM((n_chunks * SUBLANES, GQK_W), f32),
               pltpu.VMEM((rows, GV_W), bf), pltpu.VMEM((rows, GV_W), f32)])


N_SLOT_REFS = 14


def _make_slot(refs):
    return Slot(refs[0], tuple(refs[1:7]), *refs[7:14])


def _dot(a, b):
    return jnp.dot(a, b, preferred_element_type=jnp.float32)


def _dot_nt(a, b):
    return lax.dot_general(a, b, (((1,), (1,)), ((), ())), preferred_element_type=jnp.float32)


def _bf(x):
    return x.astype(jnp.bfloat16)


def _lane_masks():
    lane = lax.broadcasted_iota(jnp.int32, (1, LANES), 1)
    return lane < HEAD_DIM, lane >= HEAD_DIM


def _silu(x):
    return x * (1.0 / (1.0 + jnp.exp(-x)))


def _log_sigmoid_scaled(x, scale):
    soft = jnp.log2(1.0 + jnp.exp2(jnp.abs(x) * -LOG2E))
    return jnp.minimum(x, 0.0) * scale - soft * (scale / LOG2E)


def _rms_scale(x):
    return lax.rsqrt(jnp.mean(x * x, axis=-1, keepdims=True) + NORM_EPS)


def _chunk_cumsum(g, chunk):
    row = lax.broadcasted_iota(jnp.int32, g.shape, 0) & (chunk - 1)
    s = 1
    while s < chunk:
        g = g + jnp.where(row >= s, pltpu.roll(g, s, 0), 0.0)
        s *= 2
    return g


def _rotary(xcol, cos_t, sin_dn, sin_up):
    return (xcol * cos_t + pltpu.roll(xcol, ROT_DIM // 2, 1) * sin_dn
            + pltpu.roll(xcol, LANES - ROT_DIM // 2, 1) * sin_up)


def _rotary_lanes(cos_off, sin_off, cos_base, sin_base):
    d = lax.broadcasted_iota(jnp.int32, (1, LANES), 1) & (HEAD_DIM - 1)
    half = ROT_DIM // 2
    rot, up, dn = d < ROT_DIM, d < half, (d >= half) & (d < ROT_DIM)
    cb, sb = cos_base[0:1, :], sin_base[0:1, :]
    co, so = cos_off[...], sin_off[...]
    cos_t = jnp.where(rot, cb, 0.0) * co - jnp.where(rot, sb, 0.0) * so + jnp.where(rot, 0.0, 1.0)
    sin_dn = jnp.where(dn, sb, 0.0) * co + jnp.where(dn, cb, 0.0) * so
    sin_up = jnp.where(up, -sb, 0.0) * co + jnp.where(up, -cb, 0.0) * so
    return cos_t, sin_dn, sin_up


def _kv_variants(k, v):
    m_lo, m_hi = _lane_masks()
    k_sw = pltpu.roll(k, HEAD_DIM, 1)
    v_sw = pltpu.roll(v, HEAD_DIM, 1)
    return (_bf(k), _bf(k_sw),
            _bf(jnp.where(m_lo, v, 0.0)), _bf(jnp.where(m_hi, v_sw, 0.0)),
            _bf(jnp.where(m_lo, v_sw, 0.0)), _bf(jnp.where(m_hi, v, 0.0)))


def _store_halves(ref, j, width, col):
    m_lo, m_hi = _lane_masks()
    ref[:, j * LANES:(j + 1) * LANES] = _bf(jnp.where(m_lo, col, 0.0))
    ref[:, width + j * LANES:width + (j + 1) * LANES] = _bf(jnp.where(m_hi, col, 0.0))


def _project_stages(x, tabs, wpre_ref, win_ref, wup_ref, bgk_ref, chunk, slot, store_kv):
    rows = x.shape[0]
    h = _bf(x * _rms_scale(x) * wpre_ref[...])
    cos_t, sin_dn, sin_up = _rotary_lanes(*tabs)

    def seg(off, width):
        return _dot(h, win_ref[:, off:off + width])

    yield
    q = seg(OFF_AQ, ATTN_W)
    for j in range(ATTN_W // LANES):
        ln = slice(j * LANES, (j + 1) * LANES)
        _store_halves(slot.qs, j, ATTN_W, _rotary(q[:, ln], cos_t, sin_dn, sin_up) * (LOG2E * HEAD_DIM ** -0.5))
    yield
    kv = seg(OFF_AK, 2 * KV_W)
    store_kv(_rotary(kv[:, :KV_W], cos_t, sin_dn, sin_up), kv[:, KV_W:])
    yield
    slot.ag[...] = _silu(seg(OFF_AG, ATTN_W))
    yield
    lr = _bf(seg(OFF_LR, LANES))
    log_a = _log_sigmoid_scaled(_dot(lr, wup_ref[...]) + bgk_ref[...], 1.0 / GATE_TAU)
    b = _chunk_cumsum(log_a, chunk)
    lasts = [b[r + chunk - 1:r + chunk] for r in range(0, rows, chunk)]
    for i, bl in enumerate(lasts):
        slot.dec[i * SUBLANES:(i + 1) * SUBLANES, :] = jnp.broadcast_to(jnp.exp(bl), (SUBLANES, GQK_W))
    b_last = jnp.concatenate([jnp.broadcast_to(bl, (chunk, GQK_W)) for bl in lasts], axis=0)
    q_in = seg(OFF_GQ, GQK_W) * (GLA_DK ** -0.5) * jnp.exp(b)
    for j in range(GQK_W // LANES):
        _store_halves(slot.qin, j, GQK_W, q_in[:, j * LANES:(j + 1) * LANES])
    yield
    gk = seg(OFF_GK, GQK_W)
    slot.kog[...] = _bf(gk * jnp.exp(-b))
    slot.kdec[...] = gk * jnp.exp(b_last - b)
    yield
    slot.gv[...] = _bf(seg(OFF_GV, GV_W))
    yield
    slot.gg[...] = _silu(seg(OFF_GG, GV_W))


def _attend_stages(slot, rows, kv_win, sinks_ref, n_invalid, emit):
    c = rows.stop - rows.start
    lo = [slot.qs[rows, j * LANES:(j + 1) * LANES] for j in range(ATTN_W // LANES)]
    hi = [slot.qs[rows, ATTN_W + j * LANES:ATTN_W + (j + 1) * LANES] for j in range(ATTN_W // LANES)]
    k_a, k_b = kv_win(0), kv_win(1)
    n_hi = k_a.shape[0] - LANES
    zrows = jnp.zeros((LANES - n_hi, LANES), jnp.bfloat16)
    ext = lambda a: jnp.concatenate([a, zrows], axis=0)
    s_a = _dot_nt(jnp.concatenate([lo[0], lo[1], hi[2], hi[3]], axis=0), ext(k_a))
    s_b = _dot_nt(jnp.concatenate([hi[0], hi[1], lo[2], lo[3]], axis=0), ext(k_b))
    yield
    lane = lax.broadcasted_iota(jnp.int32, (1, LANES), 1)
    where = {0: (s_a, 0), 2: (s_a, 1), 5: (s_a, 2), 7: (s_a, 3),
             1: (s_b, 0), 3: (s_b, 1), 4: (s_b, 2), 6: (s_b, 3)}
    probs, rdenom = {}, {}
    for h in range(N_HEADS):
        src, blk = where[h]
        s = src[blk * c:(blk + 1) * c]
        pad_row = jnp.where(lane == n_hi, sinks_ref[0, h] * LOG2E, NEG)
        s_lo = s[:, :LANES]
        s_hi = jnp.where(lane >= n_hi, pad_row, s[:, LANES:])
        if n_invalid is not None:
            s_lo = jnp.where(lane < n_invalid, NEG, s_lo)
        m = jnp.max(jnp.maximum(s_lo, s_hi), axis=-1, keepdims=True)
        p_lo = jnp.exp2(s_lo - m)
        p_hi = jnp.exp2(s_hi - m)
        rdenom[h] = 1.0 / jnp.sum(p_lo + p_hi, axis=-1, keepdims=True)
        probs[h] = _bf(jnp.concatenate([p_lo, p_hi], axis=1))
    outs = {}
    for g in range(N_KV_HEADS):
        for par in range(2):
            ha, hb = g * GQA_GROUP + par, g * GQA_GROUP + par + 2
            o2 = _dot(jnp.concatenate([probs[ha], probs[hb]], axis=0), ext(kv_win(2 + 2 * g + par)))
            outs[ha] = o2[:c] * rdenom[ha]
            outs[hb] = o2[c:] * rdenom[hb]
    yield
    emit(jnp.concatenate([outs[2 * j] + outs[2 * j + 1] for j in range(4)], axis=1))


def _gla_stages(slot, r0, c, state, gnw_ref, mix_ref):
    rows = slice(r0, r0 + c)
    ci0 = (r0 // c) * SUBLANES
    lhs, a2, uw, dcol, vw = [], [], [], [], []
    for p in range(GLA_HEADS // 2):
        ln = slice(p * LANES, (p + 1) * LANES)
        vwide = slot.gv[rows, p * 2 * GLA_DV:(p + 1) * 2 * GLA_DV]
        lhs.append(jnp.concatenate([slot.qin[rows, ln], slot.qin[rows, GQK_W + p * LANES:GQK_W + (p + 1) * LANES]],
                                   axis=0))
        a2.append(_dot_nt(lhs[p], slot.kog[rows, ln]))
        uw.append(_dot(_bf(slot.kdec[rows, ln].T), vwide))
        dcol.append(jnp.broadcast_to(slot.dec[ci0:ci0 + 1, ln], (LANES, LANES)).T)
        vw.append(vwide)
    yield
    outs = []
    ri = lax.broadcasted_iota(jnp.int32, a2[0].shape, 0) & (c - 1)
    ci = lax.broadcasted_iota(jnp.int32, a2[0].shape, 1)
    for p in range(GLA_HEADS // 2):
        am = _bf(jnp.where(ri >= ci, a2[p], 0.0))
        sb = _bf(state[p])
        for i in range(2):
            hr = slice(i * c, (i + 1) * c)
            outs.append(_dot(jnp.concatenate([lhs[p][hr], am[hr]], axis=1),
                             jnp.concatenate([sb, vw[p][:, i * GLA_DV:(i + 1) * GLA_DV]], axis=0)))
        u_pair = jnp.concatenate([uw[p][:GLA_DK, :GLA_DV], uw[p][GLA_DK:, GLA_DV:]], axis=0)
        state[p] = dcol[p] * state[p] + u_pair
    yield
    gnw = gnw_ref[...]
    for hh, o in enumerate(outs):
        y = o * _rms_scale(o) * gnw * slot.gg[rows, hh * GLA_DV:(hh + 1) * GLA_DV]
        mix_ref[rows, ATTN_W + hh * GLA_DV:ATTN_W + (hh + 1) * GLA_DV] = _bf(y)


def _lockstep(gens):
    live = list(gens)
    while live:
        for g in list(live):
            try:
                next(g)
            except StopIteration:
                live.remove(g)
        yield


def _chunk_stages(slot, chunk_args, c, state_of, sinks_ref, gnw_ref, mix_ref):
    gens = []
    for i, (r0, kv_win, n_invalid) in enumerate(chunk_args):
        rows = slice(r0, r0 + c)

        def emit(attn, rows=rows):
            mix_ref[rows, 0:ATTN_W] = _bf(attn * slot.ag[rows, :])

        gens.append(_attend_stages(slot, rows, kv_win, sinks_ref, n_invalid, emit))
        gens.append(_gla_stages(slot, r0, c, state_of(i), gnw_ref, mix_ref))
    for g in range(0, len(gens), 2 * CHUNK_GROUP):
        yield from _lockstep(gens[g:g + 2 * CHUNK_GROUP])


def _finish(x, mix_ref, wout_ref, wpost_ref):
    mix = _dot(mix_ref[...], wout_ref[...])
    return x + mix * _rms_scale(mix) * wpost_ref[...]


def _run_order(order, p_gen, c_gen):
    for ch in order:
        next(p_gen if ch == "P" else c_gen, None)
    for g in (p_gen, c_gen):
        for _ in g:
            pass


def _pipeline_step(n_tiles, slots, body, first_step_init):
    s = pl.program_id(0)

    @pl.when(s == 0)
    def _():
        first_step_init()
        body(slots[0], slots[1], True, False)

    @pl.when(s == n_tiles)
    def _():
        body(slots[n_tiles % 2], slots[1 - n_tiles % 2], False, True)

    middle = (s > 0) & (s < n_tiles)

    @pl.when(middle & (s % 2 == 0))
    def _():
        body(slots[0], slots[1], True, True)

    @pl.when(middle & (s % 2 == 1))
    def _():
        body(slots[1], slots[0], True, True)


def _prompt_kernel(nt, n_tiles, sinks_ref, xp_ref, coff_ref, soff_ref, cbase_ref, sbase_ref, wpre_ref, win_ref,
                   wup_ref, bgk_ref, gnw_ref, wout_ref, wpost_ref,
                   y_ref, kout_ref, vout_ref, sout_ref, *scratch):
    slots = (_make_slot(scratch[:N_SLOT_REFS]), _make_slot(scratch[N_SLOT_REFS:2 * N_SLOT_REFS]))
    s_ref, mix_ref, woutb_ref, x_keep0, x_keep1 = scratch[2 * N_SLOT_REFS:]
    x_keep = {id(slots[0]): x_keep0, id(slots[1]): x_keep1}
    tt = xp_ref.shape[1]
    t_c = jnp.maximum(pl.program_id(0) - 1, 0) % nt

    def body(slot_p, slot_c, project, chunks):
        def store_kv(k, v):
            kout_ref[0] = k[tt - WINDOW:]
            vout_ref[0] = v[tt - WINDOW:]
            for r, rc, val in zip(slot_p.kv, slot_c.kv, _kv_variants(k, v)):
                r[WINDOW:WINDOW + tt, :] = val
                r[0:WINDOW, :] = rc[tt:tt + WINDOW, :]

        if project:
            x_keep[id(slot_p)][...] = xp_ref[0]
        p_gen = (_project_stages(xp_ref[0], (coff_ref, soff_ref, cbase_ref, sbase_ref), wpre_ref, win_ref, wup_ref,
                                 bgk_ref, CHUNK, slot_p, store_kv) if project else iter(()))

        def c_stages():
            keep = t_c != 0
            state = [jnp.where(keep, s_ref[p * LANES:(p + 1) * LANES, :], 0.0) for p in range(GLA_HEADS // 2)]
            win = WINDOW + CHUNK
            chunk_args = []
            for r0 in range(0, tt, CHUNK):
                n_invalid = WINDOW - (t_c * tt + r0) if r0 < WINDOW else None
                chunk_args.append((r0, lambda i, r0=r0: slot_c.kv[i][r0:r0 + win, :], n_invalid))
            yield from _chunk_stages(slot_c, chunk_args, CHUNK, lambda i: state, sinks_ref, gnw_ref, mix_ref)
            y_ref[0] = _finish(x_keep[id(slot_c)][...], mix_ref, woutb_ref, wpost_ref)
            for p in range(GLA_HEADS // 2):
                s_ref[p * LANES:(p + 1) * LANES, :] = state[p]
                sout_ref[0, p * LANES:(p + 1) * LANES, :] = state[p]

        _run_order(STEP_ORDER if project and chunks else "", p_gen, c_stages() if chunks else iter(()))

    def first_step_init():
        for r in slots[1].kv + (s_ref,):
            r[...] = jnp.zeros_like(r)
        woutb_ref[...] = _bf(wout_ref[...])

    _pipeline_step(n_tiles, slots, body, first_step_init)


def _sample_kernel(sinks_ref, x_ref, coff_ref, soff_ref, cbase_ref, sbase_ref, ck_ref, cv_ref, s0_ref, wpre_ref,
                   win_ref, wup_ref, bgk_ref, gnw_ref, wout_ref, wpost_ref,
                   y_ref, kout_ref, vout_ref, sout_ref, *scratch):
    slot = _make_slot(scratch[:N_SLOT_REFS])
    mix_ref, woutb_ref = scratch[N_SLOT_REFS:]
    nb = ck_ref.shape[0]

    @pl.when(pl.program_id(0) == 0)
    def _():
        woutb_ref[...] = _bf(wout_ref[...])

    c = x_ref.shape[0] // nb
    win = WINDOW + c

    def store_kv(k, v):
        kout_ref[...] = k
        vout_ref[...] = v
        for i in range(nb):
            rows = slice(i * c, (i + 1) * c)
            kv = _kv_variants(jnp.concatenate([ck_ref[i], k[rows]], axis=0),
                              jnp.concatenate([cv_ref[i], v[rows]], axis=0))
            for r, val in zip(slot.kv, kv):
                r[i * win:(i + 1) * win, :] = val

    x = x_ref[...]
    for _ in _project_stages(x, (coff_ref, soff_ref, cbase_ref, sbase_ref), wpre_ref, win_ref, wup_ref, bgk_ref, c, slot,
                             store_kv):
        pass
    states = [[s0_ref[i, p * LANES:(p + 1) * LANES, :] for p in range(GLA_HEADS // 2)] for i in range(nb)]
    chunk_args = [(i * c, lambda j, i=i: slot.kv[j][i * win:(i + 1) * win, :], None) for i in range(nb)]
    for _ in _chunk_stages(slot, chunk_args, c, lambda i: states[i], sinks_ref, gnw_ref, mix_ref):
        pass
    for i in range(nb):
        for p in range(GLA_HEADS // 2):
            sout_ref[i, p * LANES:(p + 1) * LANES, :] = states[i][p]
    y_ref[...] = _finish(x, mix_ref, woutb_ref, wpost_ref)


def _rotary_angles(offsets, bases):
    half = ROT_DIM // 2
    d = jnp.arange(LANES) % HEAD_DIM
    inv = ROPE_THETA ** (-(d % half).astype(jnp.float32) * (2.0 / ROT_DIM))
    pos = jnp.concatenate([offsets, jnp.repeat(bases, SUBLANES)]).astype(jnp.float32)
    ang = pos[:, None] * inv[None, :]
    return jnp.cos(ang), jnp.sin(ang)


def _cast_w_in_kernel(wt_ref, out_ref):
    row = lax.broadcasted_iota(jnp.int32, wt_ref.shape, 0) + pl.program_id(0) * CAST_BLOCK
    out_ref[...] = _bf(jnp.where(row < D_IN, wt_ref[...], 0.0).T)


def _cast_w_in(w_in_t):
    return pl.pallas_call(
        _cast_w_in_kernel,
        grid=(W_IN_ALLOC // CAST_BLOCK,),
        in_specs=[pl.BlockSpec((CAST_BLOCK, D_MODEL), lambda i: (i, 0))],
        out_specs=pl.BlockSpec((D_MODEL, CAST_BLOCK), lambda i: (0, i)),
        out_shape=jax.ShapeDtypeStruct((D_MODEL, W_IN_ALLOC), jnp.bfloat16),
        compiler_params=pltpu.CompilerParams(dimension_semantics=("arbitrary",)),
        name="cast_w_in",
    )(w_in_t)


def _const_spec(shape):
    zeros = (0,) * len(shape)
    return pl.BlockSpec(shape, lambda i: zeros, pipeline_mode=pl.Buffered(1))


def kernel(x_prompt, x_sample, cache_k, cache_v, state_gla, norm_pre_w, w_in, attn_sinks, w_gk_up, b_gk,
           gla_norm_w, w_out, norm_post_w):
    bsz, t_p, _ = x_prompt.shape
    dec_b, t_s, _ = x_sample.shape
    assert w_in.shape[0] == 1, "single layer"
    assert t_p % PROMPT_TILE == 0 and PROMPT_TILE % CHUNK == 0 and PROMPT_TILE >= WINDOW
    assert dec_b % SAMPLE_SEQS == 0 and t_s & (t_s - 1) == 0

    win = _cast_w_in(w_in[0].T)
    wout = w_out[0]
    wup = _bf(jnp.pad(w_gk_up[0], ((0, LANES - GLA_RANK), (0, 0))))
    wpre = norm_pre_w[0][None, :]
    wpost = norm_post_w[0][None, :]
    bgk = b_gk[0][None, :]
    gnw = gla_norm_w[0][None, :]
    sinks = attn_sinks[0][None, :]
    smem = pl.BlockSpec(memory_space=pltpu.SMEM)

    weights_specs = [
        _const_spec((1, D_MODEL)), _const_spec((D_MODEL, D_IN_PAD)), _const_spec((LANES, GQK_W)),
        _const_spec((1, GQK_W)), _const_spec((1, GLA_DV)), _const_spec((D_MODEL, D_MODEL)),
        _const_spec((1, D_MODEL))]

    tt = PROMPT_TILE
    nt = t_p // tt
    n_tiles = bsz * nt
    nb = SAMPLE_SEQS
    rows = nb * t_s
    assert tt % rows == 0 and (tt + rows) % SUBLANES == 0
    cos_all, sin_all = _rotary_angles(jnp.concatenate([jnp.arange(tt), jnp.tile(jnp.arange(t_s), nb)]),
                                      jnp.concatenate([jnp.arange(nt) * tt, jnp.array([PAST_LEN])]))
    base_blk = (tt + rows) // SUBLANES

    def tile_p(s):
        return jnp.minimum(s, n_tiles - 1)

    def tile_c(s):
        return jnp.maximum(s - 1, 0)

    off_spec = pl.BlockSpec((tt, LANES), lambda s: (0, 0))
    base_spec = pl.BlockSpec((SUBLANES, LANES), lambda s: (base_blk + tile_p(s) % nt, 0))
    x_spec = lambda tile: pl.BlockSpec((1, tt, D_MODEL), lambda s: (tile(s) // nt, tile(s) % nt, 0))
    seq_spec = lambda tile, d1, d2: pl.BlockSpec((1, d1, d2), lambda s: (tile(s) // nt, 0, 0))
    y_p, k_p, v_p, s_p = pl.pallas_call(
        functools.partial(_prompt_kernel, nt, n_tiles),
        grid=(n_tiles + 1,),
        in_specs=[smem, x_spec(tile_p), off_spec, off_spec, base_spec, base_spec] + weights_specs,
        out_specs=[x_spec(tile_c), seq_spec(tile_p, WINDOW, KV_W), seq_spec(tile_p, WINDOW, KV_W),
                   seq_spec(tile_c, S_ROWS, GLA_DV)],
        out_shape=[jax.ShapeDtypeStruct((bsz, t_p, D_MODEL), jnp.float32),
                   jax.ShapeDtypeStruct((bsz, WINDOW, KV_W), jnp.float32),
                   jax.ShapeDtypeStruct((bsz, WINDOW, KV_W), jnp.float32),
                   jax.ShapeDtypeStruct((bsz, S_ROWS, GLA_DV), jnp.float32)],
        scratch_shapes=_slot_shapes(tt, WINDOW + tt, tt // CHUNK) * 2
        + [pltpu.VMEM((S_ROWS, GLA_DV), jnp.float32), pltpu.VMEM((tt, D_MODEL), jnp.bfloat16),
           pltpu.VMEM((D_MODEL, D_MODEL), jnp.bfloat16)] + [pltpu.VMEM((tt, D_MODEL), jnp.float32)] * 2,
        compiler_params=pltpu.CompilerParams(dimension_semantics=("arbitrary",), vmem_limit_bytes=VMEM_LIMIT),
        name="prompt_layer",
    )(sinks, x_prompt, cos_all, sin_all, cos_all, sin_all, wpre, win, wup, bgk, gnw, wout, wpost)

    off_spec = pl.BlockSpec((rows, LANES), lambda i: (tt // rows, 0))
    base_spec = pl.BlockSpec((SUBLANES, LANES), lambda i: (base_blk + nt, 0))
    seq3 = lambda d1, d2: pl.BlockSpec((nb, d1, d2), lambda i: (i, 0, 0))
    tok = lambda d: pl.BlockSpec((rows, d), lambda i: (i, 0))
    y_s, k_s, v_s, s_s = pl.pallas_call(
        _sample_kernel,
        grid=(dec_b // nb,),
        in_specs=[smem, tok(D_MODEL), off_spec, off_spec, base_spec, base_spec, seq3(WINDOW, KV_W), seq3(WINDOW, KV_W),
                  seq3(S_ROWS, GLA_DV)] + weights_specs,
        out_specs=[tok(D_MODEL), tok(KV_W), tok(KV_W), seq3(S_ROWS, GLA_DV)],
        out_shape=[jax.ShapeDtypeStruct((dec_b * t_s, D_MODEL), jnp.float32),
                   jax.ShapeDtypeStruct((dec_b * t_s, KV_W), jnp.float32),
                   jax.ShapeDtypeStruct((dec_b * t_s, KV_W), jnp.float32),
                   jax.ShapeDtypeStruct((dec_b, S_ROWS, GLA_DV), jnp.float32)],
        scratch_shapes=_slot_shapes(rows, nb * (WINDOW + t_s), nb)
        + [pltpu.VMEM((rows, D_MODEL), jnp.bfloat16), pltpu.VMEM((D_MODEL, D_MODEL), jnp.bfloat16)],
        compiler_params=pltpu.CompilerParams(dimension_semantics=("arbitrary",), vmem_limit_bytes=VMEM_LIMIT),
        name="sample_layer",
    )(sinks, x_sample.reshape(dec_b * t_s, D_MODEL), cos_all, sin_all, cos_all, sin_all,
      cache_k[0].reshape(dec_b, WINDOW, KV_W), cache_v[0].reshape(dec_b, WINDOW, KV_W),
      state_gla[0].reshape(dec_b, S_ROWS, GLA_DV), wpre, win, wup, bgk, gnw, wout, wpost)

    kv5 = lambda a, b, t: a.reshape(1, b, t, N_KV_HEADS, HEAD_DIM)
    st5 = lambda a, b: a.reshape(1, b, GLA_HEADS, GLA_DK, GLA_DV)
    return (y_p, y_s.reshape(dec_b, t_s, D_MODEL),
            kv5(k_p, bsz, WINDOW), kv5(v_p, bsz, WINDOW), st5(s_p, bsz),
            kv5(k_s, dec_b, t_s), kv5(v_s, dec_b, t_s), st5(s_s, dec_b))
```

```python
import collections
import functools

import jax
import jax.numpy as jnp
from jax import lax
from jax.experimental import pallas as pl
from jax.experimental.pallas import tpu as pltpu

D_MODEL = 1024
CHUNK = 64
WINDOW = 128
HEAD_DIM = 64
N_HEADS = 8
N_KV_HEADS = 2
GQA_GROUP = N_HEADS // N_KV_HEADS
ROT_DIM = 16
ROPE_THETA = 500000.0
GLA_HEADS = 4
GLA_DV = 128
GLA_DK = 64
GLA_RANK = 16
GATE_TAU = 16.0
NORM_EPS = 1e-6
PAST_LEN = 4096

LANES = 128
SUBLANES = 8
ATTN_W = N_HEADS * HEAD_DIM
KV_W = N_KV_HEADS * HEAD_DIM
GQK_W = GLA_HEADS * GLA_DK
GV_W = GLA_HEADS * GLA_DV
S_ROWS = GLA_HEADS * GLA_DK
OFF_AQ = 0
OFF_AK = OFF_AQ + ATTN_W
OFF_AV = OFF_AK + KV_W
OFF_AG = OFF_AV + KV_W
OFF_GQ = OFF_AG + ATTN_W
OFF_GK = OFF_GQ + GQK_W
OFF_GV = OFF_GK + GQK_W
OFF_GG = OFF_GV + GV_W
OFF_LR = OFF_GG + GV_W
D_IN = OFF_LR + GLA_RANK
D_IN_PAD = OFF_LR + LANES
CAST_BLOCK = 1024
W_IN_ALLOC = -(-D_IN_PAD // CAST_BLOCK) * CAST_BLOCK

PROMPT_TILE = 256
SAMPLE_SEQS = 8
V7X_VMEM_BYTES = 64 * 1024 * 1024
VMEM_LIMIT = V7X_VMEM_BYTES * 3 // 4
STEP_ORDER = "CPCPCPCPCPPPP"
CHUNK_GROUP = 8

NEG = -1e30
LOG2E = 1.4426950408889634

Slot = collections.namedtuple("Slot", "qs kv ag qin kog kdec dec gv gg")


def _slot_shapes(rows, kv_rows, n_chunks):
    bf, f32 = jnp.bfloat16, jnp.float32
    return ([pltpu.VMEM((rows, 2 * ATTN_W), bf)] + [pltpu.VMEM((kv_rows, LANES), bf) for _ in range(6)]
            + [pltpu.VMEM((rows, ATTN_W), f32), pltpu.VMEM((rows, 2 * GQK_W), bf), pltpu.VMEM((rows, GQK_W), bf),
               pltpu.VMEM((rows, GQK_W), f32), pltpu.VMEM((n_chunks * SUBLANES, GQK_W), f32),
               pltpu.VMEM((rows, GV_W), bf), pltpu.VMEM((rows, GV_W), f32)])


N_SLOT_REFS = 14


def _make_slot(refs):
    return Slot(refs[0], tuple(refs[1:7]), *refs[7:14])


def _dot(a, b):
    return jnp.dot(a, b, preferred_element_type=jnp.float32)


def _dot_nt(a, b):
    return lax.dot_general(a, b, (((1,), (1,)), ((), ())), preferred_element_type=jnp.float32)


def _bf(x):
    return x.astype(jnp.bfloat16)


def _lane_masks():
    lane = lax.broadcasted_iota(jnp.int32, (1, LANES), 1)
    return lane < HEAD_DIM, lane >= HEAD_DIM


def _silu(x):
    return x * (1.0 / (1.0 + jnp.exp(-x)))


def _log_sigmoid_scaled(x, scale):
    soft = jnp.log2(1.0 + jnp.exp2(jnp.abs(x) * -LOG2E))
    return jnp.minimum(x, 0.0) * scale - soft * (scale / LOG2E)


def _rms_scale(x):
    return lax.rsqrt(jnp.mean(x * x, axis=-1, keepdims=True) + NORM_EPS)


def _chunk_cumsum(g, chunk):
    row = lax.broadcasted_iota(jnp.int32, g.shape, 0) & (chunk - 1)
    s = 1
    while s < chunk:
        g = g + jnp.where(row >= s, pltpu.roll(g, s, 0), 0.0)
        s *= 2
    return g


def _rotary(xcol, cos_t, sin_dn, sin_up):
    return (xcol * cos_t + pltpu.roll(xcol, ROT_DIM // 2, 1) * sin_dn
            + pltpu.roll(xcol, LANES - ROT_DIM // 2, 1) * sin_up)


def _rotary_lanes(cos_off, sin_off, cos_base, sin_base):
    d = lax.broadcasted_iota(jnp.int32, (1, LANES), 1) & (HEAD_DIM - 1)
    half = ROT_DIM // 2
    rot, up, dn = d < ROT_DIM, d < half, (d >= half) & (d < ROT_DIM)
    cb, sb = cos_base[0:1, :], sin_base[0:1, :]
    co, so = cos_off[...], sin_off[...]
    cos_t = jnp.where(rot, cb, 0.0) * co - jnp.where(rot, sb, 0.0) * so + jnp.where(rot, 0.0, 1.0)
    sin_dn = jnp.where(dn, sb, 0.0) * co + jnp.where(dn, cb, 0.0) * so
    sin_up = jnp.where(up, -sb, 0.0) * co + jnp.where(up, -cb, 0.0) * so
    return cos_t, sin_dn, sin_up


def _kv_variants(k, v):
    m_lo, m_hi = _lane_masks()
    k_sw = pltpu.roll(k, HEAD_DIM, 1)
    v_sw = pltpu.roll(v, HEAD_DIM, 1)
    return (_bf(k), _bf(k_sw),
            _bf(jnp.where(m_lo, v, 0.0)), _bf(jnp.where(m_hi, v_sw, 0.0)),
            _bf(jnp.where(m_lo, v_sw, 0.0)), _bf(jnp.where(m_hi, v, 0.0)))


def _store_halves(ref, j, width, col):
    m_lo, m_hi = _lane_masks()
    ref[:, j * LANES:(j + 1) * LANES] = _bf(jnp.where(m_lo, col, 0.0))
    ref[:, width + j * LANES:width + (j + 1) * LANES] = _bf(jnp.where(m_hi, col, 0.0))


def _project_stages(x, tabs, wpre_ref, win_ref, wup_ref, bgk_ref, chunk, slot, store_kv):
    rows = x.shape[0]
    h = _bf(x * _rms_scale(x) * wpre_ref[...])
    cos_t, sin_dn, sin_up = _rotary_lanes(*tabs)

    def seg(off, width):
        return _dot(h, win_ref[:, off:off + width])

    yield
    q = seg(OFF_AQ, ATTN_W)
    for j in range(ATTN_W // LANES):
        ln = slice(j * LANES, (j + 1) * LANES)
        _store_halves(slot.qs, j, ATTN_W, _rotary(q[:, ln], cos_t, sin_dn, sin_up) * (LOG2E * HEAD_DIM ** -0.5))
    yield
    kv = seg(OFF_AK, 2 * KV_W)
    store_kv(_rotary(kv[:, :KV_W], cos_t, sin_dn, sin_up), kv[:, KV_W:])
    yield
    slot.ag[...] = _silu(seg(OFF_AG, ATTN_W))
    yield
    lr = _bf(seg(OFF_LR, LANES))
    log_a = _log_sigmoid_scaled(_dot(lr, wup_ref[...]) + bgk_ref[...], 1.0 / GATE_TAU)
    b = _chunk_cumsum(log_a, chunk)
    lasts = [b[r + chunk - 1:r + chunk] for r in range(0, rows, chunk)]
    for i, bl in enumerate(lasts):
        slot.dec[i * SUBLANES:(i + 1) * SUBLANES, :] = jnp.broadcast_to(jnp.exp(bl), (SUBLANES, GQK_W))
    b_last = jnp.concatenate([jnp.broadcast_to(bl, (chunk, GQK_W)) for bl in lasts], axis=0)
    q_in = seg(OFF_GQ, GQK_W) * (GLA_DK ** -0.5) * jnp.exp(b)
    for j in range(GQK_W // LANES):
        _store_halves(slot.qin, j, GQK_W, q_in[:, j * LANES:(j + 1) * LANES])
    yield
    gk = seg(OFF_GK, GQK_W)
    slot.kog[...] = _bf(gk * jnp.exp(-b))
    slot.kdec[...] = gk * jnp.exp(b_last - b)
    yield
    slot.gv[...] = _bf(seg(OFF_GV, GV_W))
    yield
    slot.gg[...] = _silu(seg(OFF_GG, GV_W))


def _attend_stages(slot, rows, kv_win, sinks_ref, n_invalid, emit):
    c = rows.stop - rows.start
    lo = [slot.qs[rows, j * LANES:(j + 1) * LANES] for j in range(ATTN_W // LANES)]
    hi = [slot.qs[rows, ATTN_W + j * LANES:ATTN_W + (j + 1) * LANES] for j in range(ATTN_W // LANES)]
    k_a, k_b = kv_win(0), kv_win(1)
    n_hi = k_a.shape[0] - LANES
    zrows = jnp.zeros((LANES - n_hi, LANES), jnp.bfloat16)
    ext = lambda a: jnp.concatenate([a, zrows], axis=0)
    s_a = _dot_nt(jnp.concatenate([lo[0], lo[1], hi[2], hi[3]], axis=0), ext(k_a))
    s_b = _dot_nt(jnp.concatenate([hi[0], hi[1], lo[2], lo[3]], axis=0), ext(k_b))
    yield
    lane = lax.broadcasted_iota(jnp.int32, (1, LANES), 1)
    where = {0: (s_a, 0), 2: (s_a, 1), 5: (s_a, 2), 7: (s_a, 3),
             1: (s_b, 0), 3: (s_b, 1), 4: (s_b, 2), 6: (s_b, 3)}
    probs, rdenom = {}, {}
    for h in range(N_HEADS):
        src, blk = where[h]
        s = src[blk * c:(blk + 1) * c]
        pad_row = jnp.where(lane == n_hi, sinks_ref[0, h] * LOG2E, NEG)
        s_lo = s[:, :LANES]
        s_hi = jnp.where(lane >= n_hi, pad_row, s[:, LANES:])
        if n_invalid is not None:
            s_lo = jnp.where(lane < n_invalid, NEG, s_lo)
        m = jnp.max(jnp.maximum(s_lo, s_hi), axis=-1, keepdims=True)
        p_lo = jnp.exp2(s_lo - m)
        p_hi = jnp.exp2(s_hi - m)
        rdenom[h] = 1.0 / jnp.sum(p_lo + p_hi, axis=-1, keepdims=True)
        probs[h] = _bf(jnp.concatenate([p_lo, p_hi], axis=1))
    outs = {}
    for g in range(N_KV_HEADS):
        for par in range(2):
            ha, hb = g * GQA_GROUP + par, g * GQA_GROUP + par + 2
            o2 = _dot(jnp.concatenate([probs[ha], probs[hb]], axis=0), ext(kv_win(2 + 2 * g + par)))
            outs[ha] = o2[:c] * rdenom[ha]
            outs[hb] = o2[c:] * rdenom[hb]
    yield
    emit(jnp.concatenate([outs[2 * j] + outs[2 * j + 1] for j in range(4)], axis=1))


def _gla_stages(slot, r0, c, state, gnw_ref, mix_ref):
    rows = slice(r0, r0 + c)
    ci0 = (r0 // c) * SUBLANES
    lhs, a2, uw, dcol, vw = [], [], [], [], []
    for p in range(GLA_HEADS // 2):
        ln = slice(p * LANES, (p + 1) * LANES)
        vwide = slot.gv[rows, p * 2 * GLA_DV:(p + 1) * 2 * GLA_DV]
        lhs.append(jnp.concatenate([slot.qin[rows, ln], slot.qin[rows, GQK_W + p * LANES:GQK_W + (p + 1) * LANES]],
                                   axis=0))
        a2.append(_dot_nt(lhs[p], slot.kog[rows, ln]))
        uw.append(_dot(_bf(slot.kdec[rows, ln].T), vwide))
        dcol.append(jnp.broadcast_to(slot.dec[ci0:ci0 + 1, ln], (LANES, LANES)).T)
        vw.append(vwide)
    yield
    outs = []
    ri = lax.broadcasted_iota(jnp.int32, a2[0].shape, 0) & (c - 1)
    ci = lax.broadcasted_iota(jnp.int32, a2[0].shape, 1)
    for p in range(GLA_HEADS // 2):
        am = _bf(jnp.where(ri >= ci, a2[p], 0.0))
        sb = _bf(state[p])
        for i in range(2):
            hr = slice(i * c, (i + 1) * c)
            outs.append(_dot(jnp.concatenate([lhs[p][hr], am[hr]], axis=1),
                             jnp.concatenate([sb, vw[p][:, i * GLA_DV:(i + 1) * GLA_DV]], axis=0)))
        u_pair = jnp.concatenate([uw[p][:GLA_DK, :GLA_DV], uw[p][GLA_DK:, GLA_DV:]], axis=0)
        state[p] = dcol[p] * state[p] + u_pair
    yield
    gnw = gnw_ref[...]
    for hh, o in enumerate(outs):
        y = o * _rms_scale(o) * gnw * slot.gg[rows, hh * GLA_DV:(hh + 1) * GLA_DV]
        mix_ref[rows, ATTN_W + hh * GLA_DV:ATTN_W + (hh + 1) * GLA_DV] = _bf(y)


def _lockstep(gens):
    live = list(gens)
    while live:
        for g in list(live):
            try:
                next(g)
            except StopIteration:
                live.remove(g)
        yield


def _chunk_stages(slot, chunk_args, c, state_of, sinks_ref, gnw_ref, mix_ref):
    gens = []
    for i, (r0, kv_win, n_invalid) in enumerate(chunk_args):
        rows = slice(r0, r0 + c)

        def emit(attn, rows=rows):
            mix_ref[rows, 0:ATTN_W] = _bf(attn * slot.ag[rows, :])

        gens.append(_attend_stages(slot, rows, kv_win, sinks_ref, n_invalid, emit))
        gens.append(_gla_stages(slot, r0, c, state_of(i), gnw_ref, mix_ref))
    for g in range(0, len(gens), 2 * CHUNK_GROUP):
        yield from _lockstep(gens[g:g + 2 * CHUNK_GROUP])


def _finish(x, mix_ref, wout_ref, wpost_ref):
    mix = _dot(mix_ref[...], wout_ref[...])
    return x + mix * _rms_scale(mix) * wpost_ref[...]


def _run_order(order, p_gen, c_gen):
    for ch in order:
        next(p_gen if ch == "P" else c_gen, None)
    for g in (p_gen, c_gen):
        for _ in g:
            pass


def _pipeline_step(n_tiles, slots, body, first_step_init):
    s = pl.program_id(0)

    @pl.when(s == 0)
    def _():
        first_step_init()
        body(slots[0], slots[1], True, False)

    @pl.when(s == n_tiles)
    def _():
        body(slots[n_tiles % 2], slots[1 - n_tiles % 2], False, True)

    middle = (s > 0) & (s < n_tiles)

    @pl.when(middle & (s % 2 == 0))
    def _():
        body(slots[0], slots[1], True, True)

    @pl.when(middle & (s % 2 == 1))
    def _():
        body(slots[1], slots[0], True, True)


def _prompt_kernel(nt, n_tiles, sinks_ref, xp_ref, xc_ref, coff_ref, soff_ref, cbase_ref, sbase_ref, wpre_ref, win_ref,
                   wup_ref, bgk_ref, gnw_ref, wout_ref, wpost_ref,
                   y_ref, kout_ref, vout_ref, sout_ref, *scratch):
    slots = (_make_slot(scratch[:N_SLOT_REFS]), _make_slot(scratch[N_SLOT_REFS:2 * N_SLOT_REFS]))
    s_ref, mix_ref, woutb_ref = scratch[2 * N_SLOT_REFS:]
    tt = xp_ref.shape[1]
    t_c = jnp.maximum(pl.program_id(0) - 1, 0) % nt

    def body(slot_p, slot_c, project, chunks):
        def store_kv(k, v):
            kout_ref[0] = k[tt - WINDOW:]
            vout_ref[0] = v[tt - WINDOW:]
            for r, rc, val in zip(slot_p.kv, slot_c.kv, _kv_variants(k, v)):
                r[WINDOW:WINDOW + tt, :] = val
                r[0:WINDOW, :] = rc[tt:tt + WINDOW, :]

        p_gen = (_project_stages(xp_ref[0], (coff_ref, soff_ref, cbase_ref, sbase_ref), wpre_ref, win_ref, wup_ref,
                                 bgk_ref, CHUNK, slot_p, store_kv) if project else iter(()))

        def c_stages():
            keep = t_c != 0
            state = [jnp.where(keep, s_ref[p * LANES:(p + 1) * LANES, :], 0.0) for p in range(GLA_HEADS // 2)]
            win = WINDOW + CHUNK
            chunk_args = []
            for r0 in range(0, tt, CHUNK):
                n_invalid = WINDOW - (t_c * tt + r0) if r0 < WINDOW else None
                chunk_args.append((r0, lambda i, r0=r0: slot_c.kv[i][r0:r0 + win, :], n_invalid))
            yield from _chunk_stages(slot_c, chunk_args, CHUNK, lambda i: state, sinks_ref, gnw_ref, mix_ref)
            y_ref[0] = _finish(xc_ref[0], mix_ref, woutb_ref, wpost_ref)
            for p in range(GLA_HEADS // 2):
                s_ref[p * LANES:(p + 1) * LANES, :] = state[p]
                sout_ref[0, p * LANES:(p + 1) * LANES, :] = state[p]

        _run_order(STEP_ORDER if project and chunks else "", p_gen, c_stages() if chunks else iter(()))

    def first_step_init():
        for r in slots[1].kv + (s_ref,):
            r[...] = jnp.zeros_like(r)
        woutb_ref[...] = _bf(wout_ref[...])

    _pipeline_step(n_tiles, slots, body, first_step_init)


def _sample_kernel(sinks_ref, x_ref, coff_ref, soff_ref, cbase_ref, sbase_ref, ck_ref, cv_ref, s0_ref, wpre_ref,
                   win_ref, wup_ref, bgk_ref, gnw_ref, wout_ref, wpost_ref,
                   y_ref, kout_ref, vout_ref, sout_ref, *scratch):
    slot = _make_slot(scratch[:N_SLOT_REFS])
    mix_ref, woutb_ref = scratch[N_SLOT_REFS:]
    nb = ck_ref.shape[0]

    @pl.when(pl.program_id(0) == 0)
    def _():
        woutb_ref[...] = _bf(wout_ref[...])

    c = x_ref.shape[0] // nb
    win = WINDOW + c

    def store_kv(k, v):
        kout_ref[...] = k
        vout_ref[...] = v
        for i in range(nb):
            rows = slice(i * c, (i + 1) * c)
            kv = _kv_variants(jnp.concatenate([ck_ref[i], k[rows]], axis=0),
                              jnp.concatenate([cv_ref[i], v[rows]], axis=0))
            for r, val in zip(slot.kv, kv):
                r[i * win:(i + 1) * win, :] = val

    x = x_ref[...]
    for _ in _project_stages(x, (coff_ref, soff_ref, cbase_ref, sbase_ref), wpre_ref, win_ref, wup_ref, bgk_ref, c, slot,
                             store_kv):
        pass
    states = [[s0_ref[i, p * LANES:(p + 1) * LANES, :] for p in range(GLA_HEADS // 2)] for i in range(nb)]
    chunk_args = [(i * c, lambda j, i=i: slot.kv[j][i * win:(i + 1) * win, :], None) for i in range(nb)]
    for _ in _chunk_stages(slot, chunk_args, c, lambda i: states[i], sinks_ref, gnw_ref, mix_ref):
        pass
    for i in range(nb):
        for p in range(GLA_HEADS // 2):
            sout_ref[i, p * LANES:(p + 1) * LANES, :] = states[i][p]
    y_ref[...] = _finish(x, mix_ref, woutb_ref, wpost_ref)


def _rotary_angles(offsets, bases):
    half = ROT_DIM // 2
    d = jnp.arange(LANES) % HEAD_DIM
    inv = ROPE_THETA ** (-(d % half).astype(jnp.float32) * (2.0 / ROT_DIM))
    pos = jnp.concatenate([offsets, jnp.repeat(bases, SUBLANES)]).astype(jnp.float32)
    ang = pos[:, None] * inv[None, :]
    return jnp.cos(ang), jnp.sin(ang)


def _cast_w_in_kernel(wt_ref, out_ref):
    row = lax.broadcasted_iota(jnp.int32, wt_ref.shape, 0) + pl.program_id(0) * CAST_BLOCK
    out_ref[...] = _bf(jnp.where(row < D_IN, wt_ref[...], 0.0).T)


def _cast_w_in(w_in_t):
    return pl.pallas_call(
        _cast_w_in_kernel,
        grid=(W_IN_ALLOC // CAST_BLOCK,),
        in_specs=[pl.BlockSpec((CAST_BLOCK, D_MODEL), lambda i: (i, 0))],
        out_specs=pl.BlockSpec((D_MODEL, CAST_BLOCK), lambda i: (0, i)),
        out_shape=jax.ShapeDtypeStruct((D_MODEL, W_IN_ALLOC), jnp.bfloat16),
        compiler_params=pltpu.CompilerParams(dimension_semantics=("arbitrary",)),
        name="cast_w_in",
    )(w_in_t)


def _const_spec(shape):
    zeros = (0,) * len(shape)
    return pl.BlockSpec(shape, lambda i: zeros, pipeline_mode=pl.Buffered(1))


def kernel(x_prompt, x_sample, cache_k, cache_v, state_gla, norm_pre_w, w_in, attn_sinks, w_gk_up, b_gk,
           gla_norm_w, w_out, norm_post_w):
    bsz, t_p, _ = x_prompt.shape
    dec_b, t_s, _ = x_sample.shape
    assert w_in.shape[0] == 1, "single layer"
    assert t_p % PROMPT_TILE == 0 and PROMPT_TILE % CHUNK == 0 and PROMPT_TILE >= WINDOW
    assert dec_b % SAMPLE_SEQS == 0 and t_s & (t_s - 1) == 0

    win = _cast_w_in(w_in[0].T)
    wout = w_out[0]
    wup = _bf(jnp.pad(w_gk_up[0], ((0, LANES - GLA_RANK), (0, 0))))
    wpre = norm_pre_w[0][None, :]
    wpost = norm_post_w[0][None, :]
    bgk = b_gk[0][None, :]
    gnw = gla_norm_w[0][None, :]
    sinks = attn_sinks[0][None, :]
    smem = pl.BlockSpec(memory_space=pltpu.SMEM)

    weights_specs = [
        _const_spec((1, D_MODEL)), _const_spec((D_MODEL, D_IN_PAD)), _const_spec((LANES, GQK_W)),
        _const_spec((1, GQK_W)), _const_spec((1, GLA_DV)), _const_spec((D_MODEL, D_MODEL)),
        _const_spec((1, D_MODEL))]

    tt = PROMPT_TILE
    nt = t_p // tt
    n_tiles = bsz * nt
    nb = SAMPLE_SEQS
    rows = nb * t_s
    assert tt % rows == 0 and (tt + rows) % SUBLANES == 0
    cos_all, sin_all = _rotary_angles(jnp.concatenate([jnp.arange(tt), jnp.tile(jnp.arange(t_s), nb)]),
                                      jnp.concatenate([jnp.arange(nt) * tt, jnp.array([PAST_LEN])]))
    base_blk = (tt + rows) // SUBLANES

    def tile_p(s):
        return jnp.minimum(s, n_tiles - 1)

    def tile_c(s):
        return jnp.maximum(s - 1, 0)

    off_spec = pl.BlockSpec((tt, LANES), lambda s: (0, 0))
    base_spec = pl.BlockSpec((SUBLANES, LANES), lambda s: (base_blk + tile_p(s) % nt, 0))
    x_spec = lambda tile: pl.BlockSpec((1, tt, D_MODEL), lambda s: (tile(s) // nt, tile(s) % nt, 0))
    seq_spec = lambda tile, d1, d2: pl.BlockSpec((1, d1, d2), lambda s: (tile(s) // nt, 0, 0))
    y_p, k_p, v_p, s_p = pl.pallas_call(
        functools.partial(_prompt_kernel, nt, n_tiles),
        grid=(n_tiles + 1,),
        in_specs=[smem, x_spec(tile_p), x_spec(tile_c), off_spec, off_spec, base_spec, base_spec] + weights_specs,
        out_specs=[x_spec(tile_c), seq_spec(tile_p, WINDOW, KV_W), seq_spec(tile_p, WINDOW, KV_W),
                   seq_spec(tile_c, S_ROWS, GLA_DV)],
        out_shape=[jax.ShapeDtypeStruct((bsz, t_p, D_MODEL), jnp.float32),
                   jax.ShapeDtypeStruct((bsz, WINDOW, KV_W), jnp.float32),
                   jax.ShapeDtypeStruct((bsz, WINDOW, KV_W), jnp.float32),
                   jax.ShapeDtypeStruct((bsz, S_ROWS, GLA_DV), jnp.float32)],
        scratch_shapes=_slot_shapes(tt, WINDOW + tt, tt // CHUNK) * 2
        + [pltpu.VMEM((S_ROWS, GLA_DV), jnp.float32), pltpu.VMEM((tt, D_MODEL), jnp.bfloat16),
           pltpu.VMEM((D_MODEL, D_MODEL), jnp.bfloat16)],
        compiler_params=pltpu.CompilerParams(dimension_semantics=("arbitrary",), vmem_limit_bytes=VMEM_LIMIT),
        name="prompt_layer",
    )(sinks, x_prompt, x_prompt, cos_all, sin_all, cos_all, sin_all, wpre, win, wup, bgk, gnw, wout, wpost)

    off_spec = pl.BlockSpec((rows, LANES), lambda i: (tt // rows, 0))
    base_spec = pl.BlockSpec((SUBLANES, LANES), lambda i: (base_blk + nt, 0))
    seq3 = lambda d1, d2: pl.BlockSpec((nb, d1, d2), lambda i: (i, 0, 0))
    tok = lambda d: pl.BlockSpec((rows, d), lambda i: (i, 0))
    y_s, k_s, v_s, s_s = pl.pallas_call(
        _sample_kernel,
        grid=(dec_b // nb,),
        in_specs=[smem, tok(D_MODEL), off_spec, off_spec, base_spec, base_spec, seq3(WINDOW, KV_W), seq3(WINDOW, KV_W),
                  seq3(S_ROWS, GLA_DV)] + weights_specs,
        out_specs=[tok(D_MODEL), tok(KV_W), tok(KV_W), seq3(S_ROWS, GLA_DV)],
        out_shape=[jax.ShapeDtypeStruct((dec_b * t_s, D_MODEL), jnp.float32),
                   jax.ShapeDtypeStruct((dec_b * t_s, KV_W), jnp.float32),
                   jax.ShapeDtypeStruct((dec_b * t_s, KV_W), jnp.float32),
                   jax.ShapeDtypeStruct((dec_b, S_ROWS, GLA_DV), jnp.float32)],
        scratch_shapes=_slot_shapes(rows, nb * (WINDOW + t_s), nb)
        + [pltpu.VMEM((rows, D_MODEL), jnp.bfloat16), pltpu.VMEM((D_MODEL, D_MODEL), jnp.bfloat16)],
        compiler_params=pltpu.CompilerParams(dimension_semantics=("arbitrary",), vmem_limit_bytes=VMEM_LIMIT),
        name="sample_layer",
    )(sinks, x_sample.reshape(dec_b * t_s, D_MODEL), cos_all, sin_all, cos_all, sin_all,
      cache_k[0].reshape(dec_b, WINDOW, KV_W), cache_v[0].reshape(dec_b, WINDOW, KV_W),
      state_gla[0].reshape(dec_b, S_ROWS, GLA_DV), wpre, win, wup, bgk, gnw, wout, wpost)

    kv5 = lambda a, b, t: a.reshape(1, b, t, N_KV_HEADS, HEAD_DIM)
    st5 = lambda a, b: a.reshape(1, b, GLA_HEADS, GLA_DK, GLA_DV)
    return (y_p, y_s.reshape(dec_b, t_s, D_MODEL),
            kv5(k_p, bsz, WINDOW), kv5(v_p, bsz, WINDOW), st5(s_p, bsz),
            kv5(k_s, dec_b, t_s), kv5(v_s, dec_b, t_s), st5(s_s, dec_b))
```

```python
import collections
import functools

import jax
import jax.numpy as jnp
from jax import lax
from jax.experimental import pallas as pl
from jax.experimental.pallas import tpu as pltpu

D_MODEL = 1024
CHUNK = 64
WINDOW = 128
HEAD_DIM = 64
N_HEADS = 8
N_KV_HEADS = 2
GQA_GROUP = N_HEADS // N_KV_HEADS
ROT_DIM = 16
ROPE_THETA = 500000.0
GLA_HEADS = 4
GLA_DV = 128
GLA_DK = 64
GLA_RANK = 16
GATE_TAU = 16.0
NORM_EPS = 1e-6
PAST_LEN = 4096

LANES = 128
SUBLANES = 8
ATTN_W = N_HEADS * HEAD_DIM
KV_W = N_KV_HEADS * HEAD_DIM
GQK_W = GLA_HEADS * GLA_DK
GV_W = GLA_HEADS * GLA_DV
S_ROWS = GLA_HEADS * GLA_DK
OFF_AQ = 0
OFF_AK = OFF_AQ + ATTN_W
OFF_AV = OFF_AK + KV_W
OFF_AG = OFF_AV + KV_W
OFF_GQ = OFF_AG + ATTN_W
OFF_GK = OFF_GQ + GQK_W
OFF_GV = OFF_GK + GQK_W
OFF_GG = OFF_GV + GV_W
OFF_LR = OFF_GG + GV_W
D_IN = OFF_LR + GLA_RANK
D_IN_PAD = OFF_LR + LANES
CAST_BLOCK = 1024
W_IN_ALLOC = -(-D_IN_PAD // CAST_BLOCK) * CAST_BLOCK

PROMPT_TILE = 512
SAMPLE_SEQS = 8
V7X_VMEM_BYTES = 64 * 1024 * 1024
VMEM_LIMIT = V7X_VMEM_BYTES * 3 // 4
STEP_ORDER = "CPCPCPCPCPPPP"
CHUNK_GROUP = 8

NEG = -1e30
LOG2E = 1.4426950408889634

Slot = collections.namedtuple("Slot", "qs kv ag qin kog kdec dec gv gg")


def _slot_shapes(rows, kv_rows, n_chunks):
    bf, f32 = jnp.bfloat16, jnp.float32
    return ([pltpu.VMEM((rows, 2 * ATTN_W), bf)] + [pltpu.VMEM((kv_rows, LANES), bf) for _ in range(6)]
            + [pltpu.VMEM((rows, ATTN_W), f32), pltpu.VMEM((rows, 2 * GQK_W), bf), pltpu.VMEM((rows, GQK_W), bf),
               pltpu.VMEM((rows, GQK_W), f32), pltpu.VMEM((n_chunks * SUBLANES, GQK_W), f32),
               pltpu.VMEM((rows, GV_W), bf), pltpu.VMEM((rows, GV_W), f32)])


N_SLOT_REFS = 14


def _make_slot(refs):
    return Slot(refs[0], tuple(refs[1:7]), *refs[7:14])


def _dot(a, b):
    return jnp.dot(a, b, preferred_element_type=jnp.float32)


def _dot_nt(a, b):
    return lax.dot_general(a, b, (((1,), (1,)), ((), ())), preferred_element_type=jnp.float32)


def _bf(x):
    return x.astype(jnp.bfloat16)


def _lane_masks():
    lane = lax.broadcasted_iota(jnp.int32, (1, LANES), 1)
    return lane < HEAD_DIM, lane >= HEAD_DIM


def _silu(x):
    return x * (1.0 / (1.0 + jnp.exp(-x)))


def _log_sigmoid_scaled(x, scale):
    soft = jnp.log2(1.0 + jnp.exp2(jnp.abs(x) * -LOG2E))
    return jnp.minimum(x, 0.0) * scale - soft * (scale / LOG2E)


def _rms_scale(x):
    return lax.rsqrt(jnp.mean(x * x, axis=-1, keepdims=True) + NORM_EPS)


def _chunk_cumsum(g, chunk):
    row = lax.broadcasted_iota(jnp.int32, g.shape, 0) & (chunk - 1)
    s = 1
    while s < chunk:
        g = g + jnp.where(row >= s, pltpu.roll(g, s, 0), 0.0)
        s *= 2
    return g


def _rotary(xcol, cos_t, sin_dn, sin_up):
    return (xcol * cos_t + pltpu.roll(xcol, ROT_DIM // 2, 1) * sin_dn
            + pltpu.roll(xcol, LANES - ROT_DIM // 2, 1) * sin_up)


def _rotary_lanes(cos_off, sin_off, cos_base, sin_base):
    d = lax.broadcasted_iota(jnp.int32, (1, LANES), 1) & (HEAD_DIM - 1)
    half = ROT_DIM // 2
    rot, up, dn = d < ROT_DIM, d < half, (d >= half) & (d < ROT_DIM)
    cb, sb = cos_base[0:1, :], sin_base[0:1, :]
    co, so = cos_off[...], sin_off[...]
    cos_t = jnp.where(rot, cb, 0.0) * co - jnp.where(rot, sb, 0.0) * so + jnp.where(rot, 0.0, 1.0)
    sin_dn = jnp.where(dn, sb, 0.0) * co + jnp.where(dn, cb, 0.0) * so
    sin_up = jnp.where(up, -sb, 0.0) * co + jnp.where(up, -cb, 0.0) * so
    return cos_t, sin_dn, sin_up


def _kv_variants(k, v):
    m_lo, m_hi = _lane_masks()
    k_sw = pltpu.roll(k, HEAD_DIM, 1)
    v_sw = pltpu.roll(v, HEAD_DIM, 1)
    return (_bf(k), _bf(k_sw),
            _bf(jnp.where(m_lo, v, 0.0)), _bf(jnp.where(m_hi, v_sw, 0.0)),
            _bf(jnp.where(m_lo, v_sw, 0.0)), _bf(jnp.where(m_hi, v, 0.0)))


def _store_halves(ref, j, width, col):
    m_lo, m_hi = _lane_masks()
    ref[:, j * LANES:(j + 1) * LANES] = _bf(jnp.where(m_lo, col, 0.0))
    ref[:, width + j * LANES:width + (j + 1) * LANES] = _bf(jnp.where(m_hi, col, 0.0))


def _project_stages(x, tabs, wpre_ref, win_ref, wup_ref, bgk_ref, chunk, slot, store_kv):
    rows = x.shape[0]
    h = _bf(x * _rms_scale(x) * wpre_ref[...])
    cos_t, sin_dn, sin_up = _rotary_lanes(*tabs)

    def seg(off, width):
        return _dot(h, win_ref[:, off:off + width])

    yield
    q = seg(OFF_AQ, ATTN_W)
    for j in range(ATTN_W // LANES):
        ln = slice(j * LANES, (j + 1) * LANES)
        _store_halves(slot.qs, j, ATTN_W, _rotary(q[:, ln], cos_t, sin_dn, sin_up) * (LOG2E * HEAD_DIM ** -0.5))
    yield
    kv = seg(OFF_AK, 2 * KV_W)
    store_kv(_rotary(kv[:, :KV_W], cos_t, sin_dn, sin_up), kv[:, KV_W:])
    yield
    slot.ag[...] = _silu(seg(OFF_AG, ATTN_W))
    yield
    lr = _bf(seg(OFF_LR, LANES))
    log_a = _log_sigmoid_scaled(_dot(lr, wup_ref[...]) + bgk_ref[...], 1.0 / GATE_TAU)
    b = _chunk_cumsum(log_a, chunk)
    lasts = [b[r + chunk - 1:r + chunk] for r in range(0, rows, chunk)]
    for i, bl in enumerate(lasts):
        slot.dec[i * SUBLANES:(i + 1) * SUBLANES, :] = jnp.broadcast_to(jnp.exp(bl), (SUBLANES, GQK_W))
    b_last = jnp.concatenate([jnp.broadcast_to(bl, (chunk, GQK_W)) for bl in lasts], axis=0)
    q_in = seg(OFF_GQ, GQK_W) * (GLA_DK ** -0.5) * jnp.exp(b)
    for j in range(GQK_W // LANES):
        _store_halves(slot.qin, j, GQK_W, q_in[:, j * LANES:(j + 1) * LANES])
    yield
    gk = seg(OFF_GK, GQK_W)
    slot.kog[...] = _bf(gk * jnp.exp(-b))
    slot.kdec[...] = gk * jnp.exp(b_last - b)
    yield
    slot.gv[...] = _bf(seg(OFF_GV, GV_W))
    yield
    slot.gg[...] = _silu(seg(OFF_GG, GV_W))


def _attend_stages(slot, rows, kv_win, sinks_ref, n_invalid, emit):
    c = rows.stop - rows.start
    lo = [slot.qs[rows, j * LANES:(j + 1) * LANES] for j in range(ATTN_W // LANES)]
    hi = [slot.qs[rows, ATTN_W + j * LANES:ATTN_W + (j + 1) * LANES] for j in range(ATTN_W // LANES)]
    k_a, k_b = kv_win(0), kv_win(1)
    n_hi = k_a.shape[0] - LANES
    zrows = jnp.zeros((LANES - n_hi, LANES), jnp.bfloat16)
    ext = lambda a: jnp.concatenate([a, zrows], axis=0)
    s_a = _dot_nt(jnp.concatenate([lo[0], lo[1], hi[2], hi[3]], axis=0), ext(k_a))
    s_b = _dot_nt(jnp.concatenate([hi[0], hi[1], lo[2], lo[3]], axis=0), ext(k_b))
    yield
    lane = lax.broadcasted_iota(jnp.int32, (1, LANES), 1)
    where = {0: (s_a, 0), 2: (s_a, 1), 5: (s_a, 2), 7: (s_a, 3),
             1: (s_b, 0), 3: (s_b, 1), 4: (s_b, 2), 6: (s_b, 3)}
    probs, rdenom = {}, {}
    for h in range(N_HEADS):
        src, blk = where[h]
        s = src[blk * c:(blk + 1) * c]
        pad_row = jnp.where(lane == n_hi, sinks_ref[0, h] * LOG2E, NEG)
        s_lo = s[:, :LANES]
        s_hi = jnp.where(lane >= n_hi, pad_row, s[:, LANES:])
        if n_invalid is not None:
            s_lo = jnp.where(lane < n_invalid, NEG, s_lo)
        m = jnp.max(jnp.maximum(s_lo, s_hi), axis=-1, keepdims=True)
        p_lo = jnp.exp2(s_lo - m)
        p_hi = jnp.exp2(s_hi - m)
        rdenom[h] = 1.0 / jnp.sum(p_lo + p_hi, axis=-1, keepdims=True)
        probs[h] = _bf(jnp.concatenate([p_lo, p_hi], axis=1))
    outs = {}
    for g in range(N_KV_HEADS):
        for par in range(2):
            ha, hb = g * GQA_GROUP + par, g * GQA_GROUP + par + 2
            o2 = _dot(jnp.concatenate([probs[ha], probs[hb]], axis=0), ext(kv_win(2 + 2 * g + par)))
            outs[ha] = o2[:c] * rdenom[ha]
            outs[hb] = o2[c:] * rdenom[hb]
    yield
    emit(jnp.concatenate([outs[2 * j] + outs[2 * j + 1] for j in range(4)], axis=1))


def _gla_stages(slot, r0, c, state, gnw_ref, mix_ref):
    rows = slice(r0, r0 + c)
    ci0 = (r0 // c) * SUBLANES
    lhs, a2, uw, dcol, vw = [], [], [], [], []
    for p in range(GLA_HEADS // 2):
        ln = slice(p * LANES, (p + 1) * LANES)
        vwide = slot.gv[rows, p * 2 * GLA_DV:(p + 1) * 2 * GLA_DV]
        lhs.append(jnp.concatenate([slot.qin[rows, ln], slot.qin[rows, GQK_W + p * LANES:GQK_W + (p + 1) * LANES]],
                                   axis=0))
        a2.append(_dot_nt(lhs[p], slot.kog[rows, ln]))
        uw.append(_dot(_bf(slot.kdec[rows, ln].T), vwide))
        dcol.append(jnp.broadcast_to(slot.dec[ci0:ci0 + 1, ln], (LANES, LANES)).T)
        vw.append(vwide)
    yield
    outs = []
    ri = lax.broadcasted_iota(jnp.int32, a2[0].shape, 0) & (c - 1)
    ci = lax.broadcasted_iota(jnp.int32, a2[0].shape, 1)
    for p in range(GLA_HEADS // 2):
        am = _bf(jnp.where(ri >= ci, a2[p], 0.0))
        sb = _bf(state[p])
        for i in range(2):
            hr = slice(i * c, (i + 1) * c)
            outs.append(_dot(jnp.concatenate([lhs[p][hr], am[hr]], axis=1),
                             jnp.concatenate([sb, vw[p][:, i * GLA_DV:(i + 1) * GLA_DV]], axis=0)))
        u_pair = jnp.concatenate([uw[p][:GLA_DK, :GLA_DV], uw[p][GLA_DK:, GLA_DV:]], axis=0)
        state[p] = dcol[p] * state[p] + u_pair
    yield
    gnw = gnw_ref[...]
    for hh, o in enumerate(outs):
        y = o * _rms_scale(o) * gnw * slot.gg[rows, hh * GLA_DV:(hh + 1) * GLA_DV]
        mix_ref[rows, ATTN_W + hh * GLA_DV:ATTN_W + (hh + 1) * GLA_DV] = _bf(y)


def _lockstep(gens):
    live = list(gens)
    while live:
        for g in list(live):
            try:
                next(g)
            except StopIteration:
                live.remove(g)
        yield


def _chunk_stages(slot, chunk_args, c, state_of, sinks_ref, gnw_ref, mix_ref):
    gens = []
    for i, (r0, kv_win, n_invalid) in enumerate(chunk_args):
        rows = slice(r0, r0 + c)

        def emit(attn, rows=rows):
            mix_ref[rows, 0:ATTN_W] = _bf(attn * slot.ag[rows, :])

        gens.append(_attend_stages(slot, rows, kv_win, sinks_ref, n_invalid, emit))
        gens.append(_gla_stages(slot, r0, c, state_of(i), gnw_ref, mix_ref))
    for g in range(0, len(gens), 2 * CHUNK_GROUP):
        yield from _lockstep(gens[g:g + 2 * CHUNK_GROUP])


def _finish(x, mix_ref, wout_ref, wpost_ref):
    mix = _dot(mix_ref[...], wout_ref[...])
    return x + mix * _rms_scale(mix) * wpost_ref[...]


def _run_order(order, p_gen, c_gen):
    for ch in order:
        next(p_gen if ch == "P" else c_gen, None)
    for g in (p_gen, c_gen):
        for _ in g:
            pass


def _pipeline_step(n_tiles, slots, body, first_step_init):
    s = pl.program_id(0)

    @pl.when(s == 0)
    def _():
        first_step_init()
        body(slots[0], slots[1], True, False)

    @pl.when(s == n_tiles)
    def _():
        body(slots[n_tiles % 2], slots[1 - n_tiles % 2], False, True)

    middle = (s > 0) & (s < n_tiles)

    @pl.when(middle & (s % 2 == 0))
    def _():
        body(slots[0], slots[1], True, True)

    @pl.when(middle & (s % 2 == 1))
    def _():
        body(slots[1], slots[0], True, True)


def _prompt_kernel(nt, n_tiles, sinks_ref, xp_ref, xc_ref, coff_ref, soff_ref, cbase_ref, sbase_ref, wpre_ref, win_ref,
                   wup_ref, bgk_ref, gnw_ref, wout_ref, wpost_ref,
                   y_ref, kout_ref, vout_ref, sout_ref, *scratch):
    slots = (_make_slot(scratch[:N_SLOT_REFS]), _make_slot(scratch[N_SLOT_REFS:2 * N_SLOT_REFS]))
    s_ref, mix_ref, woutb_ref = scratch[2 * N_SLOT_REFS:]
    tt = xp_ref.shape[1]
    t_c = jnp.maximum(pl.program_id(0) - 1, 0) % nt

    def body(slot_p, slot_c, project, chunks):
        def store_kv(k, v):
            kout_ref[0] = k[tt - WINDOW:]
            vout_ref[0] = v[tt - WINDOW:]
            for r, rc, val in zip(slot_p.kv, slot_c.kv, _kv_variants(k, v)):
                r[WINDOW:WINDOW + tt, :] = val
                r[0:WINDOW, :] = rc[tt:tt + WINDOW, :]

        p_gen = (_project_stages(xp_ref[0], (coff_ref, soff_ref, cbase_ref, sbase_ref), wpre_ref, win_ref, wup_ref,
                                 bgk_ref, CHUNK, slot_p, store_kv) if project else iter(()))

        def c_stages():
            keep = t_c != 0
            state = [jnp.where(keep, s_ref[p * LANES:(p + 1) * LANES, :], 0.0) for p in range(GLA_HEADS // 2)]
            win = WINDOW + CHUNK
            chunk_args = []
            for r0 in range(0, tt, CHUNK):
                n_invalid = WINDOW - (t_c * tt + r0) if r0 < WINDOW else None
                chunk_args.append((r0, lambda i, r0=r0: slot_c.kv[i][r0:r0 + win, :], n_invalid))
            yield from _chunk_stages(slot_c, chunk_args, CHUNK, lambda i: state, sinks_ref, gnw_ref, mix_ref)
            y_ref[0] = _finish(xc_ref[0], mix_ref, woutb_ref, wpost_ref)
            for p in range(GLA_HEADS // 2):
                s_ref[p * LANES:(p + 1) * LANES, :] = state[p]
                sout_ref[0, p * LANES:(p + 1) * LANES, :] = state[p]

        _run_order(STEP_ORDER if project and chunks else "", p_gen, c_stages() if chunks else iter(()))

    def first_step_init():
        for r in slots[1].kv + (s_ref,):
            r[...] = jnp.zeros_like(r)
        woutb_ref[...] = _bf(wout_ref[...])

    _pipeline_step(n_tiles, slots, body, first_step_init)


def _sample_kernel(sinks_ref, x_ref, coff_ref, soff_ref, cbase_ref, sbase_ref, ck_ref, cv_ref, s0_ref, wpre_ref,
                   win_ref, wup_ref, bgk_ref, gnw_ref, wout_ref, wpost_ref,
                   y_ref, kout_ref, vout_ref, sout_ref, *scratch):
    slot = _make_slot(scratch[:N_SLOT_REFS])
    mix_ref, woutb_ref = scratch[N_SLOT_REFS:]
    nb = ck_ref.shape[0]

    @pl.when(pl.program_id(0) == 0)
    def _():
        woutb_ref[...] = _bf(wout_ref[...])

    c = x_ref.shape[0] // nb
    win = WINDOW + c

    def store_kv(k, v):
        kout_ref[...] = k
        vout_ref[...] = v
        for i in range(nb):
            rows = slice(i * c, (i + 1) * c)
            kv = _kv_variants(jnp.concatenate([ck_ref[i], k[rows]], axis=0),
                              jnp.concatenate([cv_ref[i], v[rows]], axis=0))
            for r, val in zip(slot.kv, kv):
                r[i * win:(i + 1) * win, :] = val

    x = x_ref[...]
    for _ in _project_stages(x, (coff_ref, soff_ref, cbase_ref, sbase_ref), wpre_ref, win_ref, wup_ref, bgk_ref, c, slot,
                             store_kv):
        pass
    states = [[s0_ref[i, p * LANES:(p + 1) * LANES, :] for p in range(GLA_HEADS // 2)] for i in range(nb)]
    chunk_args = [(i * c, lambda j, i=i: slot.kv[j][i * win:(i + 1) * win, :], None) for i in range(nb)]
    for _ in _chunk_stages(slot, chunk_args, c, lambda i: states[i], sinks_ref, gnw_ref, mix_ref):
        pass
    for i in range(nb):
        for p in range(GLA_HEADS // 2):
            sout_ref[i, p * LANES:(p + 1) * LANES, :] = states[i][p]
    y_ref[...] = _finish(x, mix_ref, woutb_ref, wpost_ref)


def _rotary_angles(offsets, bases):
    half = ROT_DIM // 2
    d = jnp.arange(LANES) % HEAD_DIM
    inv = ROPE_THETA ** (-(d % half).astype(jnp.float32) * (2.0 / ROT_DIM))
    pos = jnp.concatenate([offsets, jnp.repeat(bases, SUBLANES)]).astype(jnp.float32)
    ang = pos[:, None] * inv[None, :]
    return jnp.cos(ang), jnp.sin(ang)


def _cast_w_in_kernel(wt_ref, out_ref):
    row = lax.broadcasted_iota(jnp.int32, wt_ref.shape, 0) + pl.program_id(0) * CAST_BLOCK
    out_ref[...] = _bf(jnp.where(row < D_IN, wt_ref[...], 0.0).T)


def _cast_w_in(w_in_t):
    return pl.pallas_call(
        _cast_w_in_kernel,
        grid=(W_IN_ALLOC // CAST_BLOCK,),
        in_specs=[pl.BlockSpec((CAST_BLOCK, D_MODEL), lambda i: (i, 0))],
        out_specs=pl.BlockSpec((D_MODEL, CAST_BLOCK), lambda i: (0, i)),
        out_shape=jax.ShapeDtypeStruct((D_MODEL, W_IN_ALLOC), jnp.bfloat16),
        compiler_params=pltpu.CompilerParams(dimension_semantics=("arbitrary",)),
        name="cast_w_in",
    )(w_in_t)


def _const_spec(shape):
    zeros = (0,) * len(shape)
    return pl.BlockSpec(shape, lambda i: zeros, pipeline_mode=pl.Buffered(1))


def kernel(x_prompt, x_sample, cache_k, cache_v, state_gla, norm_pre_w, w_in, attn_sinks, w_gk_up, b_gk,
           gla_norm_w, w_out, norm_post_w):
    bsz, t_p, _ = x_prompt.shape
    dec_b, t_s, _ = x_sample.shape
    assert w_in.shape[0] == 1, "single layer"
    assert t_p % PROMPT_TILE == 0 and PROMPT_TILE % CHUNK == 0 and PROMPT_TILE >= WINDOW
    assert dec_b % SAMPLE_SEQS == 0 and t_s & (t_s - 1) == 0

    win = _cast_w_in(w_in[0].T)
    wout = w_out[0]
    wup = _bf(jnp.pad(w_gk_up[0], ((0, LANES - GLA_RANK), (0, 0))))
    wpre = norm_pre_w[0][None, :]
    wpost = norm_post_w[0][None, :]
    bgk = b_gk[0][None, :]
    gnw = gla_norm_w[0][None, :]
    sinks = attn_sinks[0][None, :]
    smem = pl.BlockSpec(memory_space=pltpu.SMEM)

    weights_specs = [
        _const_spec((1, D_MODEL)), _const_spec((D_MODEL, D_IN_PAD)), _const_spec((LANES, GQK_W)),
        _const_spec((1, GQK_W)), _const_spec((1, GLA_DV)), _const_spec((D_MODEL, D_MODEL)),
        _const_spec((1, D_MODEL))]

    tt = PROMPT_TILE
    nt = t_p // tt
    n_tiles = bsz * nt
    nb = SAMPLE_SEQS
    rows = nb * t_s
    assert tt % rows == 0 and (tt + rows) % SUBLANES == 0
    cos_all, sin_all = _rotary_angles(jnp.concatenate([jnp.arange(tt), jnp.tile(jnp.arange(t_s), nb)]),
                                      jnp.concatenate([jnp.arange(nt) * tt, jnp.array([PAST_LEN])]))
    base_blk = (tt + rows) // SUBLANES

    def tile_p(s):
        return jnp.minimum(s, n_tiles - 1)

    def tile_c(s):
        return jnp.maximum(s - 1, 0)

    off_spec = pl.BlockSpec((tt, LANES), lambda s: (0, 0))
    base_spec = pl.BlockSpec((SUBLANES, LANES), lambda s: (base_blk + tile_p(s) % nt, 0))
    x_spec = lambda tile: pl.BlockSpec((1, tt, D_MODEL), lambda s: (tile(s) // nt, tile(s) % nt, 0))
    seq_spec = lambda tile, d1, d2: pl.BlockSpec((1, d1, d2), lambda s: (tile(s) // nt, 0, 0))
    y_p, k_p, v_p, s_p = pl.pallas_call(
        functools.partial(_prompt_kernel, nt, n_tiles),
        grid=(n_tiles + 1,),
        in_specs=[smem, x_spec(tile_p), x_spec(tile_c), off_spec, off_spec, base_spec, base_spec] + weights_specs,
        out_specs=[x_spec(tile_c), seq_spec(tile_p, WINDOW, KV_W), seq_spec(tile_p, WINDOW, KV_W),
                   seq_spec(tile_c, S_ROWS, GLA_DV)],
        out_shape=[jax.ShapeDtypeStruct((bsz, t_p, D_MODEL), jnp.float32),
                   jax.ShapeDtypeStruct((bsz, WINDOW, KV_W), jnp.float32),
                   jax.ShapeDtypeStruct((bsz, WINDOW, KV_W), jnp.float32),
                   jax.ShapeDtypeStruct((bsz, S_ROWS, GLA_DV), jnp.float32)],
        scratch_shapes=_slot_shapes(tt, WINDOW + tt, tt // CHUNK) * 2
        + [pltpu.VMEM((S_ROWS, GLA_DV), jnp.float32), pltpu.VMEM((tt, D_MODEL), jnp.bfloat16),
           pltpu.VMEM((D_MODEL, D_MODEL), jnp.bfloat16)],
        compiler_params=pltpu.CompilerParams(dimension_semantics=("arbitrary",), vmem_limit_bytes=VMEM_LIMIT),
        name="prompt_layer",
    )(sinks, x_prompt, x_prompt, cos_all, sin_all, cos_all, sin_all, wpre, win, wup, bgk, gnw, wout, wpost)

    off_spec = pl.BlockSpec((rows, LANES), lambda i: (tt // rows, 0))
    base_spec = pl.BlockSpec((SUBLANES, LANES), lambda i: (base_blk + nt, 0))
    seq3 = lambda d1, d2: pl.BlockSpec((nb, d1, d2), lambda i: (i, 0, 0))
    tok = lambda d: pl.BlockSpec((rows, d), lambda i: (i, 0))
    y_s, k_s, v_s, s_s = pl.pallas_call(
        _sample_kernel,
        grid=(dec_b // nb,),
        in_specs=[smem, tok(D_MODEL), off_spec, off_spec, base_spec, base_spec, seq3(WINDOW, KV_W), seq3(WINDOW, KV_W),
                  seq3(S_ROWS, GLA_DV)] + weights_specs,
        out_specs=[tok(D_MODEL), tok(KV_W), tok(KV_W), seq3(S_ROWS, GLA_DV)],
        out_shape=[jax.ShapeDtypeStruct((dec_b * t_s, D_MODEL), jnp.float32),
                   jax.ShapeDtypeStruct((dec_b * t_s, KV_W), jnp.float32),
                   jax.ShapeDtypeStruct((dec_b * t_s, KV_W), jnp.float32),
                   jax.ShapeDtypeStruct((dec_b, S_ROWS, GLA_DV), jnp.float32)],
        scratch_shapes=_slot_shapes(rows, nb * (WINDOW + t_s), nb)
        + [pltpu.VMEM((rows, D_MODEL), jnp.bfloat16), pltpu.VMEM((D_MODEL, D_MODEL), jnp.bfloat16)],
        compiler_params=pltpu.CompilerParams(dimension_semantics=("arbitrary",), vmem_limit_bytes=VMEM_LIMIT),
        name="sample_layer",
    )(sinks, x_sample.reshape(dec_b * t_s, D_MODEL), cos_all, sin_all, cos_all, sin_all,
      cache_k[0].reshape(dec_b, WINDOW, KV_W), cache_v[0].reshape(dec_b, WINDOW, KV_W),
      state_gla[0].reshape(dec_b, S_ROWS, GLA_DV), wpre, win, wup, bgk, gnw, wout, wpost)

    kv5 = lambda a, b, t: a.reshape(1, b, t, N_KV_HEADS, HEAD_DIM)
    st5 = lambda a, b: a.reshape(1, b, GLA_HEADS, GLA_DK, GLA_DV)
    return (y_p, y_s.reshape(dec_b, t_s, D_MODEL),
            kv5(k_p, bsz, WINDOW), kv5(v_p, bsz, WINDOW), st5(s_p, bsz),
            kv5(k_s, dec_b, t_s), kv5(v_s, dec_b, t_s), st5(s_s, dec_b))
```

```python
import collections
import functools

import jax
import jax.numpy as jnp
from jax import lax
from jax.experimental import pallas as pl
from jax.experimental.pallas import tpu as pltpu

D_MODEL = 1024
CHUNK = 64
WINDOW = 128
HEAD_DIM = 64
N_HEADS = 8
N_KV_HEADS = 2
GQA_GROUP = N_HEADS // N_KV_HEADS
ROT_DIM = 16
ROPE_THETA = 500000.0
GLA_HEADS = 4
GLA_DV = 128
GLA_DK = 64
GLA_RANK = 16
GATE_TAU = 16.0
NORM_EPS = 1e-6
PAST_LEN = 4096

LANES = 128
SUBLANES = 8
ATTN_W = N_HEADS * HEAD_DIM
KV_W = N_KV_HEADS * HEAD_DIM
GQK_W = GLA_HEADS * GLA_DK
GV_W = GLA_HEADS * GLA_DV
S_ROWS = GLA_HEADS * GLA_DK
OFF_AQ = 0
OFF_AK = OFF_AQ + ATTN_W
OFF_AV = OFF_AK + KV_W
OFF_AG = OFF_AV + KV_W
OFF_GQ = OFF_AG + ATTN_W
OFF_GK = OFF_GQ + GQK_W
OFF_GV = OFF_GK + GQK_W
OFF_GG = OFF_GV + GV_W
OFF_LR = OFF_GG + GV_W
D_IN = OFF_LR + GLA_RANK
D_IN_PAD = OFF_LR + LANES
CAST_BLOCK = 1024
W_IN_ALLOC = -(-D_IN_PAD // CAST_BLOCK) * CAST_BLOCK

PROMPT_TILE = 512
SAMPLE_SEQS = 8
V7X_VMEM_BYTES = 64 * 1024 * 1024
VMEM_LIMIT = V7X_VMEM_BYTES * 3 // 4
STEP_ORDER = "CPCPPCPPPCPP"
CHUNK_GROUP = 8

NEG = -1e30
LOG2E = 1.4426950408889634

Slot = collections.namedtuple("Slot", "qs kv ag qin kog kdec dec gv gg")


def _slot_shapes(rows, kv_rows, n_chunks):
    bf, f32 = jnp.bfloat16, jnp.float32
    return ([pltpu.VMEM((rows, 2 * ATTN_W), bf)] + [pltpu.VMEM((kv_rows, LANES), bf) for _ in range(6)]
            + [pltpu.VMEM((rows, ATTN_W), f32), pltpu.VMEM((rows, 2 * GQK_W), bf), pltpu.VMEM((rows, GQK_W), bf),
               pltpu.VMEM((rows, GQK_W), f32), pltpu.VMEM((n_chunks * SUBLANES, GQK_W), f32),
               pltpu.VMEM((rows, GV_W), bf), pltpu.VMEM((rows, GV_W), f32)])


N_SLOT_REFS = 14


def _make_slot(refs):
    return Slot(refs[0], tuple(refs[1:7]), *refs[7:14])


def _dot(a, b):
    return jnp.dot(a, b, preferred_element_type=jnp.float32)


def _dot_nt(a, b):
    return lax.dot_general(a, b, (((1,), (1,)), ((), ())), preferred_element_type=jnp.float32)


def _bf(x):
    return x.astype(jnp.bfloat16)


def _lane_masks():
    lane = lax.broadcasted_iota(jnp.int32, (1, LANES), 1)
    return lane < HEAD_DIM, lane >= HEAD_DIM


def _silu(x):
    return x * (1.0 / (1.0 + jnp.exp(-x)))


def _log_sigmoid_scaled(x, scale):
    soft = jnp.log2(1.0 + jnp.exp2(jnp.abs(x) * -LOG2E))
    return jnp.minimum(x, 0.0) * scale - soft * (scale / LOG2E)


def _rms_scale(x):
    return lax.rsqrt(jnp.mean(x * x, axis=-1, keepdims=True) + NORM_EPS)


def _chunk_cumsum(g, chunk):
    row = lax.broadcasted_iota(jnp.int32, g.shape, 0) & (chunk - 1)
    s = 1
    while s < chunk:
        g = g + jnp.where(row >= s, pltpu.roll(g, s, 0), 0.0)
        s *= 2
    return g


def _rotary(xcol, cos_t, sin_dn, sin_up):
    return (xcol * cos_t + pltpu.roll(xcol, ROT_DIM // 2, 1) * sin_dn
            + pltpu.roll(xcol, LANES - ROT_DIM // 2, 1) * sin_up)


def _rotary_lanes(cos_off, sin_off, cos_base, sin_base):
    d = lax.broadcasted_iota(jnp.int32, (1, LANES), 1) & (HEAD_DIM - 1)
    half = ROT_DIM // 2
    rot, up, dn = d < ROT_DIM, d < half, (d >= half) & (d < ROT_DIM)
    cb, sb = cos_base[0:1, :], sin_base[0:1, :]
    co, so = cos_off[...], sin_off[...]
    cos_t = jnp.where(rot, cb, 0.0) * co - jnp.where(rot, sb, 0.0) * so + jnp.where(rot, 0.0, 1.0)
    sin_dn = jnp.where(dn, sb, 0.0) * co + jnp.where(dn, cb, 0.0) * so
    sin_up = jnp.where(up, -sb, 0.0) * co + jnp.where(up, -cb, 0.0) * so
    return cos_t, sin_dn, sin_up


def _kv_variants(k, v):
    m_lo, m_hi = _lane_masks()
    k_sw = pltpu.roll(k, HEAD_DIM, 1)
    v_sw = pltpu.roll(v, HEAD_DIM, 1)
    return (_bf(k), _bf(k_sw),
            _bf(jnp.where(m_lo, v, 0.0)), _bf(jnp.where(m_hi, v_sw, 0.0)),
            _bf(jnp.where(m_lo, v_sw, 0.0)), _bf(jnp.where(m_hi, v, 0.0)))


def _store_halves(ref, j, width, col):
    m_lo, m_hi = _lane_masks()
    ref[:, j * LANES:(j + 1) * LANES] = _bf(jnp.where(m_lo, col, 0.0))
    ref[:, width + j * LANES:width + (j + 1) * LANES] = _bf(jnp.where(m_hi, col, 0.0))


def _project_stages(x, tabs, wpre_ref, win_ref, wup_ref, bgk_ref, chunk, slot, store_kv):
    rows = x.shape[0]
    h = _bf(x * _rms_scale(x) * wpre_ref[...])
    cos_t, sin_dn, sin_up = _rotary_lanes(*tabs)

    def seg(off, width):
        return _dot(h, win_ref[:, off:off + width])

    yield
    q = seg(OFF_AQ, ATTN_W)
    for j in range(ATTN_W // LANES):
        ln = slice(j * LANES, (j + 1) * LANES)
        _store_halves(slot.qs, j, ATTN_W, _rotary(q[:, ln], cos_t, sin_dn, sin_up) * (LOG2E * HEAD_DIM ** -0.5))
    yield
    kv = seg(OFF_AK, 2 * KV_W)
    store_kv(_rotary(kv[:, :KV_W], cos_t, sin_dn, sin_up), kv[:, KV_W:])
    yield
    slot.ag[...] = _silu(seg(OFF_AG, ATTN_W))
    yield
    lr = _bf(seg(OFF_LR, LANES))
    log_a = _log_sigmoid_scaled(_dot(lr, wup_ref[...]) + bgk_ref[...], 1.0 / GATE_TAU)
    b = _chunk_cumsum(log_a, chunk)
    lasts = [b[r + chunk - 1:r + chunk] for r in range(0, rows, chunk)]
    for i, bl in enumerate(lasts):
        slot.dec[i * SUBLANES:(i + 1) * SUBLANES, :] = jnp.broadcast_to(jnp.exp(bl), (SUBLANES, GQK_W))
    b_last = jnp.concatenate([jnp.broadcast_to(bl, (chunk, GQK_W)) for bl in lasts], axis=0)
    gqk = seg(OFF_GQ, 2 * GQK_W)
    q_in = gqk[:, :GQK_W] * (GLA_DK ** -0.5) * jnp.exp(b)
    for j in range(GQK_W // LANES):
        _store_halves(slot.qin, j, GQK_W, q_in[:, j * LANES:(j + 1) * LANES])
    yield
    gk = gqk[:, GQK_W:]
    slot.kog[...] = _bf(gk * jnp.exp(-b))
    slot.kdec[...] = gk * jnp.exp(b_last - b)
    yield
    slot.gv[...] = _bf(seg(OFF_GV, GV_W))
    yield
    slot.gg[...] = _silu(seg(OFF_GG, GV_W))


def _attend_stages(slot, rows, kv_win, sinks_ref, n_invalid, emit):
    c = rows.stop - rows.start
    lo = [slot.qs[rows, j * LANES:(j + 1) * LANES] for j in range(ATTN_W // LANES)]
    hi = [slot.qs[rows, ATTN_W + j * LANES:ATTN_W + (j + 1) * LANES] for j in range(ATTN_W // LANES)]
    k_a, k_b = kv_win(0), kv_win(1)
    n_hi = k_a.shape[0] - LANES
    zrows = jnp.zeros((LANES - n_hi, LANES), jnp.bfloat16)
    ext = lambda a: jnp.concatenate([a, zrows], axis=0)
    s_a = _dot_nt(jnp.concatenate([lo[0], lo[1], hi[2], hi[3]], axis=0), ext(k_a))
    s_b = _dot_nt(jnp.concatenate([hi[0], hi[1], lo[2], lo[3]], axis=0), ext(k_b))
    yield
    lane = lax.broadcasted_iota(jnp.int32, (1, LANES), 1)
    where = {0: (s_a, 0), 2: (s_a, 1), 5: (s_a, 2), 7: (s_a, 3),
             1: (s_b, 0), 3: (s_b, 1), 4: (s_b, 2), 6: (s_b, 3)}
    probs, rdenom = {}, {}
    for h in range(N_HEADS):
        src, blk = where[h]
        s = src[blk * c:(blk + 1) * c]
        pad_row = jnp.where(lane == n_hi, sinks_ref[0, h] * LOG2E, NEG)
        s_lo = s[:, :LANES]
        s_hi = jnp.where(lane >= n_hi, pad_row, s[:, LANES:])
        if n_invalid is not None:
            s_lo = jnp.where(lane < n_invalid, NEG, s_lo)
        m = jnp.max(jnp.maximum(s_lo, s_hi), axis=-1, keepdims=True)
        p_lo = jnp.exp2(s_lo - m)
        p_hi = jnp.exp2(s_hi - m)
        rdenom[h] = 1.0 / jnp.sum(p_lo + p_hi, axis=-1, keepdims=True)
        probs[h] = _bf(jnp.concatenate([p_lo, p_hi], axis=1))
    outs = {}
    for g in range(N_KV_HEADS):
        for par in range(2):
            ha, hb = g * GQA_GROUP + par, g * GQA_GROUP + par + 2
            o2 = _dot(jnp.concatenate([probs[ha], probs[hb]], axis=0), ext(kv_win(2 + 2 * g + par)))
            outs[ha] = o2[:c] * rdenom[ha]
            outs[hb] = o2[c:] * rdenom[hb]
    yield
    emit(jnp.concatenate([outs[2 * j] + outs[2 * j + 1] for j in range(4)], axis=1))


def _gla_stages(slot, r0, c, state, gnw_ref, mix_ref):
    rows = slice(r0, r0 + c)
    ci0 = (r0 // c) * SUBLANES
    lhs, a2, uw, dcol, vw = [], [], [], [], []
    for p in range(GLA_HEADS // 2):
        ln = slice(p * LANES, (p + 1) * LANES)
        vwide = slot.gv[rows, p * 2 * GLA_DV:(p + 1) * 2 * GLA_DV]
        lhs.append(jnp.concatenate([slot.qin[rows, ln], slot.qin[rows, GQK_W + p * LANES:GQK_W + (p + 1) * LANES]],
                                   axis=0))
        a2.append(_dot_nt(lhs[p], slot.kog[rows, ln]))
        uw.append(_dot(_bf(slot.kdec[rows, ln].T), vwide))
        dcol.append(jnp.broadcast_to(slot.dec[ci0:ci0 + 1, ln], (LANES, LANES)).T)
        vw.append(vwide)
    yield
    outs = []
    ri = lax.broadcasted_iota(jnp.int32, a2[0].shape, 0) & (c - 1)
    ci = lax.broadcasted_iota(jnp.int32, a2[0].shape, 1)
    for p in range(GLA_HEADS // 2):
        am = _bf(jnp.where(ri >= ci, a2[p], 0.0))
        sb = _bf(state[p])
        for i in range(2):
            hr = slice(i * c, (i + 1) * c)
            outs.append(_dot(jnp.concatenate([lhs[p][hr], am[hr]], axis=1),
                             jnp.concatenate([sb, vw[p][:, i * GLA_DV:(i + 1) * GLA_DV]], axis=0)))
        u_pair = jnp.concatenate([uw[p][:GLA_DK, :GLA_DV], uw[p][GLA_DK:, GLA_DV:]], axis=0)
        state[p] = dcol[p] * state[p] + u_pair
    yield
    gnw = gnw_ref[...]
    for hh, o in enumerate(outs):
        y = o * _rms_scale(o) * gnw * slot.gg[rows, hh * GLA_DV:(hh + 1) * GLA_DV]
        mix_ref[rows, ATTN_W + hh * GLA_DV:ATTN_W + (hh + 1) * GLA_DV] = _bf(y)


def _lockstep(gens):
    live = list(gens)
    while live:
        for g in list(live):
            try:
                next(g)
            except StopIteration:
                live.remove(g)
        yield


def _chunk_stages(slot, chunk_args, c, state_of, sinks_ref, gnw_ref, mix_ref):
    gens = []
    for i, (r0, kv_win, n_invalid) in enumerate(chunk_args):
        rows = slice(r0, r0 + c)

        def emit(attn, rows=rows):
            mix_ref[rows, 0:ATTN_W] = _bf(attn * slot.ag[rows, :])

        gens.append(_attend_stages(slot, rows, kv_win, sinks_ref, n_invalid, emit))
        gens.append(_gla_stages(slot, r0, c, state_of(i), gnw_ref, mix_ref))
    for g in range(0, len(gens), 2 * CHUNK_GROUP):
        yield from _lockstep(gens[g:g + 2 * CHUNK_GROUP])


def _finish(x, mix_ref, wout_ref, wpost_ref):
    mix = _dot(mix_ref[...], wout_ref[...])
    return x + mix * _rms_scale(mix) * wpost_ref[...]


def _run_order(order, p_gen, c_gen):
    for ch in order:
        next(p_gen if ch == "P" else c_gen, None)
    for g in (p_gen, c_gen):
        for _ in g:
            pass


def _pipeline_step(n_tiles, slots, body, first_step_init):
    s = pl.program_id(0)

    @pl.when(s == 0)
    def _():
        first_step_init()
        body(slots[0], slots[1], True, False)

    @pl.when(s == n_tiles)
    def _():
        body(slots[n_tiles % 2], slots[1 - n_tiles % 2], False, True)

    middle = (s > 0) & (s < n_tiles)

    @pl.when(middle & (s % 2 == 0))
    def _():
        body(slots[0], slots[1], True, True)

    @pl.when(middle & (s % 2 == 1))
    def _():
        body(slots[1], slots[0], True, True)


def _prompt_kernel(nt, n_tiles, sinks_ref, xp_ref, xc_ref, coff_ref, soff_ref, cbase_ref, sbase_ref, wpre_ref, win_ref,
                   wup_ref, bgk_ref, gnw_ref, wout_ref, wpost_ref,
                   y_ref, kout_ref, vout_ref, sout_ref, *scratch):
    slots = (_make_slot(scratch[:N_SLOT_REFS]), _make_slot(scratch[N_SLOT_REFS:2 * N_SLOT_REFS]))
    s_ref, mix_ref, woutb_ref = scratch[2 * N_SLOT_REFS:]
    tt = xp_ref.shape[1]
    t_c = jnp.maximum(pl.program_id(0) - 1, 0) % nt

    def body(slot_p, slot_c, project, chunks):
        def store_kv(k, v):
            kout_ref[0] = k[tt - WINDOW:]
            vout_ref[0] = v[tt - WINDOW:]
            for r, rc, val in zip(slot_p.kv, slot_c.kv, _kv_variants(k, v)):
                r[WINDOW:WINDOW + tt, :] = val
                r[0:WINDOW, :] = rc[tt:tt + WINDOW, :]

        p_gen = (_project_stages(xp_ref[0], (coff_ref, soff_ref, cbase_ref, sbase_ref), wpre_ref, win_ref, wup_ref,
                                 bgk_ref, CHUNK, slot_p, store_kv) if project else iter(()))

        def c_stages():
            keep = t_c != 0
            state = [jnp.where(keep, s_ref[p * LANES:(p + 1) * LANES, :], 0.0) for p in range(GLA_HEADS // 2)]
            win = WINDOW + CHUNK
            chunk_args = []
            for r0 in range(0, tt, CHUNK):
                n_invalid = WINDOW - (t_c * tt + r0) if r0 < WINDOW else None
                chunk_args.append((r0, lambda i, r0=r0: slot_c.kv[i][r0:r0 + win, :], n_invalid))
            yield from _chunk_stages(slot_c, chunk_args, CHUNK, lambda i: state, sinks_ref, gnw_ref, mix_ref)
            y_ref[0] = _finish(xc_ref[0], mix_ref, woutb_ref, wpost_ref)
            for p in range(GLA_HEADS // 2):
                s_ref[p * LANES:(p + 1) * LANES, :] = state[p]
                sout_ref[0, p * LANES:(p + 1) * LANES, :] = state[p]

        _run_order(STEP_ORDER if project and chunks else "", p_gen, c_stages() if chunks else iter(()))

    def first_step_init():
        for r in slots[1].kv + (s_ref,):
            r[...] = jnp.zeros_like(r)
        woutb_ref[...] = _bf(wout_ref[...])

    _pipeline_step(n_tiles, slots, body, first_step_init)


def _sample_kernel(sinks_ref, x_ref, coff_ref, soff_ref, cbase_ref, sbase_ref, ck_ref, cv_ref, s0_ref, wpre_ref,
                   win_ref, wup_ref, bgk_ref, gnw_ref, wout_ref, wpost_ref,
                   y_ref, kout_ref, vout_ref, sout_ref, *scratch):
    slot = _make_slot(scratch[:N_SLOT_REFS])
    mix_ref, woutb_ref = scratch[N_SLOT_REFS:]
    nb = ck_ref.shape[0]

    @pl.when(pl.program_id(0) == 0)
    def _():
        woutb_ref[...] = _bf(wout_ref[...])

    c = x_ref.shape[0] // nb
    win = WINDOW + c

    def store_kv(k, v):
        kout_ref[...] = k
        vout_ref[...] = v
        for i in range(nb):
            rows = slice(i * c, (i + 1) * c)
            kv = _kv_variants(jnp.concatenate([ck_ref[i], k[rows]], axis=0),
                              jnp.concatenate([cv_ref[i], v[rows]], axis=0))
            for r, val in zip(slot.kv, kv):
                r[i * win:(i + 1) * win, :] = val

    x = x_ref[...]
    for _ in _project_stages(x, (coff_ref, soff_ref, cbase_ref, sbase_ref), wpre_ref, win_ref, wup_ref, bgk_ref, c, slot,
                             store_kv):
        pass
    states = [[s0_ref[i, p * LANES:(p + 1) * LANES, :] for p in range(GLA_HEADS // 2)] for i in range(nb)]
    chunk_args = [(i * c, lambda j, i=i: slot.kv[j][i * win:(i + 1) * win, :], None) for i in range(nb)]
    for _ in _chunk_stages(slot, chunk_args, c, lambda i: states[i], sinks_ref, gnw_ref, mix_ref):
        pass
    for i in range(nb):
        for p in range(GLA_HEADS // 2):
            sout_ref[i, p * LANES:(p + 1) * LANES, :] = states[i][p]
    y_ref[...] = _finish(x, mix_ref, woutb_ref, wpost_ref)


def _rotary_angles(offsets, bases):
    half = ROT_DIM // 2
    d = jnp.arange(LANES) % HEAD_DIM
    inv = ROPE_THETA ** (-(d % half).astype(jnp.float32) * (2.0 / ROT_DIM))
    pos = jnp.concatenate([offsets, jnp.repeat(bases, SUBLANES)]).astype(jnp.float32)
    ang = pos[:, None] * inv[None, :]
    return jnp.cos(ang), jnp.sin(ang)


def _cast_w_in_kernel(wt_ref, out_ref):
    row = lax.broadcasted_iota(jnp.int32, wt_ref.shape, 0) + pl.program_id(0) * CAST_BLOCK
    out_ref[...] = _bf(jnp.where(row < D_IN, wt_ref[...], 0.0).T)


def _cast_w_in(w_in_t):
    return pl.pallas_call(
        _cast_w_in_kernel,
        grid=(W_IN_ALLOC // CAST_BLOCK,),
        in_specs=[pl.BlockSpec((CAST_BLOCK, D_MODEL), lambda i: (i, 0))],
        out_specs=pl.BlockSpec((D_MODEL, CAST_BLOCK), lambda i: (0, i)),
        out_shape=jax.ShapeDtypeStruct((D_MODEL, W_IN_ALLOC), jnp.bfloat16),
        compiler_params=pltpu.CompilerParams(dimension_semantics=("arbitrary",)),
        name="cast_w_in",
    )(w_in_t)


def _const_spec(shape):
    zeros = (0,) * len(shape)
    return pl.BlockSpec(shape, lambda i: zeros, pipeline_mode=pl.Buffered(1))


def kernel(x_prompt, x_sample, cache_k, cache_v, state_gla, norm_pre_w, w_in, attn_sinks, w_gk_up, b_gk,
           gla_norm_w, w_out, norm_post_w):
    bsz, t_p, _ = x_prompt.shape
    dec_b, t_s, _ = x_sample.shape
    assert w_in.shape[0] == 1, "single layer"
    assert t_p % PROMPT_TILE == 0 and PROMPT_TILE % CHUNK == 0 and PROMPT_TILE >= WINDOW
    assert dec_b % SAMPLE_SEQS == 0 and t_s & (t_s - 1) == 0

    win = _cast_w_in(w_in[0].T)
    wout = w_out[0]
    wup = _bf(jnp.pad(w_gk_up[0], ((0, LANES - GLA_RANK), (0, 0))))
    wpre = norm_pre_w[0][None, :]
    wpost = norm_post_w[0][None, :]
    bgk = b_gk[0][None, :]
    gnw = gla_norm_w[0][None, :]
    sinks = attn_sinks[0][None, :]
    smem = pl.BlockSpec(memory_space=pltpu.SMEM)

    weights_specs = [
        _const_spec((1, D_MODEL)), _const_spec((D_MODEL, D_IN_PAD)), _const_spec((LANES, GQK_W)),
        _const_spec((1, GQK_W)), _const_spec((1, GLA_DV)), _const_spec((D_MODEL, D_MODEL)),
        _const_spec((1, D_MODEL))]

    tt = PROMPT_TILE
    nt = t_p // tt
    n_tiles = bsz * nt
    nb = SAMPLE_SEQS
    rows = nb * t_s
    assert tt % rows == 0 and (tt + rows) % SUBLANES == 0
    cos_all, sin_all = _rotary_angles(jnp.concatenate([jnp.arange(tt), jnp.tile(jnp.arange(t_s), nb)]),
                                      jnp.concatenate([jnp.arange(nt) * tt, jnp.array([PAST_LEN])]))
    base_blk = (tt + rows) // SUBLANES

    def tile_p(s):
        return jnp.minimum(s, n_tiles - 1)

    def tile_c(s):
        return jnp.maximum(s - 1, 0)

    off_spec = pl.BlockSpec((tt, LANES), lambda s: (0, 0))
    base_spec = pl.BlockSpec((SUBLANES, LANES), lambda s: (base_blk + tile_p(s) % nt, 0))
    x_spec = lambda tile: pl.BlockSpec((1, tt, D_MODEL), lambda s: (tile(s) // nt, tile(s) % nt, 0))
    seq_spec = lambda tile, d1, d2: pl.BlockSpec((1, d1, d2), lambda s: (tile(s) // nt, 0, 0))
    y_p, k_p, v_p, s_p = pl.pallas_call(
        functools.partial(_prompt_kernel, nt, n_tiles),
        grid=(n_tiles + 1,),
        in_specs=[smem, x_spec(tile_p), x_spec(tile_c), off_spec, off_spec, base_spec, base_spec] + weights_specs,
        out_specs=[x_spec(tile_c), seq_spec(tile_p, WINDOW, KV_W), seq_spec(tile_p, WINDOW, KV_W),
                   seq_spec(tile_c, S_ROWS, GLA_DV)],
        out_shape=[jax.ShapeDtypeStruct((bsz, t_p, D_MODEL), jnp.float32),
                   jax.ShapeDtypeStruct((bsz, WINDOW, KV_W), jnp.float32),
                   jax.ShapeDtypeStruct((bsz, WINDOW, KV_W), jnp.float32),
                   jax.ShapeDtypeStruct((bsz, S_ROWS, GLA_DV), jnp.float32)],
        scratch_shapes=_slot_shapes(tt, WINDOW + tt, tt // CHUNK) * 2
        + [pltpu.VMEM((S_ROWS, GLA_DV), jnp.float32), pltpu.VMEM((tt, D_MODEL), jnp.bfloat16),
           pltpu.VMEM((D_MODEL, D_MODEL), jnp.bfloat16)],
        compiler_params=pltpu.CompilerParams(dimension_semantics=("arbitrary",), vmem_limit_bytes=VMEM_LIMIT),
        name="prompt_layer",
    )(sinks, x_prompt, x_prompt, cos_all, sin_all, cos_all, sin_all, wpre, win, wup, bgk, gnw, wout, wpost)

    off_spec = pl.BlockSpec((rows, LANES), lambda i: (tt // rows, 0))
    base_spec = pl.BlockSpec((SUBLANES, LANES), lambda i: (base_blk + nt, 0))
    seq3 = lambda d1, d2: pl.BlockSpec((nb, d1, d2), lambda i: (i, 0, 0))
    tok = lambda d: pl.BlockSpec((rows, d), lambda i: (i, 0))
    y_s, k_s, v_s, s_s = pl.pallas_call(
        _sample_kernel,
        grid=(dec_b // nb,),
        in_specs=[smem, tok(D_MODEL), off_spec, off_spec, base_spec, base_spec, seq3(WINDOW, KV_W), seq3(WINDOW, KV_W),
                  seq3(S_ROWS, GLA_DV)] + weights_specs,
        out_specs=[tok(D_MODEL), tok(KV_W), tok(KV_W), seq3(S_ROWS, GLA_DV)],
        out_shape=[jax.ShapeDtypeStruct((dec_b * t_s, D_MODEL), jnp.float32),
                   jax.ShapeDtypeStruct((dec_b * t_s, KV_W), jnp.float32),
                   jax.ShapeDtypeStruct((dec_b * t_s, KV_W), jnp.float32),
                   jax.ShapeDtypeStruct((dec_b, S_ROWS, GLA_DV), jnp.float32)],
        scratch_shapes=_slot_shapes(rows, nb * (WINDOW + t_s), nb)
        + [pltpu.VMEM((rows, D_MODEL), jnp.bfloat16), pltpu.VMEM((D_MODEL, D_MODEL), jnp.bfloat16)],
        compiler_params=pltpu.CompilerParams(dimension_semantics=("arbitrary",), vmem_limit_bytes=VMEM_LIMIT),
        name="sample_layer",
    )(sinks, x_sample.reshape(dec_b * t_s, D_MODEL), cos_all, sin_all, cos_all, sin_all,
      cache_k[0].reshape(dec_b, WINDOW, KV_W), cache_v[0].reshape(dec_b, WINDOW, KV_W),
      state_gla[0].reshape(dec_b, S_ROWS, GLA_DV), wpre, win, wup, bgk, gnw, wout, wpost)

    kv5 = lambda a, b, t: a.reshape(1, b, t, N_KV_HEADS, HEAD_DIM)
    st5 = lambda a, b: a.reshape(1, b, GLA_HEADS, GLA_DK, GLA_DV)
    return (y_p, y_s.reshape(dec_b, t_s, D_MODEL),
            kv5(k_p, bsz, WINDOW), kv5(v_p, bsz, WINDOW), st5(s_p, bsz),
            kv5(k_s, dec_b, t_s), kv5(v_s, dec_b, t_s), st5(s_s, dec_b))
```

```python
import collections
import functools

import jax
import jax.numpy as jnp
from jax import lax
from jax.experimental import pallas as pl
from jax.experimental.pallas import tpu as pltpu

D_MODEL = 1024
CHUNK = 64
WINDOW = 128
HEAD_DIM = 64
N_HEADS = 8
N_KV_HEADS = 2
GQA_GROUP = N_HEADS // N_KV_HEADS
ROT_DIM = 16
ROPE_THETA = 500000.0
GLA_HEADS = 4
GLA_DV = 128
GLA_DK = 64
GLA_RANK = 16
GATE_TAU = 16.0
NORM_EPS = 1e-6
PAST_LEN = 4096

LANES = 128
SUBLANES = 8
ATTN_W = N_HEADS * HEAD_DIM
KV_W = N_KV_HEADS * HEAD_DIM
GQK_W = GLA_HEADS * GLA_DK
GV_W = GLA_HEADS * GLA_DV
S_ROWS = GLA_HEADS * GLA_DK
OFF_AQ = 0
OFF_AK = OFF_AQ + ATTN_W
OFF_AV = OFF_AK + KV_W
OFF_AG = OFF_AV + KV_W
OFF_GQ = OFF_AG + ATTN_W
OFF_GK = OFF_GQ + GQK_W
OFF_GV = OFF_GK + GQK_W
OFF_GG = OFF_GV + GV_W
OFF_LR = OFF_GG + GV_W
D_IN = OFF_LR + GLA_RANK
D_IN_PAD = OFF_LR + LANES
CAST_BLOCK = 1024
W_IN_ALLOC = -(-D_IN_PAD // CAST_BLOCK) * CAST_BLOCK

PROMPT_TILE = 512
SAMPLE_SEQS = 8
V7X_VMEM_BYTES = 64 * 1024 * 1024
VMEM_LIMIT = V7X_VMEM_BYTES * 3 // 4
STEP_ORDER = "CPCPPCPPPCPP"
CHUNK_GROUP = 8
FINISH_ROWS = 256

NEG = -1e30
LOG2E = 1.4426950408889634

Slot = collections.namedtuple("Slot", "qs kv ag qin kog kdec dec gv gg")


def _slot_shapes(rows, kv_rows, n_chunks):
    bf, f32 = jnp.bfloat16, jnp.float32
    return ([pltpu.VMEM((rows, 2 * ATTN_W), bf)] + [pltpu.VMEM((kv_rows, LANES), bf) for _ in range(6)]
            + [pltpu.VMEM((rows, ATTN_W), f32), pltpu.VMEM((rows, 2 * GQK_W), bf), pltpu.VMEM((rows, GQK_W), bf),
               pltpu.VMEM((rows, GQK_W), f32), pltpu.VMEM((n_chunks * SUBLANES, GQK_W), f32),
               pltpu.VMEM((rows, GV_W), bf), pltpu.VMEM((rows, GV_W), f32)])


N_SLOT_REFS = 14


def _make_slot(refs):
    return Slot(refs[0], tuple(refs[1:7]), *refs[7:14])


def _dot(a, b):
    return jnp.dot(a, b, preferred_element_type=jnp.float32)


def _dot_nt(a, b):
    return lax.dot_general(a, b, (((1,), (1,)), ((), ())), preferred_element_type=jnp.float32)


def _bf(x):
    return x.astype(jnp.bfloat16)


def _lane_masks():
    lane = lax.broadcasted_iota(jnp.int32, (1, LANES), 1)
    return lane < HEAD_DIM, lane >= HEAD_DIM


def _silu(x):
    return x * (1.0 / (1.0 + jnp.exp(-x)))


def _log_sigmoid_scaled(x, scale):
    soft = jnp.log2(1.0 + jnp.exp2(jnp.abs(x) * -LOG2E))
    return jnp.minimum(x, 0.0) * scale - soft * (scale / LOG2E)


def _rms_scale(x):
    return lax.rsqrt(jnp.mean(x * x, axis=-1, keepdims=True) + NORM_EPS)


def _chunk_cumsum(g, chunk):
    row = lax.broadcasted_iota(jnp.int32, g.shape, 0) & (chunk - 1)
    s = 1
    while s < chunk:
        g = g + jnp.where(row >= s, pltpu.roll(g, s, 0), 0.0)
        s *= 2
    return g


def _rotary(xcol, cos_t, sin_dn, sin_up):
    return (xcol * cos_t + pltpu.roll(xcol, ROT_DIM // 2, 1) * sin_dn
            + pltpu.roll(xcol, LANES - ROT_DIM // 2, 1) * sin_up)


def _rotary_lanes(cos_off, sin_off, cos_base, sin_base):
    d = lax.broadcasted_iota(jnp.int32, (1, LANES), 1) & (HEAD_DIM - 1)
    half = ROT_DIM // 2
    rot, up, dn = d < ROT_DIM, d < half, (d >= half) & (d < ROT_DIM)
    cb, sb = cos_base[0:1, :], sin_base[0:1, :]
    co, so = cos_off[...], sin_off[...]
    cos_t = jnp.where(rot, cb, 0.0) * co - jnp.where(rot, sb, 0.0) * so + jnp.where(rot, 0.0, 1.0)
    sin_dn = jnp.where(dn, sb, 0.0) * co + jnp.where(dn, cb, 0.0) * so
    sin_up = jnp.where(up, -sb, 0.0) * co + jnp.where(up, -cb, 0.0) * so
    return cos_t, sin_dn, sin_up


def _kv_variants(k, v):
    m_lo, m_hi = _lane_masks()
    k_sw = pltpu.roll(k, HEAD_DIM, 1)
    v_sw = pltpu.roll(v, HEAD_DIM, 1)
    return (_bf(k), _bf(k_sw),
            _bf(jnp.where(m_lo, v, 0.0)), _bf(jnp.where(m_hi, v_sw, 0.0)),
            _bf(jnp.where(m_lo, v_sw, 0.0)), _bf(jnp.where(m_hi, v, 0.0)))


def _store_halves(ref, j, width, col):
    m_lo, m_hi = _lane_masks()
    ref[:, j * LANES:(j + 1) * LANES] = _bf(jnp.where(m_lo, col, 0.0))
    ref[:, width + j * LANES:width + (j + 1) * LANES] = _bf(jnp.where(m_hi, col, 0.0))


def _project_stages(x, tabs, wpre_ref, win_ref, wup_ref, bgk_ref, chunk, slot, store_kv):
    rows = x.shape[0]
    h = _bf(x * _rms_scale(x) * wpre_ref[...])
    cos_t, sin_dn, sin_up = _rotary_lanes(*tabs)

    def seg(off, width):
        return _dot(h, win_ref[:, off:off + width])

    yield
    q = seg(OFF_AQ, ATTN_W)
    for j in range(ATTN_W // LANES):
        ln = slice(j * LANES, (j + 1) * LANES)
        _store_halves(slot.qs, j, ATTN_W, _rotary(q[:, ln], cos_t, sin_dn, sin_up) * (LOG2E * HEAD_DIM ** -0.5))
    yield
    kv = seg(OFF_AK, 2 * KV_W)
    store_kv(_rotary(kv[:, :KV_W], cos_t, sin_dn, sin_up), kv[:, KV_W:])
    yield
    slot.ag[...] = _silu(seg(OFF_AG, ATTN_W))
    yield
    lr = _bf(seg(OFF_LR, LANES))
    log_a = _log_sigmoid_scaled(_dot(lr, wup_ref[...]) + bgk_ref[...], 1.0 / GATE_TAU)
    b = _chunk_cumsum(log_a, chunk)
    lasts = [b[r + chunk - 1:r + chunk] for r in range(0, rows, chunk)]
    for i, bl in enumerate(lasts):
        slot.dec[i * SUBLANES:(i + 1) * SUBLANES, :] = jnp.broadcast_to(jnp.exp(bl), (SUBLANES, GQK_W))
    b_last = jnp.concatenate([jnp.broadcast_to(bl, (chunk, GQK_W)) for bl in lasts], axis=0)
    gqk = seg(OFF_GQ, 2 * GQK_W)
    q_in = gqk[:, :GQK_W] * (GLA_DK ** -0.5) * jnp.exp(b)
    for j in range(GQK_W // LANES):
        _store_halves(slot.qin, j, GQK_W, q_in[:, j * LANES:(j + 1) * LANES])
    yield
    gk = gqk[:, GQK_W:]
    slot.kog[...] = _bf(gk * jnp.exp(-b))
    slot.kdec[...] = gk * jnp.exp(b_last - b)
    yield
    slot.gv[...] = _bf(seg(OFF_GV, GV_W))
    yield
    slot.gg[...] = _silu(seg(OFF_GG, GV_W))


def _attend_stages(slot, rows, kv_win, sinks_ref, n_invalid, emit):
    c = rows.stop - rows.start
    lo = [slot.qs[rows, j * LANES:(j + 1) * LANES] for j in range(ATTN_W // LANES)]
    hi = [slot.qs[rows, ATTN_W + j * LANES:ATTN_W + (j + 1) * LANES] for j in range(ATTN_W // LANES)]
    k_a, k_b = kv_win(0), kv_win(1)
    n_hi = k_a.shape[0] - LANES
    zrows = jnp.zeros((LANES - n_hi, LANES), jnp.bfloat16)
    ext = lambda a: jnp.concatenate([a, zrows], axis=0)
    s_a = _dot_nt(jnp.concatenate([lo[0], lo[1], hi[2], hi[3]], axis=0), ext(k_a))
    s_b = _dot_nt(jnp.concatenate([hi[0], hi[1], lo[2], lo[3]], axis=0), ext(k_b))
    yield
    lane = lax.broadcasted_iota(jnp.int32, (1, LANES), 1)
    where = {0: (s_a, 0), 2: (s_a, 1), 5: (s_a, 2), 7: (s_a, 3),
             1: (s_b, 0), 3: (s_b, 1), 4: (s_b, 2), 6: (s_b, 3)}
    probs, rdenom = {}, {}
    for h in range(N_HEADS):
        src, blk = where[h]
        s = src[blk * c:(blk + 1) * c]
        pad_row = jnp.where(lane == n_hi, sinks_ref[0, h] * LOG2E, NEG)
        s_lo = s[:, :LANES]
        s_hi = jnp.where(lane >= n_hi, pad_row, s[:, LANES:])
        if n_invalid is not None:
            s_lo = jnp.where(lane < n_invalid, NEG, s_lo)
        m = jnp.max(jnp.maximum(s_lo, s_hi), axis=-1, keepdims=True)
        p_lo = jnp.exp2(s_lo - m)
        p_hi = jnp.exp2(s_hi - m)
        rdenom[h] = 1.0 / jnp.sum(p_lo + p_hi, axis=-1, keepdims=True)
        probs[h] = _bf(jnp.concatenate([p_lo, p_hi], axis=1))
    outs = {}
    for g in range(N_KV_HEADS):
        for par in range(2):
            ha, hb = g * GQA_GROUP + par, g * GQA_GROUP + par + 2
            o2 = _dot(jnp.concatenate([probs[ha], probs[hb]], axis=0), ext(kv_win(2 + 2 * g + par)))
            outs[ha] = o2[:c] * rdenom[ha]
            outs[hb] = o2[c:] * rdenom[hb]
    yield
    emit(jnp.concatenate([outs[2 * j] + outs[2 * j + 1] for j in range(4)], axis=1))


def _gla_stages(slot, r0, c, state, gnw_ref, mix_ref):
    rows = slice(r0, r0 + c)
    ci0 = (r0 // c) * SUBLANES
    lhs, a2, uw, dcol, vw = [], [], [], [], []
    for p in range(GLA_HEADS // 2):
        ln = slice(p * LANES, (p + 1) * LANES)
        vwide = slot.gv[rows, p * 2 * GLA_DV:(p + 1) * 2 * GLA_DV]
        lhs.append(jnp.concatenate([slot.qin[rows, ln], slot.qin[rows, GQK_W + p * LANES:GQK_W + (p + 1) * LANES]],
                                   axis=0))
        a2.append(_dot_nt(lhs[p], slot.kog[rows, ln]))
        uw.append(_dot(_bf(slot.kdec[rows, ln].T), vwide))
        dcol.append(jnp.broadcast_to(slot.dec[ci0:ci0 + 1, ln], (LANES, LANES)).T)
        vw.append(vwide)
    yield
    outs = []
    ri = lax.broadcasted_iota(jnp.int32, a2[0].shape, 0) & (c - 1)
    ci = lax.broadcasted_iota(jnp.int32, a2[0].shape, 1)
    for p in range(GLA_HEADS // 2):
        am = _bf(jnp.where(ri >= ci, a2[p], 0.0))
        sb = _bf(state[p])
        for i in range(2):
            hr = slice(i * c, (i + 1) * c)
            outs.append(_dot(jnp.concatenate([lhs[p][hr], am[hr]], axis=1),
                             jnp.concatenate([sb, vw[p][:, i * GLA_DV:(i + 1) * GLA_DV]], axis=0)))
        u_pair = jnp.concatenate([uw[p][:GLA_DK, :GLA_DV], uw[p][GLA_DK:, GLA_DV:]], axis=0)
        state[p] = dcol[p] * state[p] + u_pair
    yield
    gnw = gnw_ref[...]
    for hh, o in enumerate(outs):
        y = o * _rms_scale(o) * gnw * slot.gg[rows, hh * GLA_DV:(hh + 1) * GLA_DV]
        mix_ref[rows, ATTN_W + hh * GLA_DV:ATTN_W + (hh + 1) * GLA_DV] = _bf(y)


def _lockstep(gens):
    live = list(gens)
    while live:
        for g in list(live):
            try:
                next(g)
            except StopIteration:
                live.remove(g)
        yield


def _chunk_stages(slot, chunk_args, c, state_of, sinks_ref, gnw_ref, mix_ref):
    gens = []
    for i, (r0, kv_win, n_invalid) in enumerate(chunk_args):
        rows = slice(r0, r0 + c)

        def emit(attn, rows=rows):
            mix_ref[rows, 0:ATTN_W] = _bf(attn * slot.ag[rows, :])

        gens.append(_attend_stages(slot, rows, kv_win, sinks_ref, n_invalid, emit))
        gens.append(_gla_stages(slot, r0, c, state_of(i), gnw_ref, mix_ref))
    for g in range(0, len(gens), 2 * CHUNK_GROUP):
        yield from _lockstep(gens[g:g + 2 * CHUNK_GROUP])


def _finish(x, mix_ref, wout_ref, wpost_ref):
    mix = _dot(mix_ref[...], wout_ref[...])
    return x + mix * _rms_scale(mix) * wpost_ref[...]


def _run_order(order, p_gen, c_gen):
    for ch in order:
        next(p_gen if ch == "P" else c_gen, None)
    for g in (p_gen, c_gen):
        for _ in g:
            pass


def _pipeline_step(n_tiles, slots, body, first_step_init):
    s = pl.program_id(0)

    @pl.when(s == 0)
    def _():
        first_step_init()
        body(slots[0], slots[1], True, False)

    @pl.when(s == n_tiles)
    def _():
        body(slots[n_tiles % 2], slots[1 - n_tiles % 2], False, True)

    middle = (s > 0) & (s < n_tiles)

    @pl.when(middle & (s % 2 == 0))
    def _():
        body(slots[0], slots[1], True, True)

    @pl.when(middle & (s % 2 == 1))
    def _():
        body(slots[1], slots[0], True, True)


def _prompt_kernel(nt, n_tiles, sinks_ref, xp_ref, xc_ref, coff_ref, soff_ref, cbase_ref, sbase_ref, wpre_ref, win_ref,
                   wup_ref, bgk_ref, gnw_ref, wout_ref, wpost_ref,
                   y_ref, kout_ref, vout_ref, sout_ref, *scratch):
    slots = (_make_slot(scratch[:N_SLOT_REFS]), _make_slot(scratch[N_SLOT_REFS:2 * N_SLOT_REFS]))
    s_ref, mix_ref, woutb_ref = scratch[2 * N_SLOT_REFS:]
    tt = xp_ref.shape[1]
    t_c = jnp.maximum(pl.program_id(0) - 1, 0) % nt

    def body(slot_p, slot_c, project, chunks):
        def store_kv(k, v):
            kout_ref[0] = k[tt - WINDOW:]
            vout_ref[0] = v[tt - WINDOW:]
            for r, rc, val in zip(slot_p.kv, slot_c.kv, _kv_variants(k, v)):
                r[WINDOW:WINDOW + tt, :] = val
                r[0:WINDOW, :] = rc[tt:tt + WINDOW, :]

        p_gen = (_project_stages(xp_ref[0], (coff_ref, soff_ref, cbase_ref, sbase_ref), wpre_ref, win_ref, wup_ref,
                                 bgk_ref, CHUNK, slot_p, store_kv) if project else iter(()))

        def c_stages():
            keep = t_c != 0
            state = [jnp.where(keep, s_ref[p * LANES:(p + 1) * LANES, :], 0.0) for p in range(GLA_HEADS // 2)]
            win = WINDOW + CHUNK
            chunk_args = []
            for r0 in range(0, tt, CHUNK):
                n_invalid = WINDOW - (t_c * tt + r0) if r0 < WINDOW else None
                chunk_args.append((r0, lambda i, r0=r0: slot_c.kv[i][r0:r0 + win, :], n_invalid))
            yield from _chunk_stages(slot_c, chunk_args, CHUNK, lambda i: state, sinks_ref, gnw_ref, mix_ref)
            for r0 in range(0, tt, FINISH_ROWS):
                rows = slice(r0, r0 + FINISH_ROWS)
                y_ref[0, rows, :] = _finish(xc_ref[0, rows, :], mix_ref.at[rows, :], woutb_ref, wpost_ref)
            for p in range(GLA_HEADS // 2):
                s_ref[p * LANES:(p + 1) * LANES, :] = state[p]
                sout_ref[0, p * LANES:(p + 1) * LANES, :] = state[p]

        _run_order(STEP_ORDER if project and chunks else "", p_gen, c_stages() if chunks else iter(()))

    def first_step_init():
        for r in slots[1].kv + (s_ref,):
            r[...] = jnp.zeros_like(r)
        woutb_ref[...] = _bf(wout_ref[...])

    _pipeline_step(n_tiles, slots, body, first_step_init)


def _sample_kernel(sinks_ref, x_ref, coff_ref, soff_ref, cbase_ref, sbase_ref, ck_ref, cv_ref, s0_ref, wpre_ref,
                   win_ref, wup_ref, bgk_ref, gnw_ref, wout_ref, wpost_ref,
                   y_ref, kout_ref, vout_ref, sout_ref, *scratch):
    slot = _make_slot(scratch[:N_SLOT_REFS])
    mix_ref, woutb_ref = scratch[N_SLOT_REFS:]
    nb = ck_ref.shape[0]

    @pl.when(pl.program_id(0) == 0)
    def _():
        woutb_ref[...] = _bf(wout_ref[...])

    c = x_ref.shape[0] // nb
    win = WINDOW + c

    def store_kv(k, v):
        kout_ref[...] = k
        vout_ref[...] = v
        for i in range(nb):
            rows = slice(i * c, (i + 1) * c)
            kv = _kv_variants(jnp.concatenate([ck_ref[i], k[rows]], axis=0),
                              jnp.concatenate([cv_ref[i], v[rows]], axis=0))
            for r, val in zip(slot.kv, kv):
                r[i * win:(i + 1) * win, :] = val

    x = x_ref[...]
    for _ in _project_stages(x, (coff_ref, soff_ref, cbase_ref, sbase_ref), wpre_ref, win_ref, wup_ref, bgk_ref, c, slot,
                             store_kv):
        pass
    states = [[s0_ref[i, p * LANES:(p + 1) * LANES, :] for p in range(GLA_HEADS // 2)] for i in range(nb)]
    chunk_args = [(i * c, lambda j, i=i: slot.kv[j][i * win:(i + 1) * win, :], None) for i in range(nb)]
    for _ in _chunk_stages(slot, chunk_args, c, lambda i: states[i], sinks_ref, gnw_ref, mix_ref):
        pass
    for i in range(nb):
        for p in range(GLA_HEADS // 2):
            sout_ref[i, p * LANES:(p + 1) * LANES, :] = states[i][p]
    y_ref[...] = _finish(x, mix_ref, woutb_ref, wpost_ref)


def _rotary_angles(offsets, bases):
    half = ROT_DIM // 2
    d = jnp.arange(LANES) % HEAD_DIM
    inv = ROPE_THETA ** (-(d % half).astype(jnp.float32) * (2.0 / ROT_DIM))
    pos = jnp.concatenate([offsets, jnp.repeat(bases, SUBLANES)]).astype(jnp.float32)
    ang = pos[:, None] * inv[None, :]
    return jnp.cos(ang), jnp.sin(ang)


def _cast_w_in_kernel(wt_ref, out_ref):
    row = lax.broadcasted_iota(jnp.int32, wt_ref.shape, 0) + pl.program_id(0) * CAST_BLOCK
    out_ref[...] = _bf(jnp.where(row < D_IN, wt_ref[...], 0.0).T)


def _cast_w_in(w_in_t):
    return pl.pallas_call(
        _cast_w_in_kernel,
        grid=(W_IN_ALLOC // CAST_BLOCK,),
        in_specs=[pl.BlockSpec((CAST_BLOCK, D_MODEL), lambda i: (i, 0))],
        out_specs=pl.BlockSpec((D_MODEL, CAST_BLOCK), lambda i: (0, i)),
        out_shape=jax.ShapeDtypeStruct((D_MODEL, W_IN_ALLOC), jnp.bfloat16),
        compiler_params=pltpu.CompilerParams(dimension_semantics=("arbitrary",)),
        name="cast_w_in",
    )(w_in_t)


def _const_spec(shape):
    zeros = (0,) * len(shape)
    return pl.BlockSpec(shape, lambda i: zeros, pipeline_mode=pl.Buffered(1))


def kernel(x_prompt, x_sample, cache_k, cache_v, state_gla, norm_pre_w, w_in, attn_sinks, w_gk_up, b_gk,
           gla_norm_w, w_out, norm_post_w):
    bsz, t_p, _ = x_prompt.shape
    dec_b, t_s, _ = x_sample.shape
    assert w_in.shape[0] == 1, "single layer"
    assert t_p % PROMPT_TILE == 0 and PROMPT_TILE % CHUNK == 0 and PROMPT_TILE >= WINDOW
    assert dec_b % SAMPLE_SEQS == 0 and t_s & (t_s - 1) == 0

    win = _cast_w_in(w_in[0].T)
    wout = w_out[0]
    wup = _bf(jnp.pad(w_gk_up[0], ((0, LANES - GLA_RANK), (0, 0))))
    wpre = norm_pre_w[0][None, :]
    wpost = norm_post_w[0][None, :]
    bgk = b_gk[0][None, :]
    gnw = gla_norm_w[0][None, :]
    sinks = attn_sinks[0][None, :]
    smem = pl.BlockSpec(memory_space=pltpu.SMEM)

    weights_specs = [
        _const_spec((1, D_MODEL)), _const_spec((D_MODEL, D_IN_PAD)), _const_spec((LANES, GQK_W)),
        _const_spec((1, GQK_W)), _const_spec((1, GLA_DV)), _const_spec((D_MODEL, D_MODEL)),
        _const_spec((1, D_MODEL))]

    tt = PROMPT_TILE
    nt = t_p // tt
    n_tiles = bsz * nt
    nb = SAMPLE_SEQS
    rows = nb * t_s
    assert tt % rows == 0 and (tt + rows) % SUBLANES == 0
    cos_all, sin_all = _rotary_angles(jnp.concatenate([jnp.arange(tt), jnp.tile(jnp.arange(t_s), nb)]),
                                      jnp.concatenate([jnp.arange(nt) * tt, jnp.array([PAST_LEN])]))
    base_blk = (tt + rows) // SUBLANES

    def tile_p(s):
        return jnp.minimum(s, n_tiles - 1)

    def tile_c(s):
        return jnp.maximum(s - 1, 0)

    off_spec = pl.BlockSpec((tt, LANES), lambda s: (0, 0))
    base_spec = pl.BlockSpec((SUBLANES, LANES), lambda s: (base_blk + tile_p(s) % nt, 0))
    x_spec = lambda tile: pl.BlockSpec((1, tt, D_MODEL), lambda s: (tile(s) // nt, tile(s) % nt, 0))
    seq_spec = lambda tile, d1, d2: pl.BlockSpec((1, d1, d2), lambda s: (tile(s) // nt, 0, 0))
    y_p, k_p, v_p, s_p = pl.pallas_call(
        functools.partial(_prompt_kernel, nt, n_tiles),
        grid=(n_tiles + 1,),
        in_specs=[smem, x_spec(tile_p), x_spec(tile_c), off_spec, off_spec, base_spec, base_spec] + weights_specs,
        out_specs=[x_spec(tile_c), seq_spec(tile_p, WINDOW, KV_W), seq_spec(tile_p, WINDOW, KV_W),
                   seq_spec(tile_c, S_ROWS, GLA_DV)],
        out_shape=[jax.ShapeDtypeStruct((bsz, t_p, D_MODEL), jnp.float32),
                   jax.ShapeDtypeStruct((bsz, WINDOW, KV_W), jnp.float32),
                   jax.ShapeDtypeStruct((bsz, WINDOW, KV_W), jnp.float32),
                   jax.ShapeDtypeStruct((bsz, S_ROWS, GLA_DV), jnp.float32)],
        scratch_shapes=_slot_shapes(tt, WINDOW + tt, tt // CHUNK) * 2
        + [pltpu.VMEM((S_ROWS, GLA_DV), jnp.float32), pltpu.VMEM((tt, D_MODEL), jnp.bfloat16),
           pltpu.VMEM((D_MODEL, D_MODEL), jnp.bfloat16)],
        compiler_params=pltpu.CompilerParams(dimension_semantics=("arbitrary",), vmem_limit_bytes=VMEM_LIMIT),
        name="prompt_layer",
    )(sinks, x_prompt, x_prompt, cos_all, sin_all, cos_all, sin_all, wpre, win, wup, bgk, gnw, wout, wpost)

    off_spec = pl.BlockSpec((rows, LANES), lambda i: (tt // rows, 0))
    base_spec = pl.BlockSpec((SUBLANES, LANES), lambda i: (base_blk + nt, 0))
    seq3 = lambda d1, d2: pl.BlockSpec((nb, d1, d2), lambda i: (i, 0, 0))
    tok = lambda d: pl.BlockSpec((rows, d), lambda i: (i, 0))
    y_s, k_s, v_s, s_s = pl.pallas_call(
        _sample_kernel,
        grid=(dec_b // nb,),
        in_specs=[smem, tok(D_MODEL), off_spec, off_spec, base_spec, base_spec, seq3(WINDOW, KV_W), seq3(WINDOW, KV_W),
                  seq3(S_ROWS, GLA_DV)] + weights_specs,
        out_specs=[tok(D_MODEL), tok(KV_W), tok(KV_W), seq3(S_ROWS, GLA_DV)],
        out_shape=[jax.ShapeDtypeStruct((dec_b * t_s, D_MODEL), jnp.float32),
                   jax.ShapeDtypeStruct((dec_b * t_s, KV_W), jnp.float32),
                   jax.ShapeDtypeStruct((dec_b * t_s, KV_W), jnp.float32),
                   jax.ShapeDtypeStruct((dec_b, S_ROWS, GLA_DV), jnp.float32)],
        scratch_shapes=_slot_shapes(rows, nb * (WINDOW + t_s), nb)
        + [pltpu.VMEM((rows, D_MODEL), jnp.bfloat16), pltpu.VMEM((D_MODEL, D_MODEL), jnp.bfloat16)],
        compiler_params=pltpu.CompilerParams(dimension_semantics=("arbitrary",), vmem_limit_bytes=VMEM_LIMIT),
        name="sample_layer",
    )(sinks, x_sample.reshape(dec_b * t_s, D_MODEL), cos_all, sin_all, cos_all, sin_all,
      cache_k[0].reshape(dec_b, WINDOW, KV_W), cache_v[0].reshape(dec_b, WINDOW, KV_W),
      state_gla[0].reshape(dec_b, S_ROWS, GLA_DV), wpre, win, wup, bgk, gnw, wout, wpost)

    kv5 = lambda a, b, t: a.reshape(1, b, t, N_KV_HEADS, HEAD_DIM)
    st5 = lambda a, b: a.reshape(1, b, GLA_HEADS, GLA_DK, GLA_DV)
    return (y_p, y_s.reshape(dec_b, t_s, D_MODEL),
            kv5(k_p, bsz, WINDOW), kv5(v_p, bsz, WINDOW), st5(s_p, bsz),
            kv5(k_s, dec_b, t_s), kv5(v_s, dec_b, t_s), st5(s_s, dec_b))
```

```python
import collections
import functools

import jax
import jax.numpy as jnp
from jax import lax
from jax.experimental import pallas as pl
from jax.experimental.pallas import tpu as pltpu

D_MODEL = 1024
CHUNK = 64
WINDOW = 128
HEAD_DIM = 64
N_HEADS = 8
N_KV_HEADS = 2
GQA_GROUP = N_HEADS // N_KV_HEADS
ROT_DIM = 16
ROPE_THETA = 500000.0
GLA_HEADS = 4
GLA_DV = 128
GLA_DK = 64
GLA_RANK = 16
GATE_TAU = 16.0
NORM_EPS = 1e-6
PAST_LEN = 4096

LANES = 128
SUBLANES = 8
ATTN_W = N_HEADS * HEAD_DIM
KV_W = N_KV_HEADS * HEAD_DIM
GQK_W = GLA_HEADS * GLA_DK
GV_W = GLA_HEADS * GLA_DV
S_ROWS = GLA_HEADS * GLA_DK
OFF_AQ = 0
OFF_AK = OFF_AQ + ATTN_W
OFF_AV = OFF_AK + KV_W
OFF_AG = OFF_AV + KV_W
OFF_GQ = OFF_AG + ATTN_W
OFF_GK = OFF_GQ + GQK_W
OFF_GV = OFF_GK + GQK_W
OFF_GG = OFF_GV + GV_W
OFF_LR = OFF_GG + GV_W
D_IN = OFF_LR + GLA_RANK
D_IN_PAD = OFF_LR + LANES
CAST_BLOCK = 1024
W_IN_ALLOC = -(-D_IN_PAD // CAST_BLOCK) * CAST_BLOCK

PROMPT_TILE = 512
SAMPLE_SEQS = 8
V7X_VMEM_BYTES = 64 * 1024 * 1024
VMEM_LIMIT = V7X_VMEM_BYTES * 3 // 4
STEP_ORDER = "CPCPPCPPPCPP"
CHUNK_GROUP = 8

NEG = -1e30
LOG2E = 1.4426950408889634

Slot = collections.namedtuple("Slot", "qs kv ag qin kog kdec dec gv gg")


def _slot_shapes(rows, kv_rows, n_chunks):
    bf, f32 = jnp.bfloat16, jnp.float32
    return ([pltpu.VMEM((rows, 2 * ATTN_W), bf)] + [pltpu.VMEM((kv_rows, LANES), bf) for _ in range(6)]
            + [pltpu.VMEM((rows, ATTN_W), f32), pltpu.VMEM((rows, 2 * GQK_W), bf), pltpu.VMEM((rows, GQK_W), bf),
               pltpu.VMEM((rows, GQK_W), f32), pltpu.VMEM((n_chunks * SUBLANES, GQK_W), f32),
               pltpu.VMEM((rows, GV_W), bf), pltpu.VMEM((rows, GV_W), f32)])


N_SLOT_REFS = 14


def _make_slot(refs):
    return Slot(refs[0], tuple(refs[1:7]), *refs[7:14])


def _dot(a, b):
    return jnp.dot(a, b, preferred_element_type=jnp.float32)


def _dot_nt(a, b):
    return lax.dot_general(a, b, (((1,), (1,)), ((), ())), preferred_element_type=jnp.float32)


def _bf(x):
    return x.astype(jnp.bfloat16)


def _lane_masks():
    lane = lax.broadcasted_iota(jnp.int32, (1, LANES), 1)
    return lane < HEAD_DIM, lane >= HEAD_DIM


def _silu(x):
    return x * (1.0 / (1.0 + jnp.exp(-x)))


def _log_sigmoid_scaled(x, scale):
    soft = jnp.log2(1.0 + jnp.exp2(jnp.abs(x) * -LOG2E))
    return jnp.minimum(x, 0.0) * scale - soft * (scale / LOG2E)


def _rms_scale(x):
    return lax.rsqrt(jnp.mean(x * x, axis=-1, keepdims=True) + NORM_EPS)


def _chunk_cumsum(g, chunk):
    row = lax.broadcasted_iota(jnp.int32, g.shape, 0) & (chunk - 1)
    s = 1
    while s < chunk:
        g = g + jnp.where(row >= s, pltpu.roll(g, s, 0), 0.0)
        s *= 2
    return g


def _rotary(xcol, cos_t, sin_dn, sin_up):
    return (xcol * cos_t + pltpu.roll(xcol, ROT_DIM // 2, 1) * sin_dn
            + pltpu.roll(xcol, LANES - ROT_DIM // 2, 1) * sin_up)


def _rotary_lanes(cos_off, sin_off, cos_base, sin_base):
    d = lax.broadcasted_iota(jnp.int32, (1, LANES), 1) & (HEAD_DIM - 1)
    half = ROT_DIM // 2
    rot, up, dn = d < ROT_DIM, d < half, (d >= half) & (d < ROT_DIM)
    cb, sb = cos_base[0:1, :], sin_base[0:1, :]
    co, so = cos_off[...], sin_off[...]
    cos_t = jnp.where(rot, cb, 0.0) * co - jnp.where(rot, sb, 0.0) * so + jnp.where(rot, 0.0, 1.0)
    sin_dn = jnp.where(dn, sb, 0.0) * co + jnp.where(dn, cb, 0.0) * so
    sin_up = jnp.where(up, -sb, 0.0) * co + jnp.where(up, -cb, 0.0) * so
    return cos_t, sin_dn, sin_up


def _kv_variants(k, v):
    m_lo, m_hi = _lane_masks()
    k_sw = pltpu.roll(k, HEAD_DIM, 1)
    v_sw = pltpu.roll(v, HEAD_DIM, 1)
    return (_bf(k), _bf(k_sw),
            _bf(jnp.where(m_lo, v, 0.0)), _bf(jnp.where(m_hi, v_sw, 0.0)),
            _bf(jnp.where(m_lo, v_sw, 0.0)), _bf(jnp.where(m_hi, v, 0.0)))


def _store_halves(ref, j, width, col):
    m_lo, m_hi = _lane_masks()
    ref[:, j * LANES:(j + 1) * LANES] = _bf(jnp.where(m_lo, col, 0.0))
    ref[:, width + j * LANES:width + (j + 1) * LANES] = _bf(jnp.where(m_hi, col, 0.0))


def _project_stages(x, tabs, wpre_ref, win_ref, wup_ref, bgk_ref, chunk, slot, store_kv):
    rows = x.shape[0]
    h = _bf(x * _rms_scale(x) * wpre_ref[...])
    cos_t, sin_dn, sin_up = _rotary_lanes(*tabs)

    def seg(off, width):
        return _dot(h, win_ref[:, off:off + width])

    yield
    q = seg(OFF_AQ, ATTN_W)
    for j in range(ATTN_W // LANES):
        ln = slice(j * LANES, (j + 1) * LANES)
        _store_halves(slot.qs, j, ATTN_W, _rotary(q[:, ln], cos_t, sin_dn, sin_up) * (LOG2E * HEAD_DIM ** -0.5))
    yield
    kv = seg(OFF_AK, 2 * KV_W)
    store_kv(_rotary(kv[:, :KV_W], cos_t, sin_dn, sin_up), kv[:, KV_W:])
    yield
    slot.ag[...] = _silu(seg(OFF_AG, ATTN_W))
    yield
    lr = _bf(seg(OFF_LR, LANES))
    log_a = _log_sigmoid_scaled(_dot(lr, wup_ref[...]) + bgk_ref[...], 1.0 / GATE_TAU)
    b = _chunk_cumsum(log_a, chunk)
    lasts = [b[r + chunk - 1:r + chunk] for r in range(0, rows, chunk)]
    for i, bl in enumerate(lasts):
        slot.dec[i * SUBLANES:(i + 1) * SUBLANES, :] = jnp.broadcast_to(jnp.exp(bl), (SUBLANES, GQK_W))
    b_last = jnp.concatenate([jnp.broadcast_to(bl, (chunk, GQK_W)) for bl in lasts], axis=0)
    gqk = seg(OFF_GQ, 2 * GQK_W)
    q_in = gqk[:, :GQK_W] * (GLA_DK ** -0.5) * jnp.exp(b)
    for j in range(GQK_W // LANES):
        _store_halves(slot.qin, j, GQK_W, q_in[:, j * LANES:(j + 1) * LANES])
    yield
    gk = gqk[:, GQK_W:]
    slot.kog[...] = _bf(gk * jnp.exp(-b))
    slot.kdec[...] = gk * jnp.exp(b_last - b)
    yield
    slot.gv[...] = _bf(seg(OFF_GV, GV_W))
    yield
    slot.gg[...] = _silu(seg(OFF_GG, GV_W))


def _attend_stages(slot, rows, kv_win, sinks_ref, n_invalid, emit):
    c = rows.stop - rows.start
    lo = [slot.qs[rows, j * LANES:(j + 1) * LANES] for j in range(ATTN_W // LANES)]
    hi = [slot.qs[rows, ATTN_W + j * LANES:ATTN_W + (j + 1) * LANES] for j in range(ATTN_W // LANES)]
    k_a, k_b = kv_win(0), kv_win(1)
    n_hi = k_a.shape[0] - LANES
    zrows = jnp.zeros((LANES - n_hi, LANES), jnp.bfloat16)
    ext = lambda a: jnp.concatenate([a, zrows], axis=0)
    s_a = _dot_nt(jnp.concatenate([lo[0], lo[1], hi[2], hi[3]], axis=0), ext(k_a))
    s_b = _dot_nt(jnp.concatenate([hi[0], hi[1], lo[2], lo[3]], axis=0), ext(k_b))
    yield
    lane = lax.broadcasted_iota(jnp.int32, (1, LANES), 1)
    where = {0: (s_a, 0), 2: (s_a, 1), 5: (s_a, 2), 7: (s_a, 3),
             1: (s_b, 0), 3: (s_b, 1), 4: (s_b, 2), 6: (s_b, 3)}
    probs, rdenom = {}, {}
    for h in range(N_HEADS):
        src, blk = where[h]
        s = src[blk * c:(blk + 1) * c]
        pad_row = jnp.where(lane == n_hi, sinks_ref[0, h] * LOG2E, NEG)
        s_lo = s[:, :LANES]
        s_hi = jnp.where(lane >= n_hi, pad_row, s[:, LANES:])
        if n_invalid is not None:
            s_lo = jnp.where(lane < n_invalid, NEG, s_lo)
        m = jnp.max(jnp.maximum(s_lo, s_hi), axis=-1, keepdims=True)
        p_lo = jnp.exp2(s_lo - m)
        p_hi = jnp.exp2(s_hi - m)
        rdenom[h] = 1.0 / jnp.sum(p_lo + p_hi, axis=-1, keepdims=True)
        probs[h] = _bf(jnp.concatenate([p_lo, p_hi], axis=1))
    outs = {}
    for g in range(N_KV_HEADS):
        for par in range(2):
            ha, hb = g * GQA_GROUP + par, g * GQA_GROUP + par + 2
            o2 = _dot(jnp.concatenate([probs[ha], probs[hb]], axis=0), ext(kv_win(2 + 2 * g + par)))
            outs[ha] = o2[:c] * rdenom[ha]
            outs[hb] = o2[c:] * rdenom[hb]
    yield
    emit(jnp.concatenate([outs[2 * j] + outs[2 * j + 1] for j in range(4)], axis=1))


def _gla_stages(slot, r0, c, state, gnw_ref, mix_ref):
    rows = slice(r0, r0 + c)
    ci0 = (r0 // c) * SUBLANES
    lhs, a2, uw, dcol, vw = [], [], [], [], []
    for p in range(GLA_HEADS // 2):
        ln = slice(p * LANES, (p + 1) * LANES)
        vwide = slot.gv[rows, p * 2 * GLA_DV:(p + 1) * 2 * GLA_DV]
        lhs.append(jnp.concatenate([slot.qin[rows, ln], slot.qin[rows, GQK_W + p * LANES:GQK_W + (p + 1) * LANES]],
                                   axis=0))
        a2.append(_dot_nt(lhs[p], slot.kog[rows, ln]))
        uw.append(_dot(_bf(slot.kdec[rows, ln].T), vwide))
        dcol.append(jnp.broadcast_to(slot.dec[ci0:ci0 + 1, ln], (LANES, LANES)).T)
        vw.append(vwide)
    yield
    outs = []
    ri = lax.broadcasted_iota(jnp.int32, a2[0].shape, 0) & (c - 1)
    ci = lax.broadcasted_iota(jnp.int32, a2[0].shape, 1)
    for p in range(GLA_HEADS // 2):
        am = _bf(jnp.where(ri >= ci, a2[p], 0.0))
        sb = _bf(state[p])
        for i in range(2):
            hr = slice(i * c, (i + 1) * c)
            outs.append(_dot(jnp.concatenate([lhs[p][hr], am[hr]], axis=1),
                             jnp.concatenate([sb, vw[p][:, i * GLA_DV:(i + 1) * GLA_DV]], axis=0)))
        u_pair = jnp.concatenate([uw[p][:GLA_DK, :GLA_DV], uw[p][GLA_DK:, GLA_DV:]], axis=0)
        state[p] = dcol[p] * state[p] + u_pair
    yield
    gnw = gnw_ref[...]
    for hh, o in enumerate(outs):
        y = o * _rms_scale(o) * gnw * slot.gg[rows, hh * GLA_DV:(hh + 1) * GLA_DV]
        mix_ref[rows, ATTN_W + hh * GLA_DV:ATTN_W + (hh + 1) * GLA_DV] = _bf(y)


def _lockstep(gens):
    live = list(gens)
    while live:
        for g in list(live):
            try:
                next(g)
            except StopIteration:
                live.remove(g)
        yield


def _chunk_stages(slot, chunk_args, c, state_of, sinks_ref, gnw_ref, mix_ref):
    gens = []
    for i, (r0, kv_win, n_invalid) in enumerate(chunk_args):
        rows = slice(r0, r0 + c)

        def emit(attn, rows=rows):
            mix_ref[rows, 0:ATTN_W] = _bf(attn * slot.ag[rows, :])

        gens.append(_attend_stages(slot, rows, kv_win, sinks_ref, n_invalid, emit))
        gens.append(_gla_stages(slot, r0, c, state_of(i), gnw_ref, mix_ref))
    for g in range(0, len(gens), 2 * CHUNK_GROUP):
        yield from _lockstep(gens[g:g + 2 * CHUNK_GROUP])


def _finish(x, mix_ref, wout_ref, wpost_ref):
    mix = _dot(mix_ref[...], wout_ref[...])
    return x + mix * _rms_scale(mix) * wpost_ref[...]


def _run_order(order, p_gen, c_gen):
    for ch in order:
        next(p_gen if ch == "P" else c_gen, None)
    for g in (p_gen, c_gen):
        for _ in g:
            pass


def _pipeline_step(n_tiles, slots, body, first_step_init):
    s = pl.program_id(0)

    @pl.when(s == 0)
    def _():
        first_step_init()
        body(slots[0], slots[1], True, False)

    @pl.when(s == n_tiles)
    def _():
        body(slots[n_tiles % 2], slots[1 - n_tiles % 2], False, True)

    middle = (s > 0) & (s < n_tiles)

    @pl.when(middle & (s % 2 == 0))
    def _():
        body(slots[0], slots[1], True, True)

    @pl.when(middle & (s % 2 == 1))
    def _():
        body(slots[1], slots[0], True, True)


def _prompt_kernel(nt, n_tiles, sinks_ref, xp_ref, xc_ref, coff_ref, soff_ref, cbase_ref, sbase_ref, wpre_ref, win_ref,
                   wup_ref, bgk_ref, gnw_ref, wout_ref, wpost_ref,
                   y_ref, kout_ref, vout_ref, sout_ref, *scratch):
    slots = (_make_slot(scratch[:N_SLOT_REFS]), _make_slot(scratch[N_SLOT_REFS:2 * N_SLOT_REFS]))
    s_ref, mix_ref, woutb_ref = scratch[2 * N_SLOT_REFS:]
    tt = xp_ref.shape[1]
    t_c = jnp.maximum(pl.program_id(0) - 1, 0) % nt

    def body(slot_p, slot_c, project, chunks):
        def store_kv(k, v):
            kout_ref[0] = k[tt - WINDOW:]
            vout_ref[0] = v[tt - WINDOW:]
            for r, rc, val in zip(slot_p.kv, slot_c.kv, _kv_variants(k, v)):
                r[WINDOW:WINDOW + tt, :] = val
                r[0:WINDOW, :] = rc[tt:tt + WINDOW, :]

        p_gen = (_project_stages(xp_ref[0], (coff_ref, soff_ref, cbase_ref, sbase_ref), wpre_ref, win_ref, wup_ref,
                                 bgk_ref, CHUNK, slot_p, store_kv) if project else iter(()))

        def c_stages():
            keep = t_c != 0
            state = [jnp.where(keep, s_ref[p * LANES:(p + 1) * LANES, :], 0.0) for p in range(GLA_HEADS // 2)]
            win = WINDOW + CHUNK
            chunk_args = []
            for r0 in range(0, tt, CHUNK):
                n_invalid = WINDOW - (t_c * tt + r0) if r0 < WINDOW else None
                chunk_args.append((r0, lambda i, r0=r0: slot_c.kv[i][r0:r0 + win, :], n_invalid))
            yield from _chunk_stages(slot_c, chunk_args, CHUNK, lambda i: state, sinks_ref, gnw_ref, mix_ref)
            y_ref[0] = _finish(xc_ref[0], mix_ref, woutb_ref, wpost_ref)
            for p in range(GLA_HEADS // 2):
                s_ref[p * LANES:(p + 1) * LANES, :] = state[p]
                sout_ref[0, p * LANES:(p + 1) * LANES, :] = state[p]

        _run_order(STEP_ORDER if project and chunks else "", p_gen, c_stages() if chunks else iter(()))

    def first_step_init():
        for r in slots[1].kv + (s_ref,):
            r[...] = jnp.zeros_like(r)
        woutb_ref[...] = _bf(wout_ref[...])

    _pipeline_step(n_tiles, slots, body, first_step_init)


def _sample_kernel(sinks_ref, x_ref, coff_ref, soff_ref, cbase_ref, sbase_ref, ck_ref, cv_ref, s0_ref, wpre_ref,
                   win_ref, wup_ref, bgk_ref, gnw_ref, wout_ref, wpost_ref,
                   y_ref, kout_ref, vout_ref, sout_ref, *scratch):
    slot = _make_slot(scratch[:N_SLOT_REFS])
    mix_ref, woutb_ref = scratch[N_SLOT_REFS:]
    nb = ck_ref.shape[0]

    @pl.when(pl.program_id(0) == 0)
    def _():
        woutb_ref[...] = _bf(wout_ref[...])

    c = x_ref.shape[0] // nb
    win = WINDOW + c

    def store_kv(k, v):
        kout_ref[...] = k
        vout_ref[...] = v
        for i in range(nb):
            rows = slice(i * c, (i + 1) * c)
            kv = _kv_variants(jnp.concatenate([ck_ref[i].T, k[rows]], axis=0),
                              jnp.concatenate([cv_ref[i].T, v[rows]], axis=0))
            for r, val in zip(slot.kv, kv):
                r[i * win:(i + 1) * win, :] = val

    x = x_ref[...]
    for _ in _project_stages(x, (coff_ref, soff_ref, cbase_ref, sbase_ref), wpre_ref, win_ref, wup_ref, bgk_ref, c, slot,
                             store_kv):
        pass
    states = [[s0_ref[i, p * LANES:(p + 1) * LANES, :] for p in range(GLA_HEADS // 2)] for i in range(nb)]
    chunk_args = [(i * c, lambda j, i=i: slot.kv[j][i * win:(i + 1) * win, :], None) for i in range(nb)]
    for _ in _chunk_stages(slot, chunk_args, c, lambda i: states[i], sinks_ref, gnw_ref, mix_ref):
        pass
    for i in range(nb):
        for p in range(GLA_HEADS // 2):
            sout_ref[i, p * LANES:(p + 1) * LANES, :] = states[i][p]
    y_ref[...] = _finish(x, mix_ref, woutb_ref, wpost_ref)


def _rotary_angles(offsets, bases):
    half = ROT_DIM // 2
    d = jnp.arange(LANES) % HEAD_DIM
    inv = ROPE_THETA ** (-(d % half).astype(jnp.float32) * (2.0 / ROT_DIM))
    pos = jnp.concatenate([offsets, jnp.repeat(bases, SUBLANES)]).astype(jnp.float32)
    ang = pos[:, None] * inv[None, :]
    return jnp.cos(ang), jnp.sin(ang)


def _cast_w_in_kernel(wt_ref, out_ref):
    row = lax.broadcasted_iota(jnp.int32, wt_ref.shape, 0) + pl.program_id(0) * CAST_BLOCK
    out_ref[...] = _bf(jnp.where(row < D_IN, wt_ref[...], 0.0).T)


def _cast_w_in(w_in_t):
    return pl.pallas_call(
        _cast_w_in_kernel,
        grid=(W_IN_ALLOC // CAST_BLOCK,),
        in_specs=[pl.BlockSpec((CAST_BLOCK, D_MODEL), lambda i: (i, 0))],
        out_specs=pl.BlockSpec((D_MODEL, CAST_BLOCK), lambda i: (0, i)),
        out_shape=jax.ShapeDtypeStruct((D_MODEL, W_IN_ALLOC), jnp.bfloat16),
        compiler_params=pltpu.CompilerParams(dimension_semantics=("arbitrary",)),
        name="cast_w_in",
    )(w_in_t)


def _const_spec(shape):
    zeros = (0,) * len(shape)
    return pl.BlockSpec(shape, lambda i: zeros, pipeline_mode=pl.Buffered(1))


def kernel(x_prompt, x_sample, cache_k, cache_v, state_gla, norm_pre_w, w_in, attn_sinks, w_gk_up, b_gk,
           gla_norm_w, w_out, norm_post_w):
    bsz, t_p, _ = x_prompt.shape
    dec_b, t_s, _ = x_sample.shape
    assert w_in.shape[0] == 1, "single layer"
    assert t_p % PROMPT_TILE == 0 and PROMPT_TILE % CHUNK == 0 and PROMPT_TILE >= WINDOW
    assert dec_b % SAMPLE_SEQS == 0 and t_s & (t_s - 1) == 0

    win = _cast_w_in(w_in[0].T)
    wout = w_out[0]
    wup = _bf(jnp.pad(w_gk_up[0], ((0, LANES - GLA_RANK), (0, 0))))
    wpre = norm_pre_w[0][None, :]
    wpost = norm_post_w[0][None, :]
    bgk = b_gk[0][None, :]
    gnw = gla_norm_w[0][None, :]
    sinks = attn_sinks[0][None, :]
    smem = pl.BlockSpec(memory_space=pltpu.SMEM)

    weights_specs = [
        _const_spec((1, D_MODEL)), _const_spec((D_MODEL, D_IN_PAD)), _const_spec((LANES, GQK_W)),
        _const_spec((1, GQK_W)), _const_spec((1, GLA_DV)), _const_spec((D_MODEL, D_MODEL)),
        _const_spec((1, D_MODEL))]

    tt = PROMPT_TILE
    nt = t_p // tt
    n_tiles = bsz * nt
    nb = SAMPLE_SEQS
    rows = nb * t_s
    assert tt % rows == 0 and (tt + rows) % SUBLANES == 0
    cos_all, sin_all = _rotary_angles(jnp.concatenate([jnp.arange(tt), jnp.tile(jnp.arange(t_s), nb)]),
                                      jnp.concatenate([jnp.arange(nt) * tt, jnp.array([PAST_LEN])]))
    base_blk = (tt + rows) // SUBLANES

    def tile_p(s):
        return jnp.minimum(s, n_tiles - 1)

    def tile_c(s):
        return jnp.maximum(s - 1, 0)

    off_spec = pl.BlockSpec((tt, LANES), lambda s: (0, 0))
    base_spec = pl.BlockSpec((SUBLANES, LANES), lambda s: (base_blk + tile_p(s) % nt, 0))
    x_spec = lambda tile: pl.BlockSpec((1, tt, D_MODEL), lambda s: (tile(s) // nt, tile(s) % nt, 0))
    seq_spec = lambda tile, d1, d2: pl.BlockSpec((1, d1, d2), lambda s: (tile(s) // nt, 0, 0))
    y_p, k_p, v_p, s_p = pl.pallas_call(
        functools.partial(_prompt_kernel, nt, n_tiles),
        grid=(n_tiles + 1,),
        in_specs=[smem, x_spec(tile_p), x_spec(tile_c), off_spec, off_spec, base_spec, base_spec] + weights_specs,
        out_specs=[x_spec(tile_c), seq_spec(tile_p, WINDOW, KV_W), seq_spec(tile_p, WINDOW, KV_W),
                   seq_spec(tile_c, S_ROWS, GLA_DV)],
        out_shape=[jax.ShapeDtypeStruct((bsz, t_p, D_MODEL), jnp.float32),
                   jax.ShapeDtypeStruct((bsz, WINDOW, KV_W), jnp.float32),
                   jax.ShapeDtypeStruct((bsz, WINDOW, KV_W), jnp.float32),
                   jax.ShapeDtypeStruct((bsz, S_ROWS, GLA_DV), jnp.float32)],
        scratch_shapes=_slot_shapes(tt, WINDOW + tt, tt // CHUNK) * 2
        + [pltpu.VMEM((S_ROWS, GLA_DV), jnp.float32), pltpu.VMEM((tt, D_MODEL), jnp.bfloat16),
           pltpu.VMEM((D_MODEL, D_MODEL), jnp.bfloat16)],
        compiler_params=pltpu.CompilerParams(dimension_semantics=("arbitrary",), vmem_limit_bytes=VMEM_LIMIT),
        name="prompt_layer",
    )(sinks, x_prompt, x_prompt, cos_all, sin_all, cos_all, sin_all, wpre, win, wup, bgk, gnw, wout, wpost)

    off_spec = pl.BlockSpec((rows, LANES), lambda i: (tt // rows, 0))
    base_spec = pl.BlockSpec((SUBLANES, LANES), lambda i: (base_blk + nt, 0))
    seq3 = lambda d1, d2: pl.BlockSpec((nb, d1, d2), lambda i: (i, 0, 0))
    tok = lambda d: pl.BlockSpec((rows, d), lambda i: (i, 0))
    y_s, k_s, v_s, s_s = pl.pallas_call(
        _sample_kernel,
        grid=(dec_b // nb,),
        in_specs=[smem, tok(D_MODEL), off_spec, off_spec, base_spec, base_spec, seq3(KV_W, WINDOW), seq3(KV_W, WINDOW),
                  seq3(S_ROWS, GLA_DV)] + weights_specs,
        out_specs=[tok(D_MODEL), tok(KV_W), tok(KV_W), seq3(S_ROWS, GLA_DV)],
        out_shape=[jax.ShapeDtypeStruct((dec_b * t_s, D_MODEL), jnp.float32),
                   jax.ShapeDtypeStruct((dec_b * t_s, KV_W), jnp.float32),
                   jax.ShapeDtypeStruct((dec_b * t_s, KV_W), jnp.float32),
                   jax.ShapeDtypeStruct((dec_b, S_ROWS, GLA_DV), jnp.float32)],
        scratch_shapes=_slot_shapes(rows, nb * (WINDOW + t_s), nb)
        + [pltpu.VMEM((rows, D_MODEL), jnp.bfloat16), pltpu.VMEM((D_MODEL, D_MODEL), jnp.bfloat16)],
        compiler_params=pltpu.CompilerParams(dimension_semantics=("arbitrary",), vmem_limit_bytes=VMEM_LIMIT),
        name="sample_layer",
    )(sinks, x_sample.reshape(dec_b * t_s, D_MODEL), cos_all, sin_all, cos_all, sin_all,
      cache_k[0].transpose(0, 2, 3, 1).reshape(dec_b, KV_W, WINDOW),
      cache_v[0].transpose(0, 2, 3, 1).reshape(dec_b, KV_W, WINDOW),
      state_gla[0].reshape(dec_b, S_ROWS, GLA_DV), wpre, win, wup, bgk, gnw, wout, wpost)

    kv5 = lambda a, b, t: a.reshape(1, b, t, N_KV_HEADS, HEAD_DIM)
    st5 = lambda a, b: a.reshape(1, b, GLA_HEADS, GLA_DK, GLA_DV)
    return (y_p, y_s.reshape(dec_b, t_s, D_MODEL),
            kv5(k_p, bsz, WINDOW), kv5(v_p, bsz, WINDOW), st5(s_p, bsz),
            kv5(k_s, dec_b, t_s), kv5(v_s, dec_b, t_s), st5(s_s, dec_b))
```

```python
import collections
import functools

import jax
import jax.numpy as jnp
from jax import lax
from jax.experimental import pallas as pl
from jax.experimental.pallas import tpu as pltpu

D_MODEL = 1024
CHUNK = 64
WINDOW = 128
HEAD_DIM = 64
N_HEADS = 8
N_KV_HEADS = 2
GQA_GROUP = N_HEADS // N_KV_HEADS
ROT_DIM = 16
ROPE_THETA = 500000.0
GLA_HEADS = 4
GLA_DV = 128
GLA_DK = 64
GLA_RANK = 16
GATE_TAU = 16.0
NORM_EPS = 1e-6
PAST_LEN = 4096

LANES = 128
SUBLANES = 8
ATTN_W = N_HEADS * HEAD_DIM
KV_W = N_KV_HEADS * HEAD_DIM
GQK_W = GLA_HEADS * GLA_DK
GV_W = GLA_HEADS * GLA_DV
S_ROWS = GLA_HEADS * GLA_DK
OFF_AQ = 0
OFF_AK = OFF_AQ + ATTN_W
OFF_AV = OFF_AK + KV_W
OFF_AG = OFF_AV + KV_W
OFF_GQ = OFF_AG + ATTN_W
OFF_GK = OFF_GQ + GQK_W
OFF_GV = OFF_GK + GQK_W
OFF_GG = OFF_GV + GV_W
OFF_LR = OFF_GG + GV_W
D_IN = OFF_LR + GLA_RANK
D_IN_PAD = OFF_LR + LANES
CAST_BLOCK = 1024
W_IN_ALLOC = -(-D_IN_PAD // CAST_BLOCK) * CAST_BLOCK

PROMPT_TILE = 512
SAMPLE_SEQS = 8
V7X_VMEM_BYTES = 64 * 1024 * 1024
VMEM_LIMIT = V7X_VMEM_BYTES * 3 // 4
STEP_ORDER = "CPCPPCPPPCPP"
CHUNK_GROUP = 8

NEG = -1e30
LOG2E = 1.4426950408889634

Slot = collections.namedtuple("Slot", "qs kv ag qin kog kdec dec gv gg")


def _slot_shapes(rows, kv_rows, n_chunks):
    bf, f32 = jnp.bfloat16, jnp.float32
    return ([pltpu.VMEM((rows, 2 * ATTN_W), bf)] + [pltpu.VMEM((kv_rows, LANES), bf) for _ in range(6)]
            + [pltpu.VMEM((rows, ATTN_W), f32), pltpu.VMEM((rows, 2 * GQK_W), bf), pltpu.VMEM((rows, GQK_W), bf),
               pltpu.VMEM((rows, GQK_W), f32), pltpu.VMEM((n_chunks * SUBLANES, GQK_W), f32),
               pltpu.VMEM((rows, GV_W), bf), pltpu.VMEM((rows, GV_W), f32)])


N_SLOT_REFS = 14


def _make_slot(refs):
    return Slot(refs[0], tuple(refs[1:7]), *refs[7:14])


def _dot(a, b):
    return jnp.dot(a, b, preferred_element_type=jnp.float32)


def _dot_nt(a, b):
    return lax.dot_general(a, b, (((1,), (1,)), ((), ())), preferred_element_type=jnp.float32)


def _bf(x):
    return x.astype(jnp.bfloat16)


def _lane_masks():
    lane = lax.broadcasted_iota(jnp.int32, (1, LANES), 1)
    return lane < HEAD_DIM, lane >= HEAD_DIM


def _silu(x):
    return x * (1.0 / (1.0 + jnp.exp(-x)))


def _log_sigmoid_scaled(x, scale):
    soft = jnp.log2(1.0 + jnp.exp2(jnp.abs(x) * -LOG2E))
    return jnp.minimum(x, 0.0) * scale - soft * (scale / LOG2E)


def _rms_scale(x):
    return lax.rsqrt(jnp.mean(x * x, axis=-1, keepdims=True) + NORM_EPS)


def _chunk_cumsum(g, chunk):
    row = lax.broadcasted_iota(jnp.int32, g.shape, 0) & (chunk - 1)
    s = 1
    while s < chunk:
        g = g + jnp.where(row >= s, pltpu.roll(g, s, 0), 0.0)
        s *= 2
    return g


def _rotary(xcol, cos_t, sin_dn, sin_up):
    return (xcol * cos_t + pltpu.roll(xcol, ROT_DIM // 2, 1) * sin_dn
            + pltpu.roll(xcol, LANES - ROT_DIM // 2, 1) * sin_up)


def _rotary_lanes(cos_off, sin_off, cos_base, sin_base):
    d = lax.broadcasted_iota(jnp.int32, (1, LANES), 1) & (HEAD_DIM - 1)
    half = ROT_DIM // 2
    rot, up, dn = d < ROT_DIM, d < half, (d >= half) & (d < ROT_DIM)
    cb, sb = cos_base[0:1, :], sin_base[0:1, :]
    co, so = cos_off[...], sin_off[...]
    cos_t = jnp.where(rot, cb, 0.0) * co - jnp.where(rot, sb, 0.0) * so + jnp.where(rot, 0.0, 1.0)
    sin_dn = jnp.where(dn, sb, 0.0) * co + jnp.where(dn, cb, 0.0) * so
    sin_up = jnp.where(up, -sb, 0.0) * co + jnp.where(up, -cb, 0.0) * so
    return cos_t, sin_dn, sin_up


def _kv_variants(k, v):
    m_lo, m_hi = _lane_masks()
    k_sw = pltpu.roll(k, HEAD_DIM, 1)
    v_sw = pltpu.roll(v, HEAD_DIM, 1)
    return (_bf(k), _bf(k_sw),
            _bf(jnp.where(m_lo, v, 0.0)), _bf(jnp.where(m_hi, v_sw, 0.0)),
            _bf(jnp.where(m_lo, v_sw, 0.0)), _bf(jnp.where(m_hi, v, 0.0)))


def _store_halves(ref, j, width, col):
    m_lo, m_hi = _lane_masks()
    ref[:, j * LANES:(j + 1) * LANES] = _bf(jnp.where(m_lo, col, 0.0))
    ref[:, width + j * LANES:width + (j + 1) * LANES] = _bf(jnp.where(m_hi, col, 0.0))


def _project_stages(x, tabs, wpre_ref, win_ref, wup_ref, bgk_ref, chunk, slot, store_kv):
    rows = x.shape[0]
    h = _bf(x * _rms_scale(x) * wpre_ref[...])
    cos_t, sin_dn, sin_up = _rotary_lanes(*tabs)

    def seg(off, width):
        return _dot(h, win_ref[:, off:off + width])

    yield
    q = seg(OFF_AQ, ATTN_W)
    for j in range(ATTN_W // LANES):
        ln = slice(j * LANES, (j + 1) * LANES)
        _store_halves(slot.qs, j, ATTN_W, _rotary(q[:, ln], cos_t, sin_dn, sin_up) * (LOG2E * HEAD_DIM ** -0.5))
    yield
    kv = seg(OFF_AK, 2 * KV_W)
    store_kv(_rotary(kv[:, :KV_W], cos_t, sin_dn, sin_up), kv[:, KV_W:])
    yield
    slot.ag[...] = _silu(seg(OFF_AG, ATTN_W))
    yield
    lr = _bf(seg(OFF_LR, LANES))
    log_a = _log_sigmoid_scaled(_dot(lr, wup_ref[...]) + bgk_ref[...], 1.0 / GATE_TAU)
    b = _chunk_cumsum(log_a, chunk)
    lasts = [b[r + chunk - 1:r + chunk] for r in range(0, rows, chunk)]
    for i, bl in enumerate(lasts):
        slot.dec[i * SUBLANES:(i + 1) * SUBLANES, :] = jnp.broadcast_to(jnp.exp(bl), (SUBLANES, GQK_W))
    b_last = jnp.concatenate([jnp.broadcast_to(bl, (chunk, GQK_W)) for bl in lasts], axis=0)
    gqk = seg(OFF_GQ, 2 * GQK_W)
    q_in = gqk[:, :GQK_W] * (GLA_DK ** -0.5) * jnp.exp(b)
    for j in range(GQK_W // LANES):
        _store_halves(slot.qin, j, GQK_W, q_in[:, j * LANES:(j + 1) * LANES])
    yield
    gk = gqk[:, GQK_W:]
    slot.kog[...] = _bf(gk * jnp.exp(-b))
    slot.kdec[...] = gk * jnp.exp(b_last - b)
    yield
    slot.gv[...] = _bf(seg(OFF_GV, GV_W))
    yield
    slot.gg[...] = _silu(seg(OFF_GG, GV_W))


def _attend_stages(slot, rows, kv_win, sinks_ref, n_invalid, emit):
    c = rows.stop - rows.start
    lo = [slot.qs[rows, j * LANES:(j + 1) * LANES] for j in range(ATTN_W // LANES)]
    hi = [slot.qs[rows, ATTN_W + j * LANES:ATTN_W + (j + 1) * LANES] for j in range(ATTN_W // LANES)]
    k_a, k_b = kv_win(0), kv_win(1)
    n_hi = k_a.shape[0] - LANES
    zrows = jnp.zeros((LANES - n_hi, LANES), jnp.bfloat16)
    ext = lambda a: jnp.concatenate([a, zrows], axis=0)
    s_a = _dot_nt(jnp.concatenate([lo[0], lo[1], hi[2], hi[3]], axis=0), ext(k_a))
    s_b = _dot_nt(jnp.concatenate([hi[0], hi[1], lo[2], lo[3]], axis=0), ext(k_b))
    yield
    lane = lax.broadcasted_iota(jnp.int32, (1, LANES), 1)
    where = {0: (s_a, 0), 2: (s_a, 1), 5: (s_a, 2), 7: (s_a, 3),
             1: (s_b, 0), 3: (s_b, 1), 4: (s_b, 2), 6: (s_b, 3)}
    probs, rdenom = {}, {}
    for h in range(N_HEADS):
        src, blk = where[h]
        s = src[blk * c:(blk + 1) * c]
        pad_row = jnp.where(lane == n_hi, sinks_ref[0, h] * LOG2E, NEG)
        s_lo = s[:, :LANES]
        s_hi = jnp.where(lane >= n_hi, pad_row, s[:, LANES:])
        if n_invalid is not None:
            s_lo = jnp.where(lane < n_invalid, NEG, s_lo)
        m = jnp.max(jnp.maximum(s_lo, s_hi), axis=-1, keepdims=True)
        p_lo = jnp.exp2(s_lo - m)
        p_hi = jnp.exp2(s_hi - m)
        rdenom[h] = 1.0 / jnp.sum(p_lo + p_hi, axis=-1, keepdims=True)
        probs[h] = _bf(jnp.concatenate([p_lo, p_hi], axis=1))
    outs = {}
    for g in range(N_KV_HEADS):
        for par in range(2):
            ha, hb = g * GQA_GROUP + par, g * GQA_GROUP + par + 2
            o2 = _dot(jnp.concatenate([probs[ha], probs[hb]], axis=0), ext(kv_win(2 + 2 * g + par)))
            outs[ha] = o2[:c] * rdenom[ha]
            outs[hb] = o2[c:] * rdenom[hb]
    yield
    emit(jnp.concatenate([outs[2 * j] + outs[2 * j + 1] for j in range(4)], axis=1))


def _gla_stages(slot, r0, c, state, gnw_ref, mix_ref):
    rows = slice(r0, r0 + c)
    ci0 = (r0 // c) * SUBLANES
    lhs, a2, uw, dcol, vw = [], [], [], [], []
    for p in range(GLA_HEADS // 2):
        ln = slice(p * LANES, (p + 1) * LANES)
        vwide = slot.gv[rows, p * 2 * GLA_DV:(p + 1) * 2 * GLA_DV]
        lhs.append(jnp.concatenate([slot.qin[rows, ln], slot.qin[rows, GQK_W + p * LANES:GQK_W + (p + 1) * LANES]],
                                   axis=0))
        a2.append(_dot_nt(lhs[p], slot.kog[rows, ln]))
        uw.append(_dot(_bf(slot.kdec[rows, ln].T), vwide))
        dcol.append(jnp.broadcast_to(slot.dec[ci0:ci0 + 1, ln], (LANES, LANES)).T)
        vw.append(vwide)
    yield
    outs = []
    ri = lax.broadcasted_iota(jnp.int32, a2[0].shape, 0) & (c - 1)
    ci = lax.broadcasted_iota(jnp.int32, a2[0].shape, 1)
    for p in range(GLA_HEADS // 2):
        am = _bf(jnp.where(ri >= ci, a2[p], 0.0))
        sb = _bf(state[p])
        for i in range(2):
            hr = slice(i * c, (i + 1) * c)
            outs.append(_dot(jnp.concatenate([lhs[p][hr], am[hr]], axis=1),
                             jnp.concatenate([sb, vw[p][:, i * GLA_DV:(i + 1) * GLA_DV]], axis=0)))
        u_pair = jnp.concatenate([uw[p][:GLA_DK, :GLA_DV], uw[p][GLA_DK:, GLA_DV:]], axis=0)
        state[p] = dcol[p] * state[p] + u_pair
    yield
    gnw = gnw_ref[...]
    for hh, o in enumerate(outs):
        y = o * _rms_scale(o) * gnw * slot.gg[rows, hh * GLA_DV:(hh + 1) * GLA_DV]
        mix_ref[rows, ATTN_W + hh * GLA_DV:ATTN_W + (hh + 1) * GLA_DV] = _bf(y)


def _lockstep(gens):
    live = list(gens)
    while live:
        for g in list(live):
            try:
                next(g)
            except StopIteration:
                live.remove(g)
        yield


def _chunk_stages(slot, chunk_args, c, state_of, sinks_ref, gnw_ref, mix_ref):
    gens = []
    for i, (r0, kv_win, n_invalid) in enumerate(chunk_args):
        rows = slice(r0, r0 + c)

        def emit(attn, rows=rows):
            mix_ref[rows, 0:ATTN_W] = _bf(attn * slot.ag[rows, :])

        gens.append(_attend_stages(slot, rows, kv_win, sinks_ref, n_invalid, emit))
        gens.append(_gla_stages(slot, r0, c, state_of(i), gnw_ref, mix_ref))
    for g in range(0, len(gens), 2 * CHUNK_GROUP):
        yield from _lockstep(gens[g:g + 2 * CHUNK_GROUP])


def _finish(x, mix_ref, wout_ref, wpost_ref):
    mix = _dot(mix_ref[...], wout_ref[...])
    return x + mix * _rms_scale(mix) * wpost_ref[...]


def _run_order(order, p_gen, c_gen):
    for ch in order:
        next(p_gen if ch == "P" else c_gen, None)
    for g in (p_gen, c_gen):
        for _ in g:
            pass


def _pipeline_step(n_tiles, slots, body, first_step_init):
    s = pl.program_id(0)

    @pl.when(s == 0)
    def _():
        first_step_init()
        body(slots[0], slots[1], True, False)

    @pl.when(s == n_tiles)
    def _():
        body(slots[n_tiles % 2], slots[1 - n_tiles % 2], False, True)

    middle = (s > 0) & (s < n_tiles)

    @pl.when(middle & (s % 2 == 0))
    def _():
        body(slots[0], slots[1], True, True)

    @pl.when(middle & (s % 2 == 1))
    def _():
        body(slots[1], slots[0], True, True)


def _prompt_kernel(nt, n_tiles, sinks_ref, xp_ref, xc_ref, coff_ref, soff_ref, cbase_ref, sbase_ref, wpre_ref, win_ref,
                   wup_ref, bgk_ref, gnw_ref, wout_ref, wpost_ref,
                   y_ref, kout_ref, vout_ref, sout_ref, *scratch):
    slots = (_make_slot(scratch[:N_SLOT_REFS]), _make_slot(scratch[N_SLOT_REFS:2 * N_SLOT_REFS]))
    s_ref, mix_ref, woutb_ref = scratch[2 * N_SLOT_REFS:]
    tt = xp_ref.shape[1]
    t_c = jnp.maximum(pl.program_id(0) - 1, 0) % nt

    def body(slot_p, slot_c, project, chunks):
        def store_kv(k, v):
            kout_ref[0] = k[tt - WINDOW:]
            vout_ref[0] = v[tt - WINDOW:]
            for r, rc, val in zip(slot_p.kv, slot_c.kv, _kv_variants(k, v)):
                r[WINDOW:WINDOW + tt, :] = val
                r[0:WINDOW, :] = rc[tt:tt + WINDOW, :]

        p_gen = (_project_stages(xp_ref[0], (coff_ref, soff_ref, cbase_ref, sbase_ref), wpre_ref, win_ref, wup_ref,
                                 bgk_ref, CHUNK, slot_p, store_kv) if project else iter(()))

        def c_stages():
            keep = t_c != 0
            state = [jnp.where(keep, s_ref[p * LANES:(p + 1) * LANES, :], 0.0) for p in range(GLA_HEADS // 2)]
            win = WINDOW + CHUNK
            chunk_args = []
            for r0 in range(0, tt, CHUNK):
                n_invalid = WINDOW - (t_c * tt + r0) if r0 < WINDOW else None
                chunk_args.append((r0, lambda i, r0=r0: slot_c.kv[i][r0:r0 + win, :], n_invalid))
            yield from _chunk_stages(slot_c, chunk_args, CHUNK, lambda i: state, sinks_ref, gnw_ref, mix_ref)
            y_ref[0] = _finish(xc_ref[0], mix_ref, woutb_ref, wpost_ref)
            for p in range(GLA_HEADS // 2):
                s_ref[p * LANES:(p + 1) * LANES, :] = state[p]
                sout_ref[0, p * LANES:(p + 1) * LANES, :] = state[p]

        _run_order(STEP_ORDER if project and chunks else "", p_gen, c_stages() if chunks else iter(()))

    def first_step_init():
        for r in slots[1].kv + (s_ref,):
            r[...] = jnp.zeros_like(r)
        woutb_ref[...] = _bf(wout_ref[...])

    _pipeline_step(n_tiles, slots, body, first_step_init)


def _sample_kernel(sinks_ref, x_ref, coff_ref, soff_ref, cbase_ref, sbase_ref, ck_ref, cv_ref, s0_ref, wpre_ref,
                   win_ref, wup_ref, bgk_ref, gnw_ref, wout_ref, wpost_ref,
                   y_ref, kout_ref, vout_ref, sout_ref, *scratch):
    slot = _make_slot(scratch[:N_SLOT_REFS])
    mix_ref, woutb_ref = scratch[N_SLOT_REFS:]
    nb = ck_ref.shape[0]

    @pl.when(pl.program_id(0) == 0)
    def _():
        woutb_ref[...] = _bf(wout_ref[...])

    c = x_ref.shape[0] // nb
    win = WINDOW + c

    def store_kv(k, v):
        kout_ref[...] = k
        vout_ref[...] = v
        for i in range(nb):
            rows = slice(i * c, (i + 1) * c)
            kv = _kv_variants(jnp.concatenate([ck_ref[i].T, k[rows]], axis=0),
                              jnp.concatenate([cv_ref[i].T, v[rows]], axis=0))
            for r, val in zip(slot.kv, kv):
                r[i * win:(i + 1) * win, :] = val

    x = x_ref[...]
    for _ in _project_stages(x, (coff_ref, soff_ref, cbase_ref, sbase_ref), wpre_ref, win_ref, wup_ref, bgk_ref, c, slot,
                             store_kv):
        pass
    states = [[s0_ref[i, p * LANES:(p + 1) * LANES, :] for p in range(GLA_HEADS // 2)] for i in range(nb)]
    chunk_args = [(i * c, lambda j, i=i: slot.kv[j][i * win:(i + 1) * win, :], None) for i in range(nb)]
    for _ in _chunk_stages(slot, chunk_args, c, lambda i: states[i], sinks_ref, gnw_ref, mix_ref):
        pass
    for i in range(nb):
        for p in range(GLA_HEADS // 2):
            sout_ref[i, p * LANES:(p + 1) * LANES, :] = states[i][p]
    y_ref[...] = _finish(x, mix_ref, woutb_ref, wpost_ref)


def _rotary_angles(offsets, bases):
    half = ROT_DIM // 2
    d = jnp.arange(LANES) % HEAD_DIM
    inv = ROPE_THETA ** (-(d % half).astype(jnp.float32) * (2.0 / ROT_DIM))
    pos = jnp.concatenate([offsets, jnp.repeat(bases, SUBLANES)]).astype(jnp.float32)
    ang = pos[:, None] * inv[None, :]
    return jnp.cos(ang), jnp.sin(ang)


def _cast_w_in_kernel(wt_ref, out_ref):
    row = lax.broadcasted_iota(jnp.int32, wt_ref.shape, 0) + pl.program_id(0) * CAST_BLOCK
    out_ref[...] = _bf(jnp.where(row < D_IN, wt_ref[...], 0.0).T)


def _cast_w_in(w_in_t):
    return pl.pallas_call(
        _cast_w_in_kernel,
        grid=(W_IN_ALLOC // CAST_BLOCK,),
        in_specs=[pl.BlockSpec((CAST_BLOCK, D_MODEL), lambda i: (i, 0))],
        out_specs=pl.BlockSpec((D_MODEL, CAST_BLOCK), lambda i: (0, i)),
        out_shape=jax.ShapeDtypeStruct((D_MODEL, W_IN_ALLOC), jnp.bfloat16),
        compiler_params=pltpu.CompilerParams(dimension_semantics=("arbitrary",)),
        name="cast_w_in",
    )(w_in_t)


def _const_spec(shape):
    zeros = (0,) * len(shape)
    return pl.BlockSpec(shape, lambda i: zeros, pipeline_mode=pl.Buffered(1))


def kernel(x_prompt, x_sample, cache_k, cache_v, state_gla, norm_pre_w, w_in, attn_sinks, w_gk_up, b_gk,
           gla_norm_w, w_out, norm_post_w):
    bsz, t_p, _ = x_prompt.shape
    dec_b, t_s, _ = x_sample.shape
    assert w_in.shape[0] == 1, "single layer"
    assert t_p % PROMPT_TILE == 0 and PROMPT_TILE % CHUNK == 0 and PROMPT_TILE >= WINDOW
    assert dec_b % SAMPLE_SEQS == 0 and t_s & (t_s - 1) == 0

    win = _cast_w_in(w_in[0].T)
    wout = w_out[0]
    wup = _bf(jnp.pad(w_gk_up[0], ((0, LANES - GLA_RANK), (0, 0))))
    wpre = norm_pre_w[0][None, :]
    wpost = norm_post_w[0][None, :]
    bgk = b_gk[0][None, :]
    gnw = gla_norm_w[0][None, :]
    sinks = attn_sinks[0][None, :]
    smem = pl.BlockSpec(memory_space=pltpu.SMEM)

    weights_specs = [
        _const_spec((1, D_MODEL)), _const_spec((D_MODEL, D_IN_PAD)), _const_spec((LANES, GQK_W)),
        _const_spec((1, GQK_W)), _const_spec((1, GLA_DV)), _const_spec((D_MODEL, D_MODEL)),
        _const_spec((1, D_MODEL))]

    tt = PROMPT_TILE
    nt = t_p // tt
    n_tiles = bsz * nt
    nb = SAMPLE_SEQS
    rows = nb * t_s
    assert tt % rows == 0 and (tt + rows) % SUBLANES == 0
    cos_all, sin_all = _rotary_angles(jnp.concatenate([jnp.arange(tt), jnp.tile(jnp.arange(t_s), nb)]),
                                      jnp.concatenate([jnp.arange(nt) * tt, jnp.array([PAST_LEN])]))
    base_blk = (tt + rows) // SUBLANES

    off_spec = pl.BlockSpec((rows, LANES), lambda i: (tt // rows, 0))
    base_spec = pl.BlockSpec((SUBLANES, LANES), lambda i: (base_blk + nt, 0))
    seq3 = lambda d1, d2: pl.BlockSpec((nb, d1, d2), lambda i: (i, 0, 0))
    tok = lambda d: pl.BlockSpec((rows, d), lambda i: (i, 0))
    y_s, k_s, v_s, s_s = pl.pallas_call(
        _sample_kernel,
        grid=(dec_b // nb,),
        in_specs=[smem, tok(D_MODEL), off_spec, off_spec, base_spec, base_spec, seq3(KV_W, WINDOW), seq3(KV_W, WINDOW),
                  seq3(S_ROWS, GLA_DV)] + weights_specs,
        out_specs=[tok(D_MODEL), tok(KV_W), tok(KV_W), seq3(S_ROWS, GLA_DV)],
        out_shape=[jax.ShapeDtypeStruct((dec_b * t_s, D_MODEL), jnp.float32),
                   jax.ShapeDtypeStruct((dec_b * t_s, KV_W), jnp.float32),
                   jax.ShapeDtypeStruct((dec_b * t_s, KV_W), jnp.float32),
                   jax.ShapeDtypeStruct((dec_b, S_ROWS, GLA_DV), jnp.float32)],
        scratch_shapes=_slot_shapes(rows, nb * (WINDOW + t_s), nb)
        + [pltpu.VMEM((rows, D_MODEL), jnp.bfloat16), pltpu.VMEM((D_MODEL, D_MODEL), jnp.bfloat16)],
        compiler_params=pltpu.CompilerParams(dimension_semantics=("arbitrary",), vmem_limit_bytes=VMEM_LIMIT),
        name="sample_layer",
    )(sinks, x_sample.reshape(dec_b * t_s, D_MODEL), cos_all, sin_all, cos_all, sin_all,
      cache_k[0].transpose(0, 2, 3, 1).reshape(dec_b, KV_W, WINDOW),
      cache_v[0].transpose(0, 2, 3, 1).reshape(dec_b, KV_W, WINDOW),
      state_gla[0].reshape(dec_b, S_ROWS, GLA_DV), wpre, win, wup, bgk, gnw, wout, wpost)

    def tile_p(s):
        return jnp.minimum(s, n_tiles - 1)

    def tile_c(s):
        return jnp.maximum(s - 1, 0)

    off_spec = pl.BlockSpec((tt, LANES), lambda s: (0, 0))
    base_spec = pl.BlockSpec((SUBLANES, LANES), lambda s: (base_blk + tile_p(s) % nt, 0))
    x_spec = lambda tile: pl.BlockSpec((1, tt, D_MODEL), lambda s: (tile(s) // nt, tile(s) % nt, 0))
    seq_spec = lambda tile, d1, d2: pl.BlockSpec((1, d1, d2), lambda s: (tile(s) // nt, 0, 0))
    y_p, k_p, v_p, s_p = pl.pallas_call(
        functools.partial(_prompt_kernel, nt, n_tiles),
        grid=(n_tiles + 1,),
        in_specs=[smem, x_spec(tile_p), x_spec(tile_c), off_spec, off_spec, base_spec, base_spec] + weights_specs,
        out_specs=[x_spec(tile_c), seq_spec(tile_p, WINDOW, KV_W), seq_spec(tile_p, WINDOW, KV_W),
                   seq_spec(tile_c, S_ROWS, GLA_DV)],
        out_shape=[jax.ShapeDtypeStruct((bsz, t_p, D_MODEL), jnp.float32),
                   jax.ShapeDtypeStruct((bsz, WINDOW, KV_W), jnp.float32),
                   jax.ShapeDtypeStruct((bsz, WINDOW, KV_W), jnp.float32),
                   jax.ShapeDtypeStruct((bsz, S_ROWS, GLA_DV), jnp.float32)],
        scratch_shapes=_slot_shapes(tt, WINDOW + tt, tt // CHUNK) * 2
        + [pltpu.VMEM((S_ROWS, GLA_DV), jnp.float32), pltpu.VMEM((tt, D_MODEL), jnp.bfloat16),
           pltpu.VMEM((D_MODEL, D_MODEL), jnp.bfloat16)],
        compiler_params=pltpu.CompilerParams(dimension_semantics=("arbitrary",), vmem_limit_bytes=VMEM_LIMIT),
        name="prompt_layer",
    )(sinks, x_prompt, x_prompt, cos_all, sin_all, cos_all, sin_all, wpre, win, wup, bgk, gnw, wout, wpost)

    kv5 = lambda a, b, t: a.reshape(1, b, t, N_KV_HEADS, HEAD_DIM)
    st5 = lambda a, b: a.reshape(1, b, GLA_HEADS, GLA_DK, GLA_DV)
    return (y_p, y_s.reshape(dec_b, t_s, D_MODEL),
            kv5(k_p, bsz, WINDOW), kv5(v_p, bsz, WINDOW), st5(s_p, bsz),
            kv5(k_s, dec_b, t_s), kv5(v_s, dec_b, t_s), st5(s_s, dec_b))
```

```python
import collections
import functools

import jax
import jax.numpy as jnp
from jax import lax
from jax.experimental import pallas as pl
from jax.experimental.pallas import tpu as pltpu

D_MODEL = 1024
CHUNK = 64
WINDOW = 128
HEAD_DIM = 64
N_HEADS = 8
N_KV_HEADS = 2
GQA_GROUP = N_HEADS // N_KV_HEADS
ROT_DIM = 16
ROPE_THETA = 500000.0
GLA_HEADS = 4
GLA_DV = 128
GLA_DK = 64
GLA_RANK = 16
GATE_TAU = 16.0
NORM_EPS = 1e-6
PAST_LEN = 4096

LANES = 128
SUBLANES = 8
ATTN_W = N_HEADS * HEAD_DIM
KV_W = N_KV_HEADS * HEAD_DIM
GQK_W = GLA_HEADS * GLA_DK
GV_W = GLA_HEADS * GLA_DV
S_ROWS = GLA_HEADS * GLA_DK
OFF_AQ = 0
OFF_AK = OFF_AQ + ATTN_W
OFF_AV = OFF_AK + KV_W
OFF_AG = OFF_AV + KV_W
OFF_GQ = OFF_AG + ATTN_W
OFF_GK = OFF_GQ + GQK_W
OFF_GV = OFF_GK + GQK_W
OFF_GG = OFF_GV + GV_W
OFF_LR = OFF_GG + GV_W
D_IN = OFF_LR + GLA_RANK
D_IN_PAD = OFF_LR + LANES
CAST_BLOCK = 1024
W_IN_ALLOC = -(-D_IN_PAD // CAST_BLOCK) * CAST_BLOCK

PROMPT_TILE = 512
SAMPLE_SEQS = 8
V7X_VMEM_BYTES = 64 * 1024 * 1024
VMEM_LIMIT = V7X_VMEM_BYTES * 3 // 4
STEP_ORDER = "CPCPPCPPPCPP"
CHUNK_GROUP = 8

NEG = -1e30
LOG2E = 1.4426950408889634

Slot = collections.namedtuple("Slot", "qs kv ag qin kog kdec dec gv gg")


def _slot_shapes(rows, kv_rows, n_chunks):
    bf, f32 = jnp.bfloat16, jnp.float32
    return ([pltpu.VMEM((rows, 2 * ATTN_W), bf)] + [pltpu.VMEM((kv_rows, LANES), bf) for _ in range(6)]
            + [pltpu.VMEM((rows, ATTN_W), f32), pltpu.VMEM((rows, 2 * GQK_W), bf), pltpu.VMEM((rows, GQK_W), bf),
               pltpu.VMEM((rows, GQK_W), f32), pltpu.VMEM((n_chunks * SUBLANES, GQK_W), f32),
               pltpu.VMEM((rows, GV_W), bf), pltpu.VMEM((rows, GV_W), f32)])


N_SLOT_REFS = 14


def _make_slot(refs):
    return Slot(refs[0], tuple(refs[1:7]), *refs[7:14])


def _dot(a, b):
    return jnp.dot(a, b, preferred_element_type=jnp.float32)


def _dot_nt(a, b):
    return lax.dot_general(a, b, (((1,), (1,)), ((), ())), preferred_element_type=jnp.float32)


def _bf(x):
    return x.astype(jnp.bfloat16)


def _lane_masks():
    lane = lax.broadcasted_iota(jnp.int32, (1, LANES), 1)
    return lane < HEAD_DIM, lane >= HEAD_DIM


def _silu(x):
    return x * (1.0 / (1.0 + jnp.exp(-x)))


def _log_sigmoid_scaled(x, scale):
    soft = jnp.log2(1.0 + jnp.exp2(jnp.abs(x) * -LOG2E))
    return jnp.minimum(x, 0.0) * scale - soft * (scale / LOG2E)


def _rms_scale(x):
    return lax.rsqrt(jnp.mean(x * x, axis=-1, keepdims=True) + NORM_EPS)


def _chunk_cumsum(g, chunk):
    row = lax.broadcasted_iota(jnp.int32, g.shape, 0) & (chunk - 1)
    s = 1
    while s < chunk:
        g = g + jnp.where(row >= s, pltpu.roll(g, s, 0), 0.0)
        s *= 2
    return g


def _rotary(xcol, cos_t, sin_dn, sin_up):
    return (xcol * cos_t + pltpu.roll(xcol, ROT_DIM // 2, 1) * sin_dn
            + pltpu.roll(xcol, LANES - ROT_DIM // 2, 1) * sin_up)


def _rotary_lanes(cos_off, sin_off, cos_base, sin_base):
    d = lax.broadcasted_iota(jnp.int32, (1, LANES), 1) & (HEAD_DIM - 1)
    half = ROT_DIM // 2
    rot, up, dn = d < ROT_DIM, d < half, (d >= half) & (d < ROT_DIM)
    cb, sb = cos_base[0:1, :], sin_base[0:1, :]
    co, so = cos_off[...], sin_off[...]
    cos_t = jnp.where(rot, cb, 0.0) * co - jnp.where(rot, sb, 0.0) * so + jnp.where(rot, 0.0, 1.0)
    sin_dn = jnp.where(dn, sb, 0.0) * co + jnp.where(dn, cb, 0.0) * so
    sin_up = jnp.where(up, -sb, 0.0) * co + jnp.where(up, -cb, 0.0) * so
    return cos_t, sin_dn, sin_up


def _kv_variants(k, v):
    m_lo, m_hi = _lane_masks()
    k_sw = pltpu.roll(k, HEAD_DIM, 1)
    v_sw = pltpu.roll(v, HEAD_DIM, 1)
    return (_bf(k), _bf(k_sw),
            _bf(jnp.where(m_lo, v, 0.0)), _bf(jnp.where(m_hi, v_sw, 0.0)),
            _bf(jnp.where(m_lo, v_sw, 0.0)), _bf(jnp.where(m_hi, v, 0.0)))


def _store_halves(ref, j, width, col):
    m_lo, m_hi = _lane_masks()
    ref[:, j * LANES:(j + 1) * LANES] = _bf(jnp.where(m_lo, col, 0.0))
    ref[:, width + j * LANES:width + (j + 1) * LANES] = _bf(jnp.where(m_hi, col, 0.0))


def _project_stages(x, tabs, wpre_ref, win_ref, wup_ref, bgk_ref, chunk, slot, store_kv):
    rows = x.shape[0]
    h = _bf(x * _rms_scale(x) * wpre_ref[...])
    cos_t, sin_dn, sin_up = _rotary_lanes(*tabs)

    def seg(off, width):
        return _dot(h, win_ref[:, off:off + width])

    yield
    q = seg(OFF_AQ, ATTN_W)
    for j in range(ATTN_W // LANES):
        ln = slice(j * LANES, (j + 1) * LANES)
        _store_halves(slot.qs, j, ATTN_W, _rotary(q[:, ln], cos_t, sin_dn, sin_up) * (LOG2E * HEAD_DIM ** -0.5))
    yield
    kv = seg(OFF_AK, 2 * KV_W)
    store_kv(_rotary(kv[:, :KV_W], cos_t, sin_dn, sin_up), kv[:, KV_W:])
    yield
    slot.ag[...] = _silu(seg(OFF_AG, ATTN_W))
    yield
    lr = _bf(seg(OFF_LR, LANES))
    log_a = _log_sigmoid_scaled(_dot(lr, wup_ref[...]) + bgk_ref[...], 1.0 / GATE_TAU)
    b = _chunk_cumsum(log_a, chunk)
    lasts = [b[r + chunk - 1:r + chunk] for r in range(0, rows, chunk)]
    for i, bl in enumerate(lasts):
        slot.dec[i * SUBLANES:(i + 1) * SUBLANES, :] = jnp.broadcast_to(jnp.exp(bl), (SUBLANES, GQK_W))
    b_last = jnp.concatenate([jnp.broadcast_to(bl, (chunk, GQK_W)) for bl in lasts], axis=0)
    gqk = seg(OFF_GQ, 2 * GQK_W)
    q_in = gqk[:, :GQK_W] * (GLA_DK ** -0.5) * jnp.exp(b)
    for j in range(GQK_W // LANES):
        _store_halves(slot.qin, j, GQK_W, q_in[:, j * LANES:(j + 1) * LANES])
    yield
    gk = gqk[:, GQK_W:]
    slot.kog[...] = _bf(gk * jnp.exp(-b))
    slot.kdec[...] = gk * jnp.exp(b_last - b)
    yield
    slot.gv[...] = _bf(seg(OFF_GV, GV_W))
    yield
    slot.gg[...] = _silu(seg(OFF_GG, GV_W))


def _attend_stages(slot, rows, kv_win, sinks_ref, n_invalid, emit):
    c = rows.stop - rows.start
    lo = [slot.qs[rows, j * LANES:(j + 1) * LANES] for j in range(ATTN_W // LANES)]
    hi = [slot.qs[rows, ATTN_W + j * LANES:ATTN_W + (j + 1) * LANES] for j in range(ATTN_W // LANES)]
    k_a, k_b = kv_win(0), kv_win(1)
    n_hi = k_a.shape[0] - LANES
    zrows = jnp.zeros((LANES - n_hi, LANES), jnp.bfloat16)
    ext = lambda a: jnp.concatenate([a, zrows], axis=0)
    s_a = _dot_nt(jnp.concatenate([lo[0], lo[1], hi[2], hi[3]], axis=0), ext(k_a))
    s_b = _dot_nt(jnp.concatenate([hi[0], hi[1], lo[2], lo[3]], axis=0), ext(k_b))
    yield
    lane = lax.broadcasted_iota(jnp.int32, (1, LANES), 1)
    where = {0: (s_a, 0), 2: (s_a, 1), 5: (s_a, 2), 7: (s_a, 3),
             1: (s_b, 0), 3: (s_b, 1), 4: (s_b, 2), 6: (s_b, 3)}
    probs, rdenom = {}, {}
    for h in range(N_HEADS):
        src, blk = where[h]
        s = src[blk * c:(blk + 1) * c]
        pad_row = jnp.where(lane == n_hi, sinks_ref[0, h] * LOG2E, NEG)
        s_lo = s[:, :LANES]
        s_hi = jnp.where(lane >= n_hi, pad_row, s[:, LANES:])
        if n_invalid is not None:
            s_lo = jnp.where(lane < n_invalid, NEG, s_lo)
        m = jnp.max(jnp.maximum(s_lo, s_hi), axis=-1, keepdims=True)
        p_lo = jnp.exp2(s_lo - m)
        p_hi = jnp.exp2(s_hi - m)
        rdenom[h] = 1.0 / jnp.sum(p_lo + p_hi, axis=-1, keepdims=True)
        probs[h] = _bf(jnp.concatenate([p_lo, p_hi], axis=1))
    outs = {}
    for g in range(N_KV_HEADS):
        for par in range(2):
            ha, hb = g * GQA_GROUP + par, g * GQA_GROUP + par + 2
            o2 = _dot(jnp.concatenate([probs[ha], probs[hb]], axis=0), ext(kv_win(2 + 2 * g + par)))
            outs[ha] = o2[:c] * rdenom[ha]
            outs[hb] = o2[c:] * rdenom[hb]
    yield
    emit(jnp.concatenate([outs[2 * j] + outs[2 * j + 1] for j in range(4)], axis=1))


def _gla_stages(slot, r0, c, state, gnw_ref, mix_ref):
    rows = slice(r0, r0 + c)
    ci0 = (r0 // c) * SUBLANES
    lhs, a2, uw, dcol, vw = [], [], [], [], []
    for p in range(GLA_HEADS // 2):
        ln = slice(p * LANES, (p + 1) * LANES)
        vwide = slot.gv[rows, p * 2 * GLA_DV:(p + 1) * 2 * GLA_DV]
        lhs.append(jnp.concatenate([slot.qin[rows, ln], slot.qin[rows, GQK_W + p * LANES:GQK_W + (p + 1) * LANES]],
                                   axis=0))
        a2.append(_dot_nt(lhs[p], slot.kog[rows, ln]))
        uw.append(_dot(_bf(slot.kdec[rows, ln].T), vwide))
        dcol.append(jnp.broadcast_to(slot.dec[ci0:ci0 + 1, ln], (LANES, LANES)).T)
        vw.append(vwide)
    yield
    outs = []
    ri = lax.broadcasted_iota(jnp.int32, a2[0].shape, 0) & (c - 1)
    ci = lax.broadcasted_iota(jnp.int32, a2[0].shape, 1)
    for p in range(GLA_HEADS // 2):
        am = _bf(jnp.where(ri >= ci, a2[p], 0.0))
        sb = _bf(state[p])
        for i in range(2):
            hr = slice(i * c, (i + 1) * c)
            outs.append(_dot(jnp.concatenate([lhs[p][hr], am[hr]], axis=1),
                             jnp.concatenate([sb, vw[p][:, i * GLA_DV:(i + 1) * GLA_DV]], axis=0)))
        u_pair = jnp.concatenate([uw[p][:GLA_DK, :GLA_DV], uw[p][GLA_DK:, GLA_DV:]], axis=0)
        state[p] = dcol[p] * state[p] + u_pair
    yield
    gnw = gnw_ref[...]
    for hh, o in enumerate(outs):
        y = o * _rms_scale(o) * gnw * slot.gg[rows, hh * GLA_DV:(hh + 1) * GLA_DV]
        mix_ref[rows, ATTN_W + hh * GLA_DV:ATTN_W + (hh + 1) * GLA_DV] = _bf(y)


def _lockstep(gens):
    live = list(gens)
    while live:
        for g in list(live):
            try:
                next(g)
            except StopIteration:
                live.remove(g)
        yield


def _chunk_stages(slot, chunk_args, c, state_of, sinks_ref, gnw_ref, mix_ref):
    gens = []
    for i, (r0, kv_win, n_invalid) in enumerate(chunk_args):
        rows = slice(r0, r0 + c)

        def emit(attn, rows=rows):
            mix_ref[rows, 0:ATTN_W] = _bf(attn * slot.ag[rows, :])

        gens.append(_attend_stages(slot, rows, kv_win, sinks_ref, n_invalid, emit))
        gens.append(_gla_stages(slot, r0, c, state_of(i), gnw_ref, mix_ref))
    for g in range(0, len(gens), 2 * CHUNK_GROUP):
        yield from _lockstep(gens[g:g + 2 * CHUNK_GROUP])


def _finish(x, mix_ref, wout_ref, wpost_ref):
    mix = _dot(mix_ref[...], wout_ref[...])
    return x + mix * _rms_scale(mix) * wpost_ref[...]


def _run_order(order, p_gen, c_gen):
    for ch in order:
        next(p_gen if ch == "P" else c_gen, None)
    for g in (p_gen, c_gen):
        for _ in g:
            pass


def _pipeline_step(n_tiles, slots, body, first_step_init):
    s = pl.program_id(0)

    @pl.when(s == 0)
    def _():
        first_step_init()
        body(slots[0], slots[1], True, False)

    @pl.when(s == n_tiles)
    def _():
        body(slots[n_tiles % 2], slots[1 - n_tiles % 2], False, True)

    middle = (s > 0) & (s < n_tiles)

    @pl.when(middle & (s % 2 == 0))
    def _():
        body(slots[0], slots[1], True, True)

    @pl.when(middle & (s % 2 == 1))
    def _():
        body(slots[1], slots[0], True, True)


def _prompt_kernel(nt, n_tiles, sinks_ref, xp_ref, xc_ref, coff_ref, soff_ref, cbase_ref, sbase_ref, wpre_ref, win_ref,
                   wup_ref, bgk_ref, gnw_ref, wout_ref, wpost_ref,
                   y_ref, kout_ref, vout_ref, sout_ref, *scratch):
    slots = (_make_slot(scratch[:N_SLOT_REFS]), _make_slot(scratch[N_SLOT_REFS:2 * N_SLOT_REFS]))
    s_ref, mix_ref, woutb_ref = scratch[2 * N_SLOT_REFS:]
    tt = xp_ref.shape[1]
    t_c = jnp.maximum(pl.program_id(0) - 1, 0) % nt

    def body(slot_p, slot_c, project, chunks):
        def store_kv(k, v):
            kout_ref[0] = k[tt - WINDOW:].T
            vout_ref[0] = v[tt - WINDOW:].T
            for r, rc, val in zip(slot_p.kv, slot_c.kv, _kv_variants(k, v)):
                r[WINDOW:WINDOW + tt, :] = val
                r[0:WINDOW, :] = rc[tt:tt + WINDOW, :]

        p_gen = (_project_stages(xp_ref[0], (coff_ref, soff_ref, cbase_ref, sbase_ref), wpre_ref, win_ref, wup_ref,
                                 bgk_ref, CHUNK, slot_p, store_kv) if project else iter(()))

        def c_stages():
            keep = t_c != 0
            state = [jnp.where(keep, s_ref[p * LANES:(p + 1) * LANES, :], 0.0) for p in range(GLA_HEADS // 2)]
            win = WINDOW + CHUNK
            chunk_args = []
            for r0 in range(0, tt, CHUNK):
                n_invalid = WINDOW - (t_c * tt + r0) if r0 < WINDOW else None
                chunk_args.append((r0, lambda i, r0=r0: slot_c.kv[i][r0:r0 + win, :], n_invalid))
            yield from _chunk_stages(slot_c, chunk_args, CHUNK, lambda i: state, sinks_ref, gnw_ref, mix_ref)
            y_ref[0] = _finish(xc_ref[0], mix_ref, woutb_ref, wpost_ref)
            for p in range(GLA_HEADS // 2):
                s_ref[p * LANES:(p + 1) * LANES, :] = state[p]
                sout_ref[0, p * LANES:(p + 1) * LANES, :] = state[p]

        _run_order(STEP_ORDER if project and chunks else "", p_gen, c_stages() if chunks else iter(()))

    def first_step_init():
        for r in slots[1].kv + (s_ref,):
            r[...] = jnp.zeros_like(r)
        woutb_ref[...] = _bf(wout_ref[...])

    _pipeline_step(n_tiles, slots, body, first_step_init)


def _sample_kernel(sinks_ref, x_ref, coff_ref, soff_ref, cbase_ref, sbase_ref, ck_ref, cv_ref, s0_ref, wpre_ref,
                   win_ref, wup_ref, bgk_ref, gnw_ref, wout_ref, wpost_ref,
                   y_ref, kout_ref, vout_ref, sout_ref, *scratch):
    slot = _make_slot(scratch[:N_SLOT_REFS])
    mix_ref, woutb_ref = scratch[N_SLOT_REFS:]
    nb = ck_ref.shape[0]

    @pl.when(pl.program_id(0) == 0)
    def _():
        woutb_ref[...] = _bf(wout_ref[...])

    c = x_ref.shape[0] // nb
    win = WINDOW + c

    def store_kv(k, v):
        kout_ref[...] = k
        vout_ref[...] = v
        for i in range(nb):
            rows = slice(i * c, (i + 1) * c)
            kv = _kv_variants(jnp.concatenate([ck_ref[i].T, k[rows]], axis=0),
                              jnp.concatenate([cv_ref[i].T, v[rows]], axis=0))
            for r, val in zip(slot.kv, kv):
                r[i * win:(i + 1) * win, :] = val

    x = x_ref[...]
    for _ in _project_stages(x, (coff_ref, soff_ref, cbase_ref, sbase_ref), wpre_ref, win_ref, wup_ref, bgk_ref, c, slot,
                             store_kv):
        pass
    states = [[s0_ref[i, p * LANES:(p + 1) * LANES, :] for p in range(GLA_HEADS // 2)] for i in range(nb)]
    chunk_args = [(i * c, lambda j, i=i: slot.kv[j][i * win:(i + 1) * win, :], None) for i in range(nb)]
    for _ in _chunk_stages(slot, chunk_args, c, lambda i: states[i], sinks_ref, gnw_ref, mix_ref):
        pass
    for i in range(nb):
        for p in range(GLA_HEADS // 2):
            sout_ref[i, p * LANES:(p + 1) * LANES, :] = states[i][p]
    y_ref[...] = _finish(x, mix_ref, woutb_ref, wpost_ref)


def _rotary_angles(offsets, bases):
    half = ROT_DIM // 2
    d = jnp.arange(LANES) % HEAD_DIM
    inv = ROPE_THETA ** (-(d % half).astype(jnp.float32) * (2.0 / ROT_DIM))
    pos = jnp.concatenate([offsets, jnp.repeat(bases, SUBLANES)]).astype(jnp.float32)
    ang = pos[:, None] * inv[None, :]
    return jnp.cos(ang), jnp.sin(ang)


def _cast_w_in_kernel(wt_ref, out_ref):
    row = lax.broadcasted_iota(jnp.int32, wt_ref.shape, 0) + pl.program_id(0) * CAST_BLOCK
    out_ref[...] = _bf(jnp.where(row < D_IN, wt_ref[...], 0.0).T)


def _cast_w_in(w_in_t):
    return pl.pallas_call(
        _cast_w_in_kernel,
        grid=(W_IN_ALLOC // CAST_BLOCK,),
        in_specs=[pl.BlockSpec((CAST_BLOCK, D_MODEL), lambda i: (i, 0))],
        out_specs=pl.BlockSpec((D_MODEL, CAST_BLOCK), lambda i: (0, i)),
        out_shape=jax.ShapeDtypeStruct((D_MODEL, W_IN_ALLOC), jnp.bfloat16),
        compiler_params=pltpu.CompilerParams(dimension_semantics=("arbitrary",)),
        name="cast_w_in",
    )(w_in_t)


def _const_spec(shape):
    zeros = (0,) * len(shape)
    return pl.BlockSpec(shape, lambda i: zeros, pipeline_mode=pl.Buffered(1))


def kernel(x_prompt, x_sample, cache_k, cache_v, state_gla, norm_pre_w, w_in, attn_sinks, w_gk_up, b_gk,
           gla_norm_w, w_out, norm_post_w):
    bsz, t_p, _ = x_prompt.shape
    dec_b, t_s, _ = x_sample.shape
    assert w_in.shape[0] == 1, "single layer"
    assert t_p % PROMPT_TILE == 0 and PROMPT_TILE % CHUNK == 0 and PROMPT_TILE >= WINDOW
    assert dec_b % SAMPLE_SEQS == 0 and t_s & (t_s - 1) == 0

    win = _cast_w_in(w_in[0].T)
    wout = w_out[0]
    wup = _bf(jnp.pad(w_gk_up[0], ((0, LANES - GLA_RANK), (0, 0))))
    wpre = norm_pre_w[0][None, :]
    wpost = norm_post_w[0][None, :]
    bgk = b_gk[0][None, :]
    gnw = gla_norm_w[0][None, :]
    sinks = attn_sinks[0][None, :]
    smem = pl.BlockSpec(memory_space=pltpu.SMEM)

    weights_specs = [
        _const_spec((1, D_MODEL)), _const_spec((D_MODEL, D_IN_PAD)), _const_spec((LANES, GQK_W)),
        _const_spec((1, GQK_W)), _const_spec((1, GLA_DV)), _const_spec((D_MODEL, D_MODEL)),
        _const_spec((1, D_MODEL))]

    tt = PROMPT_TILE
    nt = t_p // tt
    n_tiles = bsz * nt
    nb = SAMPLE_SEQS
    rows = nb * t_s
    assert tt % rows == 0 and (tt + rows) % SUBLANES == 0
    cos_all, sin_all = _rotary_angles(jnp.concatenate([jnp.arange(tt), jnp.tile(jnp.arange(t_s), nb)]),
                                      jnp.concatenate([jnp.arange(nt) * tt, jnp.array([PAST_LEN])]))
    base_blk = (tt + rows) // SUBLANES

    off_spec = pl.BlockSpec((rows, LANES), lambda i: (tt // rows, 0))
    base_spec = pl.BlockSpec((SUBLANES, LANES), lambda i: (base_blk + nt, 0))
    seq3 = lambda d1, d2: pl.BlockSpec((nb, d1, d2), lambda i: (i, 0, 0))
    tok = lambda d: pl.BlockSpec((rows, d), lambda i: (i, 0))
    y_s, k_s, v_s, s_s = pl.pallas_call(
        _sample_kernel,
        grid=(dec_b // nb,),
        in_specs=[smem, tok(D_MODEL), off_spec, off_spec, base_spec, base_spec, seq3(KV_W, WINDOW), seq3(KV_W, WINDOW),
                  seq3(S_ROWS, GLA_DV)] + weights_specs,
        out_specs=[tok(D_MODEL), tok(KV_W), tok(KV_W), seq3(S_ROWS, GLA_DV)],
        out_shape=[jax.ShapeDtypeStruct((dec_b * t_s, D_MODEL), jnp.float32),
                   jax.ShapeDtypeStruct((dec_b * t_s, KV_W), jnp.float32),
                   jax.ShapeDtypeStruct((dec_b * t_s, KV_W), jnp.float32),
                   jax.ShapeDtypeStruct((dec_b, S_ROWS, GLA_DV), jnp.float32)],
        scratch_shapes=_slot_shapes(rows, nb * (WINDOW + t_s), nb)
        + [pltpu.VMEM((rows, D_MODEL), jnp.bfloat16), pltpu.VMEM((D_MODEL, D_MODEL), jnp.bfloat16)],
        compiler_params=pltpu.CompilerParams(dimension_semantics=("arbitrary",), vmem_limit_bytes=VMEM_LIMIT),
        name="sample_layer",
    )(sinks, x_sample.reshape(dec_b * t_s, D_MODEL), cos_all, sin_all, cos_all, sin_all,
      cache_k[0].transpose(0, 2, 3, 1).reshape(dec_b, KV_W, WINDOW),
      cache_v[0].transpose(0, 2, 3, 1).reshape(dec_b, KV_W, WINDOW),
      state_gla[0].reshape(dec_b, S_ROWS, GLA_DV), wpre, win, wup, bgk, gnw, wout, wpost)

    def tile_p(s):
        return jnp.minimum(s, n_tiles - 1)

    def tile_c(s):
        return jnp.maximum(s - 1, 0)

    off_spec = pl.BlockSpec((tt, LANES), lambda s: (0, 0))
    base_spec = pl.BlockSpec((SUBLANES, LANES), lambda s: (base_blk + tile_p(s) % nt, 0))
    x_spec = lambda tile: pl.BlockSpec((1, tt, D_MODEL), lambda s: (tile(s) // nt, tile(s) % nt, 0))
    seq_spec = lambda tile, d1, d2: pl.BlockSpec((1, d1, d2), lambda s: (tile(s) // nt, 0, 0))
    y_p, k_p, v_p, s_p = pl.pallas_call(
        functools.partial(_prompt_kernel, nt, n_tiles),
        grid=(n_tiles + 1,),
        in_specs=[smem, x_spec(tile_p), x_spec(tile_c), off_spec, off_spec, base_spec, base_spec] + weights_specs,
        out_specs=[x_spec(tile_c), seq_spec(tile_p, KV_W, WINDOW), seq_spec(tile_p, KV_W, WINDOW),
                   seq_spec(tile_c, S_ROWS, GLA_DV)],
        out_shape=[jax.ShapeDtypeStruct((bsz, t_p, D_MODEL), jnp.float32),
                   jax.ShapeDtypeStruct((bsz, KV_W, WINDOW), jnp.float32),
                   jax.ShapeDtypeStruct((bsz, KV_W, WINDOW), jnp.float32),
                   jax.ShapeDtypeStruct((bsz, S_ROWS, GLA_DV), jnp.float32)],
        scratch_shapes=_slot_shapes(tt, WINDOW + tt, tt // CHUNK) * 2
        + [pltpu.VMEM((S_ROWS, GLA_DV), jnp.float32), pltpu.VMEM((tt, D_MODEL), jnp.bfloat16),
           pltpu.VMEM((D_MODEL, D_MODEL), jnp.bfloat16)],
        compiler_params=pltpu.CompilerParams(dimension_semantics=("arbitrary",), vmem_limit_bytes=VMEM_LIMIT),
        name="prompt_layer",
    )(sinks, x_prompt, x_prompt, cos_all, sin_all, cos_all, sin_all, wpre, win, wup, bgk, gnw, wout, wpost)

    kv5 = lambda a, b, t: a.reshape(1, b, t, N_KV_HEADS, HEAD_DIM)
    kvt5 = lambda a: a.reshape(1, bsz, N_KV_HEADS, HEAD_DIM, WINDOW).transpose(0, 1, 4, 2, 3)
    st5 = lambda a, b: a.reshape(1, b, GLA_HEADS, GLA_DK, GLA_DV)
    return (y_p, y_s.reshape(dec_b, t_s, D_MODEL),
            kvt5(k_p), kvt5(v_p), st5(s_p, bsz),
            kv5(k_s, dec_b, t_s), kv5(v_s, dec_b, t_s), st5(s_s, dec_b))
```

```python
import collections
import functools

import jax
import jax.numpy as jnp
from jax import lax
from jax.experimental import pallas as pl
from jax.experimental.pallas import tpu as pltpu

D_MODEL = 1024
CHUNK = 64
WINDOW = 128
HEAD_DIM = 64
N_HEADS = 8
N_KV_HEADS = 2
GQA_GROUP = N_HEADS // N_KV_HEADS
ROT_DIM = 16
ROPE_THETA = 500000.0
GLA_HEADS = 4
GLA_DV = 128
GLA_DK = 64
GLA_RANK = 16
GATE_TAU = 16.0
NORM_EPS = 1e-6
PAST_LEN = 4096

LANES = 128
SUBLANES = 8
ATTN_W = N_HEADS * HEAD_DIM
KV_W = N_KV_HEADS * HEAD_DIM
GQK_W = GLA_HEADS * GLA_DK
GV_W = GLA_HEADS * GLA_DV
S_ROWS = GLA_HEADS * GLA_DK
OFF_AQ = 0
OFF_AK = OFF_AQ + ATTN_W
OFF_AV = OFF_AK + KV_W
OFF_AG = OFF_AV + KV_W
OFF_GQ = OFF_AG + ATTN_W
OFF_GK = OFF_GQ + GQK_W
OFF_GV = OFF_GK + GQK_W
OFF_GG = OFF_GV + GV_W
OFF_LR = OFF_GG + GV_W
D_IN = OFF_LR + GLA_RANK
D_IN_PAD = OFF_LR + LANES
CAST_BLOCK = 1024
W_IN_ALLOC = -(-D_IN_PAD // CAST_BLOCK) * CAST_BLOCK

PROMPT_TILE = 512
SAMPLE_SEQS = 8
V7X_VMEM_BYTES = 64 * 1024 * 1024
VMEM_LIMIT = V7X_VMEM_BYTES * 3 // 4
STEP_ORDER = "CPCPPCPPPCPP"
CHUNK_GROUP = 8

NEG = -1e30
LOG2E = 1.4426950408889634

Slot = collections.namedtuple("Slot", "qs kv ag qin kog kdec dec gv gg")


def _slot_shapes(rows, kv_rows, n_chunks):
    bf, f32 = jnp.bfloat16, jnp.float32
    return ([pltpu.VMEM((rows, 2 * ATTN_W), bf)] + [pltpu.VMEM((kv_rows, LANES), bf) for _ in range(6)]
            + [pltpu.VMEM((rows, ATTN_W), f32), pltpu.VMEM((rows, 2 * GQK_W), bf), pltpu.VMEM((rows, GQK_W), bf),
               pltpu.VMEM((rows, GQK_W), f32), pltpu.VMEM((n_chunks * SUBLANES, GQK_W), f32),
               pltpu.VMEM((rows, GV_W), bf), pltpu.VMEM((rows, GV_W), f32)])


N_SLOT_REFS = 14


def _make_slot(refs):
    return Slot(refs[0], tuple(refs[1:7]), *refs[7:14])


def _dot(a, b):
    return jnp.dot(a, b, preferred_element_type=jnp.float32)


def _dot_nt(a, b):
    return lax.dot_general(a, b, (((1,), (1,)), ((), ())), preferred_element_type=jnp.float32)


def _bf(x):
    return x.astype(jnp.bfloat16)


def _lane_masks():
    lane = lax.broadcasted_iota(jnp.int32, (1, LANES), 1)
    return lane < HEAD_DIM, lane >= HEAD_DIM


def _silu(x):
    return x * (1.0 / (1.0 + jnp.exp(-x)))


def _log_sigmoid_scaled(x, scale):
    soft = jnp.log2(1.0 + jnp.exp2(jnp.abs(x) * -LOG2E))
    return jnp.minimum(x, 0.0) * scale - soft * (scale / LOG2E)


def _rms_scale(x):
    return lax.rsqrt(jnp.mean(x * x, axis=-1, keepdims=True) + NORM_EPS)


def _chunk_cumsum(g, chunk):
    row = lax.broadcasted_iota(jnp.int32, g.shape, 0) & (chunk - 1)
    s = 1
    while s < chunk:
        g = g + jnp.where(row >= s, pltpu.roll(g, s, 0), 0.0)
        s *= 2
    return g


def _rotary(xcol, cos_t, sin_dn, sin_up):
    return (xcol * cos_t + pltpu.roll(xcol, ROT_DIM // 2, 1) * sin_dn
            + pltpu.roll(xcol, LANES - ROT_DIM // 2, 1) * sin_up)


def _rotary_lanes(cos_off, sin_off, cos_base, sin_base):
    d = lax.broadcasted_iota(jnp.int32, (1, LANES), 1) & (HEAD_DIM - 1)
    half = ROT_DIM // 2
    rot, up, dn = d < ROT_DIM, d < half, (d >= half) & (d < ROT_DIM)
    cb, sb = cos_base[0:1, :], sin_base[0:1, :]
    co, so = cos_off[...], sin_off[...]
    cos_t = jnp.where(rot, cb, 0.0) * co - jnp.where(rot, sb, 0.0) * so + jnp.where(rot, 0.0, 1.0)
    sin_dn = jnp.where(dn, sb, 0.0) * co + jnp.where(dn, cb, 0.0) * so
    sin_up = jnp.where(up, -sb, 0.0) * co + jnp.where(up, -cb, 0.0) * so
    return cos_t, sin_dn, sin_up


def _kv_variants(k, v):
    m_lo, m_hi = _lane_masks()
    k_sw = pltpu.roll(k, HEAD_DIM, 1)
    v_sw = pltpu.roll(v, HEAD_DIM, 1)
    return (_bf(k), _bf(k_sw),
            _bf(jnp.where(m_lo, v, 0.0)), _bf(jnp.where(m_hi, v_sw, 0.0)),
            _bf(jnp.where(m_lo, v_sw, 0.0)), _bf(jnp.where(m_hi, v, 0.0)))


def _store_halves(ref, j, width, col):
    m_lo, m_hi = _lane_masks()
    ref[:, j * LANES:(j + 1) * LANES] = _bf(jnp.where(m_lo, col, 0.0))
    ref[:, width + j * LANES:width + (j + 1) * LANES] = _bf(jnp.where(m_hi, col, 0.0))


def _project_stages(x, tabs, wpre_ref, win_ref, wup_ref, bgk_ref, chunk, slot, store_kv):
    rows = x.shape[0]
    h = _bf(x * _rms_scale(x) * wpre_ref[...])
    cos_t, sin_dn, sin_up = _rotary_lanes(*tabs)

    def seg(off, width):
        return _dot(h, win_ref[:, off:off + width])

    yield
    q = seg(OFF_AQ, ATTN_W)
    for j in range(ATTN_W // LANES):
        ln = slice(j * LANES, (j + 1) * LANES)
        _store_halves(slot.qs, j, ATTN_W, _rotary(q[:, ln], cos_t, sin_dn, sin_up) * (LOG2E * HEAD_DIM ** -0.5))
    yield
    kv = seg(OFF_AK, 2 * KV_W)
    store_kv(_rotary(kv[:, :KV_W], cos_t, sin_dn, sin_up), kv[:, KV_W:])
    yield
    slot.ag[...] = _silu(seg(OFF_AG, ATTN_W))
    yield
    lr = _bf(seg(OFF_LR, LANES))
    log_a = _log_sigmoid_scaled(_dot(lr, wup_ref[...]) + bgk_ref[...], 1.0 / GATE_TAU)
    b = _chunk_cumsum(log_a, chunk)
    lasts = [b[r + chunk - 1:r + chunk] for r in range(0, rows, chunk)]
    for i, bl in enumerate(lasts):
        slot.dec[i * SUBLANES:(i + 1) * SUBLANES, :] = jnp.broadcast_to(jnp.exp(bl), (SUBLANES, GQK_W))
    b_last = jnp.concatenate([jnp.broadcast_to(bl, (chunk, GQK_W)) for bl in lasts], axis=0)
    gqk = seg(OFF_GQ, 2 * GQK_W)
    q_in = gqk[:, :GQK_W] * (GLA_DK ** -0.5) * jnp.exp(b)
    for j in range(GQK_W // LANES):
        _store_halves(slot.qin, j, GQK_W, q_in[:, j * LANES:(j + 1) * LANES])
    yield
    gk = gqk[:, GQK_W:]
    slot.kog[...] = _bf(gk * jnp.exp(-b))
    slot.kdec[...] = gk * jnp.exp(b_last - b)
    yield
    slot.gv[...] = _bf(seg(OFF_GV, GV_W))
    yield
    slot.gg[...] = _silu(seg(OFF_GG, GV_W))


def _attend_stages(slot, rows, kv_win, sinks_ref, n_invalid, emit):
    c = rows.stop - rows.start
    lo = [slot.qs[rows, j * LANES:(j + 1) * LANES] for j in range(ATTN_W // LANES)]
    hi = [slot.qs[rows, ATTN_W + j * LANES:ATTN_W + (j + 1) * LANES] for j in range(ATTN_W // LANES)]
    k_a, k_b = kv_win(0), kv_win(1)
    n_hi = k_a.shape[0] - LANES
    zrows = jnp.zeros((LANES - n_hi, LANES), jnp.bfloat16)
    ext = lambda a: jnp.concatenate([a, zrows], axis=0)
    s_a = _dot_nt(jnp.concatenate([lo[0], lo[1], hi[2], hi[3]], axis=0), ext(k_a))
    s_b = _dot_nt(jnp.concatenate([hi[0], hi[1], lo[2], lo[3]], axis=0), ext(k_b))
    yield
    lane = lax.broadcasted_iota(jnp.int32, (1, LANES), 1)
    where = {0: (s_a, 0), 2: (s_a, 1), 5: (s_a, 2), 7: (s_a, 3),
             1: (s_b, 0), 3: (s_b, 1), 4: (s_b, 2), 6: (s_b, 3)}
    probs, rdenom = {}, {}
    for h in range(N_HEADS):
        src, blk = where[h]
        s = src[blk * c:(blk + 1) * c]
        pad_row = jnp.where(lane == n_hi, sinks_ref[0, h] * LOG2E, NEG)
        s_lo = s[:, :LANES]
        s_hi = jnp.where(lane >= n_hi, pad_row, s[:, LANES:])
        if n_invalid is not None:
            s_lo = jnp.where(lane < n_invalid, NEG, s_lo)
        m = jnp.max(jnp.maximum(s_lo, s_hi), axis=-1, keepdims=True)
        p_lo = jnp.exp2(s_lo - m)
        p_hi = jnp.exp2(s_hi - m)
        rdenom[h] = 1.0 / jnp.sum(p_lo + p_hi, axis=-1, keepdims=True)
        probs[h] = _bf(jnp.concatenate([p_lo, p_hi], axis=1))
    outs = {}
    for g in range(N_KV_HEADS):
        for par in range(2):
            ha, hb = g * GQA_GROUP + par, g * GQA_GROUP + par + 2
            o2 = _dot(jnp.concatenate([probs[ha], probs[hb]], axis=0), ext(kv_win(2 + 2 * g + par)))
            outs[ha] = o2[:c] * rdenom[ha]
            outs[hb] = o2[c:] * rdenom[hb]
    yield
    emit(jnp.concatenate([outs[2 * j] + outs[2 * j + 1] for j in range(4)], axis=1))


def _gla_stages(slot, r0, c, state, gnw_ref, mix_ref):
    rows = slice(r0, r0 + c)
    ci0 = (r0 // c) * SUBLANES
    lhs, a2, uw, dcol, vw = [], [], [], [], []
    for p in range(GLA_HEADS // 2):
        ln = slice(p * LANES, (p + 1) * LANES)
        vwide = slot.gv[rows, p * 2 * GLA_DV:(p + 1) * 2 * GLA_DV]
        lhs.append(jnp.concatenate([slot.qin[rows, ln], slot.qin[rows, GQK_W + p * LANES:GQK_W + (p + 1) * LANES]],
                                   axis=0))
        a2.append(_dot_nt(lhs[p], slot.kog[rows, ln]))
        uw.append(_dot(_bf(slot.kdec[rows, ln].T), vwide))
        dcol.append(jnp.broadcast_to(slot.dec[ci0:ci0 + 1, ln], (LANES, LANES)).T)
        vw.append(vwide)
    yield
    outs = []
    ri = lax.broadcasted_iota(jnp.int32, a2[0].shape, 0) & (c - 1)
    ci = lax.broadcasted_iota(jnp.int32, a2[0].shape, 1)
    for p in range(GLA_HEADS // 2):
        am = _bf(jnp.where(ri >= ci, a2[p], 0.0))
        sb = _bf(state[p])
        for i in range(2):
            hr = slice(i * c, (i + 1) * c)
            outs.append(_dot(jnp.concatenate([lhs[p][hr], am[hr]], axis=1),
                             jnp.concatenate([sb, vw[p][:, i * GLA_DV:(i + 1) * GLA_DV]], axis=0)))
        u_pair = jnp.concatenate([uw[p][:GLA_DK, :GLA_DV], uw[p][GLA_DK:, GLA_DV:]], axis=0)
        state[p] = dcol[p] * state[p] + u_pair
    yield
    gnw = gnw_ref[...]
    for hh, o in enumerate(outs):
        y = o * _rms_scale(o) * gnw * slot.gg[rows, hh * GLA_DV:(hh + 1) * GLA_DV]
        mix_ref[rows, ATTN_W + hh * GLA_DV:ATTN_W + (hh + 1) * GLA_DV] = _bf(y)


def _lockstep(gens):
    live = list(gens)
    while live:
        for g in list(live):
            try:
                next(g)
            except StopIteration:
                live.remove(g)
        yield


def _chunk_stages(slot, chunk_args, c, state_of, sinks_ref, gnw_ref, mix_ref):
    gens = []
    for i, (r0, kv_win, n_invalid) in enumerate(chunk_args):
        rows = slice(r0, r0 + c)

        def emit(attn, rows=rows):
            mix_ref[rows, 0:ATTN_W] = _bf(attn * slot.ag[rows, :])

        gens.append(_attend_stages(slot, rows, kv_win, sinks_ref, n_invalid, emit))
        gens.append(_gla_stages(slot, r0, c, state_of(i), gnw_ref, mix_ref))
    for g in range(0, len(gens), 2 * CHUNK_GROUP):
        yield from _lockstep(gens[g:g + 2 * CHUNK_GROUP])


def _finish(x, mix_ref, wout_ref, wpost_ref):
    mix = _dot(mix_ref[...], wout_ref[...])
    return x + mix * _rms_scale(mix) * wpost_ref[...]


def _run_order(order, p_gen, c_gen):
    for ch in order:
        next(p_gen if ch == "P" else c_gen, None)
    for g in (p_gen, c_gen):
        for _ in g:
            pass


def _pipeline_step(n_tiles, slots, body, first_step_init):
    s = pl.program_id(0)

    @pl.when(s == 0)
    def _():
        first_step_init()
        body(slots[0], slots[1], True, False)

    @pl.when(s == n_tiles)
    def _():
        body(slots[n_tiles % 2], slots[1 - n_tiles % 2], False, True)

    middle = (s > 0) & (s < n_tiles)

    @pl.when(middle & (s % 2 == 0))
    def _():
        body(slots[0], slots[1], True, True)

    @pl.when(middle & (s % 2 == 1))
    def _():
        body(slots[1], slots[0], True, True)


def _prompt_kernel(nt, n_tiles, sinks_ref, xp_ref, xc_ref, coff_ref, soff_ref, cbase_ref, sbase_ref, wpre_ref, win_ref,
                   wup_ref, bgk_ref, gnw_ref, wout_ref, wpost_ref,
                   y_ref, kout_ref, vout_ref, sout_ref, *scratch):
    slots = (_make_slot(scratch[:N_SLOT_REFS]), _make_slot(scratch[N_SLOT_REFS:2 * N_SLOT_REFS]))
    s_ref, mix_ref, woutb_ref = scratch[2 * N_SLOT_REFS:]
    tt = xp_ref.shape[1]
    t_c = jnp.maximum(pl.program_id(0) - 1, 0) % nt

    def body(slot_p, slot_c, project, chunks):
        def store_kv(k, v):
            kout_ref[0] = k[tt - WINDOW:].T
            vout_ref[0] = v[tt - WINDOW:].T
            for r, rc, val in zip(slot_p.kv, slot_c.kv, _kv_variants(k, v)):
                r[WINDOW:WINDOW + tt, :] = val
                r[0:WINDOW, :] = rc[tt:tt + WINDOW, :]

        p_gen = (_project_stages(xp_ref[0], (coff_ref, soff_ref, cbase_ref, sbase_ref), wpre_ref, win_ref, wup_ref,
                                 bgk_ref, CHUNK, slot_p, store_kv) if project else iter(()))

        def c_stages():
            keep = t_c != 0
            state = [jnp.where(keep, s_ref[p * LANES:(p + 1) * LANES, :], 0.0) for p in range(GLA_HEADS // 2)]
            win = WINDOW + CHUNK
            chunk_args = []
            for r0 in range(0, tt, CHUNK):
                n_invalid = WINDOW - (t_c * tt + r0) if r0 < WINDOW else None
                chunk_args.append((r0, lambda i, r0=r0: slot_c.kv[i][r0:r0 + win, :], n_invalid))
            yield from _chunk_stages(slot_c, chunk_args, CHUNK, lambda i: state, sinks_ref, gnw_ref, mix_ref)
            y_ref[0] = _finish(xc_ref[0], mix_ref, woutb_ref, wpost_ref)
            for p in range(GLA_HEADS // 2):
                s_ref[p * LANES:(p + 1) * LANES, :] = state[p]
                sout_ref[0, p * LANES:(p + 1) * LANES, :] = state[p]

        _run_order(STEP_ORDER if project and chunks else "", p_gen, c_stages() if chunks else iter(()))

    def first_step_init():
        for r in slots[1].kv + (s_ref,):
            r[...] = jnp.zeros_like(r)
        woutb_ref[...] = _bf(wout_ref[...])

    _pipeline_step(n_tiles, slots, body, first_step_init)


def _sample_kernel(sinks_ref, x_ref, coff_ref, soff_ref, cbase_ref, sbase_ref, ck_ref, cv_ref, s0_ref, wpre_ref,
                   win_ref, wup_ref, bgk_ref, gnw_ref, wout_ref, wpost_ref,
                   y_ref, kout_ref, vout_ref, sout_ref, *scratch):
    slot = _make_slot(scratch[:N_SLOT_REFS])
    mix_ref, woutb_ref = scratch[N_SLOT_REFS:]
    nb = ck_ref.shape[0]

    @pl.when(pl.program_id(0) == 0)
    def _():
        woutb_ref[...] = _bf(wout_ref[...])

    c = x_ref.shape[0] // nb
    win = WINDOW + c

    def store_kv(k, v):
        for o_ref, val in ((kout_ref, k), (vout_ref, v)):
            o_ref[:, 0, :] = val[:, :HEAD_DIM]
            o_ref[:, 1, :] = val[:, HEAD_DIM:]
        for i in range(nb):
            rows = slice(i * c, (i + 1) * c)
            kv = _kv_variants(jnp.concatenate([ck_ref[i].T, k[rows]], axis=0),
                              jnp.concatenate([cv_ref[i].T, v[rows]], axis=0))
            for r, val in zip(slot.kv, kv):
                r[i * win:(i + 1) * win, :] = val

    x = x_ref[...]
    for _ in _project_stages(x, (coff_ref, soff_ref, cbase_ref, sbase_ref), wpre_ref, win_ref, wup_ref, bgk_ref, c, slot,
                             store_kv):
        pass
    states = [[s0_ref[i, p * LANES:(p + 1) * LANES, :] for p in range(GLA_HEADS // 2)] for i in range(nb)]
    chunk_args = [(i * c, lambda j, i=i: slot.kv[j][i * win:(i + 1) * win, :], None) for i in range(nb)]
    for _ in _chunk_stages(slot, chunk_args, c, lambda i: states[i], sinks_ref, gnw_ref, mix_ref):
        pass
    for i in range(nb):
        for p in range(GLA_HEADS // 2):
            sout_ref[i, p * LANES:(p + 1) * LANES, :] = states[i][p]
    y_ref[...] = _finish(x, mix_ref, woutb_ref, wpost_ref)


def _rotary_angles(offsets, bases):
    half = ROT_DIM // 2
    d = jnp.arange(LANES) % HEAD_DIM
    inv = ROPE_THETA ** (-(d % half).astype(jnp.float32) * (2.0 / ROT_DIM))
    pos = jnp.concatenate([offsets, jnp.repeat(bases, SUBLANES)]).astype(jnp.float32)
    ang = pos[:, None] * inv[None, :]
    return jnp.cos(ang), jnp.sin(ang)


def _cast_w_in_kernel(wt_ref, out_ref):
    row = lax.broadcasted_iota(jnp.int32, wt_ref.shape, 0) + pl.program_id(0) * CAST_BLOCK
    out_ref[...] = _bf(jnp.where(row < D_IN, wt_ref[...], 0.0).T)


def _cast_w_in(w_in_t):
    return pl.pallas_call(
        _cast_w_in_kernel,
        grid=(W_IN_ALLOC // CAST_BLOCK,),
        in_specs=[pl.BlockSpec((CAST_BLOCK, D_MODEL), lambda i: (i, 0))],
        out_specs=pl.BlockSpec((D_MODEL, CAST_BLOCK), lambda i: (0, i)),
        out_shape=jax.ShapeDtypeStruct((D_MODEL, W_IN_ALLOC), jnp.bfloat16),
        compiler_params=pltpu.CompilerParams(dimension_semantics=("arbitrary",)),
        name="cast_w_in",
    )(w_in_t)


def _const_spec(shape):
    zeros = (0,) * len(shape)
    return pl.BlockSpec(shape, lambda i: zeros, pipeline_mode=pl.Buffered(1))


def kernel(x_prompt, x_sample, cache_k, cache_v, state_gla, norm_pre_w, w_in, attn_sinks, w_gk_up, b_gk,
           gla_norm_w, w_out, norm_post_w):
    bsz, t_p, _ = x_prompt.shape
    dec_b, t_s, _ = x_sample.shape
    assert w_in.shape[0] == 1, "single layer"
    assert t_p % PROMPT_TILE == 0 and PROMPT_TILE % CHUNK == 0 and PROMPT_TILE >= WINDOW
    assert dec_b % SAMPLE_SEQS == 0 and t_s & (t_s - 1) == 0

    win = _cast_w_in(w_in[0].T)
    wout = w_out[0]
    wup = _bf(jnp.pad(w_gk_up[0], ((0, LANES - GLA_RANK), (0, 0))))
    wpre = norm_pre_w[0][None, :]
    wpost = norm_post_w[0][None, :]
    bgk = b_gk[0][None, :]
    gnw = gla_norm_w[0][None, :]
    sinks = attn_sinks[0][None, :]
    smem = pl.BlockSpec(memory_space=pltpu.SMEM)

    weights_specs = [
        _const_spec((1, D_MODEL)), _const_spec((D_MODEL, D_IN_PAD)), _const_spec((LANES, GQK_W)),
        _const_spec((1, GQK_W)), _const_spec((1, GLA_DV)), _const_spec((D_MODEL, D_MODEL)),
        _const_spec((1, D_MODEL))]

    tt = PROMPT_TILE
    nt = t_p // tt
    n_tiles = bsz * nt
    nb = SAMPLE_SEQS
    rows = nb * t_s
    assert tt % rows == 0 and (tt + rows) % SUBLANES == 0
    cos_all, sin_all = _rotary_angles(jnp.concatenate([jnp.arange(tt), jnp.tile(jnp.arange(t_s), nb)]),
                                      jnp.concatenate([jnp.arange(nt) * tt, jnp.array([PAST_LEN])]))
    base_blk = (tt + rows) // SUBLANES

    off_spec = pl.BlockSpec((rows, LANES), lambda i: (tt // rows, 0))
    base_spec = pl.BlockSpec((SUBLANES, LANES), lambda i: (base_blk + nt, 0))
    seq3 = lambda d1, d2: pl.BlockSpec((nb, d1, d2), lambda i: (i, 0, 0))
    tok = lambda d: pl.BlockSpec((rows, d), lambda i: (i, 0))
    kv_out = pl.BlockSpec((rows, N_KV_HEADS, HEAD_DIM), lambda i: (i, 0, 0))
    y_s, k_s, v_s, s_s = pl.pallas_call(
        _sample_kernel,
        grid=(dec_b // nb,),
        in_specs=[smem, tok(D_MODEL), off_spec, off_spec, base_spec, base_spec, seq3(KV_W, WINDOW), seq3(KV_W, WINDOW),
                  seq3(S_ROWS, GLA_DV)] + weights_specs,
        out_specs=[tok(D_MODEL), kv_out, kv_out, seq3(S_ROWS, GLA_DV)],
        out_shape=[jax.ShapeDtypeStruct((dec_b * t_s, D_MODEL), jnp.float32),
                   jax.ShapeDtypeStruct((dec_b * t_s, N_KV_HEADS, HEAD_DIM), jnp.float32),
                   jax.ShapeDtypeStruct((dec_b * t_s, N_KV_HEADS, HEAD_DIM), jnp.float32),
                   jax.ShapeDtypeStruct((dec_b, S_ROWS, GLA_DV), jnp.float32)],
        scratch_shapes=_slot_shapes(rows, nb * (WINDOW + t_s), nb)
        + [pltpu.VMEM((rows, D_MODEL), jnp.bfloat16), pltpu.VMEM((D_MODEL, D_MODEL), jnp.bfloat16)],
        compiler_params=pltpu.CompilerParams(dimension_semantics=("arbitrary",), vmem_limit_bytes=VMEM_LIMIT),
        name="sample_layer",
    )(sinks, x_sample.reshape(dec_b * t_s, D_MODEL), cos_all, sin_all, cos_all, sin_all,
      cache_k[0].transpose(0, 2, 3, 1).reshape(dec_b, KV_W, WINDOW),
      cache_v[0].transpose(0, 2, 3, 1).reshape(dec_b, KV_W, WINDOW),
      state_gla[0].reshape(dec_b, S_ROWS, GLA_DV), wpre, win, wup, bgk, gnw, wout, wpost)

    def tile_p(s):
        return jnp.minimum(s, n_tiles - 1)

    def tile_c(s):
        return jnp.maximum(s - 1, 0)

    off_spec = pl.BlockSpec((tt, LANES), lambda s: (0, 0))
    base_spec = pl.BlockSpec((SUBLANES, LANES), lambda s: (base_blk + tile_p(s) % nt, 0))
    x_spec = lambda tile: pl.BlockSpec((1, tt, D_MODEL), lambda s: (tile(s) // nt, tile(s) % nt, 0))
    seq_spec = lambda tile, d1, d2: pl.BlockSpec((1, d1, d2), lambda s: (tile(s) // nt, 0, 0))
    y_p, k_p, v_p, s_p = pl.pallas_call(
        functools.partial(_prompt_kernel, nt, n_tiles),
        grid=(n_tiles + 1,),
        in_specs=[smem, x_spec(tile_p), x_spec(tile_c), off_spec, off_spec, base_spec, base_spec] + weights_specs,
        out_specs=[x_spec(tile_c), seq_spec(tile_p, KV_W, WINDOW), seq_spec(tile_p, KV_W, WINDOW),
                   seq_spec(tile_c, S_ROWS, GLA_DV)],
        out_shape=[jax.ShapeDtypeStruct((bsz, t_p, D_MODEL), jnp.float32),
                   jax.ShapeDtypeStruct((bsz, KV_W, WINDOW), jnp.float32),
                   jax.ShapeDtypeStruct((bsz, KV_W, WINDOW), jnp.float32),
                   jax.ShapeDtypeStruct((bsz, S_ROWS, GLA_DV), jnp.float32)],
        scratch_shapes=_slot_shapes(tt, WINDOW + tt, tt // CHUNK) * 2
        + [pltpu.VMEM((S_ROWS, GLA_DV), jnp.float32), pltpu.VMEM((tt, D_MODEL), jnp.bfloat16),
           pltpu.VMEM((D_MODEL, D_MODEL), jnp.bfloat16)],
        compiler_params=pltpu.CompilerParams(dimension_semantics=("arbitrary",), vmem_limit_bytes=VMEM_LIMIT),
        name="prompt_layer",
    )(sinks, x_prompt, x_prompt, cos_all, sin_all, cos_all, sin_all, wpre, win, wup, bgk, gnw, wout, wpost)

    kv5 = lambda a, b, t: a.reshape(1, b, t, N_KV_HEADS, HEAD_DIM)
    kvt5 = lambda a: a.reshape(1, bsz, N_KV_HEADS, HEAD_DIM, WINDOW).transpose(0, 1, 4, 2, 3)
    st5 = lambda a, b: a.reshape(1, b, GLA_HEADS, GLA_DK, GLA_DV)
    return (y_p, y_s.reshape(dec_b, t_s, D_MODEL),
            kvt5(k_p), kvt5(v_p), st5(s_p, bsz),
            kv5(k_s, dec_b, t_s), kv5(v_s, dec_b, t_s), st5(s_s, dec_b))
```
